```python
import math
import jax, jax.numpy as jnp
from jax import lax
import numpy as np

D_MODEL = 2048
BATCH = 1
SEQ = 16384
DEPTH = 4

HEAD_DIM = 128
N_HEADS_A = D_MODEL // HEAD_DIM
N_HEADS_B = D_MODEL // HEAD_DIM
N_KV_GROUPS = 4
HEADS_PER_GROUP = N_HEADS_B // N_KV_GROUPS
D_FF = 4 * D_MODEL
N_A_LAYERS = DEPTH // 2
N_B_LAYERS = DEPTH - N_A_LAYERS
Q_BLOCK = 128
CMP_BLOCK = 32
CMP_STRIDE = 16
CMP_HIDDEN = 256
SEL_BLOCK = 64
N_SELECTED = 16
SEL_RATIO = SEL_BLOCK // CMP_STRIDE
WINDOW = 512
N_BUCKETS = 32
REL_MAX_DIST = 2048
ALPHA = (2.0 * DEPTH) ** 0.25
BETA = (8.0 * DEPTH) ** -0.25
LN_EPS = 1e-5
FORCED_SCORE = 1e4
NEG_BIG = -1e30

kernel_name = "yoco_fox_nsa_deepnorm_trunk"


def layer_norm(x, g, b):
    xf = x.astype(jnp.float32)
    mu = jnp.mean(xf, axis=-1, keepdims=True)
    var = jnp.mean(jnp.square(xf - mu), axis=-1, keepdims=True)
    y = (xf - mu) * lax.rsqrt(var + LN_EPS)
    return (y * g + b).astype(x.dtype)


def masked_softmax(logits, mask):
    logits = jnp.where(mask, logits, NEG_BIG)
    m = jnp.max(logits, axis=-1, keepdims=True)
    e = jnp.where(mask, jnp.exp(logits - m), 0.0)
    return e / jnp.maximum(jnp.sum(e, axis=-1, keepdims=True), 1e-30)


def rel_bucket(dist):
    n = jnp.maximum(dist, 0)
    exact = N_BUCKETS // 2
    nf = jnp.maximum(n, 1).astype(jnp.float32)
    large = exact + (jnp.log(nf / exact) / math.log(REL_MAX_DIST / exact) * (N_BUCKETS - exact)).astype(jnp.int32)
    large = jnp.minimum(large, N_BUCKETS - 1)
    return jnp.where(n < exact, n, large)


def sq_relu_mlp(x, w1, w2):
    return jnp.square(jax.nn.relu(x @ w1)) @ w2


def fox_attention(x, w_in, b_f, w_o):
    B, S, _ = x.shape
    H, dh = N_HEADS_A, HEAD_DIM
    proj = x @ w_in
    q = proj[..., :H * dh].reshape(B, S, H, dh).transpose(0, 2, 1, 3) * (dh ** -0.5)
    k = proj[..., H * dh:2 * H * dh].reshape(B, S, H, dh).transpose(0, 2, 1, 3)
    v = proj[..., 2 * H * dh:3 * H * dh].reshape(B, S, H, dh).transpose(0, 2, 1, 3)
    log_f = jax.nn.log_sigmoid((proj[..., 3 * H * dh:] + b_f).astype(jnp.float32))
    cum = jnp.cumsum(log_f, axis=1).transpose(0, 2, 1)
    key_pos = jnp.arange(S)

    def block(i):
        t0 = i * Q_BLOCK
        t = t0 + jnp.arange(Q_BLOCK)
        qb = lax.dynamic_slice_in_dim(q, t0, Q_BLOCK, axis=2)
        cb = lax.dynamic_slice_in_dim(cum, t0, Q_BLOCK, axis=2)
        logits = jnp.einsum('bhqd,bhkd->bhqk', qb, k).astype(jnp.float32) + (cb[..., :, None] - cum[..., None, :])
        p = masked_softmax(logits, key_pos[None, :] <= t[:, None])
        o = jnp.einsum('bhqk,bhkd->bqhd', p.astype(v.dtype), v)
        return o.reshape(B, Q_BLOCK, H * dh)

    out = lax.map(block, jnp.arange(S // Q_BLOCK))
    out = out.transpose(1, 0, 2, 3).reshape(B, S, H * dh)
    return out @ w_o


def compress_blocks(kv_raw, pos, w1, w2):
    B, G, S, dh = kv_raw.shape
    chunks = kv_raw.reshape(B, G, S // CMP_STRIDE, CMP_STRIDE, dh)
    blocks = jnp.concatenate([chunks[:, :, :-1], chunks[:, :, 1:]], axis=3) + pos
    flat = blocks.reshape(B, G, blocks.shape[2], CMP_BLOCK * dh)
    return jax.nn.gelu(flat @ w1) @ w2


def shared_kv(h, kv_w, cmp_pos_k, cmp_pos_v, cmp_k_w1, cmp_k_w2, cmp_v_w1, cmp_v_w2):
    B, S, _ = h.shape
    kv = (h @ kv_w).reshape(B, S, 6, N_KV_GROUPS, HEAD_DIM).transpose(2, 0, 3, 1, 4)
    k_cmp = compress_blocks(kv[0], cmp_pos_k, cmp_k_w1, cmp_k_w2)
    v_cmp = compress_blocks(kv[1], cmp_pos_v, cmp_v_w1, cmp_v_w2)
    return k_cmp, v_cmp, kv[2], kv[3], kv[4], kv[5]


def nsa_attention(x, w_in, w_o, rel_bias, k_cmp, v_cmp, k_sel, v_sel, k_win, v_win):
    B, S, _ = x.shape
    H, G, R, dh = N_HEADS_B, N_KV_GROUPS, HEADS_PER_GROUP, HEAD_DIM
    proj = x @ w_in
    q = proj[..., :H * dh].reshape(B, S, G, R, dh).transpose(0, 2, 3, 1, 4) * (dh ** -0.5)
    gates = jax.nn.sigmoid(proj[..., H * dh:].astype(jnp.float32)).reshape(B, S, H, 3).astype(x.dtype)
    n_cmp = k_cmp.shape[2]
    n_sel_blocks = S // SEL_BLOCK
    n_top = min(N_SELECTED, n_sel_blocks)
    table = rel_bias.T.reshape(G, R, N_BUCKETS)
    cmp_end = jnp.arange(n_cmp) * CMP_STRIDE + CMP_BLOCK - 1
    k_win_p = jnp.pad(k_win, ((0, 0), (0, 0), (WINDOW, 0), (0, 0)))
    v_win_p = jnp.pad(v_win, ((0, 0), (0, 0), (WINDOW, 0), (0, 0)))
    b_idx = jnp.arange(B)[:, None, None, None]
    g_idx = jnp.arange(G)[:, None, None, None][None]
    g_idx = g_idx.reshape(1, G, 1, 1)
    gi = jnp.arange(G).reshape(1, G, 1, 1, 1)
    ri = jnp.arange(R).reshape(1, 1, R, 1, 1)
    j_blk = jnp.arange(n_sel_blocks)

    def block(i):
        t0 = i * Q_BLOCK
        t = t0 + jnp.arange(Q_BLOCK)
        qb = lax.dynamic_slice_in_dim(q, t0, Q_BLOCK, axis=3)
        d_c = t[:, None] - cmp_end[None, :]
        lg = jnp.einsum('bgrqd,bgnd->bgrqn', qb, k_cmp).astype(jnp.float32) + table[:, :, rel_bucket(d_c)]
        p_cmp = masked_softmax(lg, d_c >= 0)
        o_cmp = jnp.einsum('bgrqn,bgnd->bgrqd', p_cmp.astype(v_cmp.dtype), v_cmp)
        imp = jnp.pad(jnp.sum(p_cmp, axis=2), ((0, 0), (0, 0), (0, 0), (1, 1)))
        imp_sel = imp[..., :-1].reshape(B, G, Q_BLOCK, n_sel_blocks, SEL_RATIO).sum(-1) + imp[..., SEL_RATIO::SEL_RATIO]
        blk_t = t // SEL_BLOCK
        forced = (j_blk[None, :] == 0) | (j_blk[None, :] == blk_t[:, None]) | (j_blk[None, :] == blk_t[:, None] - 1)
        valid = j_blk[None, :] <= blk_t[:, None]
        score = jnp.where(forced, FORCED_SCORE, jnp.where(valid, imp_sel, -1.0))
        _, top = lax.top_k(score, n_top)
        tok = (top[..., None] * SEL_BLOCK + jnp.arange(SEL_BLOCK)).reshape(B, G, Q_BLOCK, n_top * SEL_BLOCK)
        ks = k_sel[b_idx, g_idx, tok]
        vs = v_sel[b_idx, g_idx, tok]
        d_s = t[:, None] - tok
        lg = jnp.einsum('bgrqd,bgqld->bgrql', qb, ks).astype(jnp.float32) + table[gi, ri, rel_bucket(d_s)[:, :, None]]
        p = masked_softmax(lg, (d_s >= 0)[:, :, None])
        o_sel = jnp.einsum('bgrql,bgqld->bgrqd', p.astype(vs.dtype), vs)
        kw = lax.dynamic_slice_in_dim(k_win_p, t0, WINDOW + Q_BLOCK, axis=2)
        vw = lax.dynamic_slice_in_dim(v_win_p, t0, WINDOW + Q_BLOCK, axis=2)
        s = t0 - WINDOW + jnp.arange(WINDOW + Q_BLOCK)
        d_w = t[:, None] - s[None, :]
        mask_w = (s[None, :] >= 0) & (d_w >= 0) & (d_w < WINDOW)
        lg = jnp.einsum('bgrqd,bgkd->bgrqk', qb, kw).astype(jnp.float32) + table[:, :, rel_bucket(d_w)]
        p = masked_softmax(lg, mask_w)
        o_win = jnp.einsum('bgrqk,bgkd->bgrqd', p.astype(vw.dtype), vw)
        to_bqhd = lambda o: o.transpose(0, 3, 1, 2, 4).reshape(B, Q_BLOCK, H, dh)
        g = lax.dynamic_slice_in_dim(gates, t0, Q_BLOCK, axis=1)
        out = g[..., 0:1] * to_bqhd(o_cmp) + g[..., 1:2] * to_bqhd(o_sel) + g[..., 2:3] * to_bqhd(o_win)
        return out.reshape(B, Q_BLOCK, H * dh)

    out = lax.map(block, jnp.arange(S // Q_BLOCK))
    out = out.transpose(1, 0, 2, 3).reshape(B, S, H * dh)
    return out @ w_o


def setup_inputs(seed: int = 0) -> dict:
    key = jax.random.key(seed)
    ks = jax.random.split(key, 24)
    D, dh, G = D_MODEL, HEAD_DIM, N_KV_GROUPS
    nrm = lambda k, shape, scale: jax.random.normal(k, shape, jnp.float32) * scale
    x = nrm(ks[0], (BATCH, SEQ, D), 1.0)
    fox_w_in = jnp.concatenate([
        nrm(ks[1], (N_A_LAYERS, D, 2 * N_HEADS_A * dh), D ** -0.5),
        nrm(ks[2], (N_A_LAYERS, D, N_HEADS_A * dh), BETA * D ** -0.5),
        nrm(ks[3], (N_A_LAYERS, D, N_HEADS_A), D ** -0.5),
    ], axis=-1)
    fox_b_f = 2.0 + nrm(ks[4], (N_A_LAYERS, N_HEADS_A), 1.0)
    fox_w_o = nrm(ks[5], (N_A_LAYERS, D, D), BETA * D ** -0.5)
    nsa_w_in = nrm(ks[6], (N_B_LAYERS, D, N_HEADS_B * dh + 3 * N_HEADS_B), D ** -0.5)
    nsa_w_o = nrm(ks[7], (N_B_LAYERS, D, D), BETA * D ** -0.5)
    slot_scale = jnp.array([1.0, BETA, 1.0, BETA, 1.0, BETA], jnp.float32)[None, :, None]
    kv_w = (nrm(ks[8], (D, 6, G * dh), D ** -0.5) * slot_scale).reshape(D, 6 * G * dh)
    cmp_pos_k = nrm(ks[9], (CMP_BLOCK, dh), 0.1)
    cmp_pos_v = nrm(ks[10], (CMP_BLOCK, dh), 0.1)
    cmp_k_w1 = nrm(ks[11], (CMP_BLOCK * dh, CMP_HIDDEN), (CMP_BLOCK * dh) ** -0.5)
    cmp_k_w2 = nrm(ks[12], (CMP_HIDDEN, dh), CMP_HIDDEN ** -0.5)
    cmp_v_w1 = nrm(ks[13], (CMP_BLOCK * dh, CMP_HIDDEN), (CMP_BLOCK * dh) ** -0.5)
    cmp_v_w2 = nrm(ks[14], (CMP_HIDDEN, dh), CMP_HIDDEN ** -0.5)
    rel_bias = nrm(ks[15], (N_BUCKETS, N_HEADS_B), 0.5)
    mlp_w1 = nrm(ks[16], (DEPTH, D, D_FF), D ** -0.5)
    mlp_w2 = nrm(ks[17], (DEPTH, D_FF, D), BETA * D_FF ** -0.5)
    ln1_g = 1.0 + nrm(ks[18], (DEPTH, D), 0.02)
    ln1_b = nrm(ks[19], (DEPTH, D), 0.02)
    ln2_g = 1.0 + nrm(ks[20], (DEPTH, D), 0.02)
    ln2_b = nrm(ks[21], (DEPTH, D), 0.02)
    return {"x": x, "fox_w_in": fox_w_in, "fox_b_f": fox_b_f, "fox_w_o": fox_w_o,
            "nsa_w_in": nsa_w_in, "nsa_w_o": nsa_w_o, "kv_w": kv_w,
            "cmp_pos_k": cmp_pos_k, "cmp_pos_v": cmp_pos_v,
            "cmp_k_w1": cmp_k_w1, "cmp_k_w2": cmp_k_w2, "cmp_v_w1": cmp_v_w1, "cmp_v_w2": cmp_v_w2,
            "rel_bias": rel_bias, "mlp_w1": mlp_w1, "mlp_w2": mlp_w2,
            "ln1_g": ln1_g, "ln1_b": ln1_b, "ln2_g": ln2_g, "ln2_b": ln2_b}


def reference(x, fox_w_in, fox_b_f, fox_w_o, nsa_w_in, nsa_w_o, kv_w,
              cmp_pos_k, cmp_pos_v, cmp_k_w1, cmp_k_w2, cmp_v_w1, cmp_v_w2,
              rel_bias, mlp_w1, mlp_w2, ln1_g, ln1_b, ln2_g, ln2_b):
    h = x
    kv = None
    for layer in range(DEPTH):
        if layer < N_A_LAYERS:
            mix = fox_attention(h, fox_w_in[layer], fox_b_f[layer], fox_w_o[layer])
        else:
            b = layer - N_A_LAYERS
            mix = nsa_attention(h, nsa_w_in[b], nsa_w_o[b], rel_bias, *kv)
        h = layer_norm(ALPHA * h + mix, ln1_g[layer], ln1_b[layer])
        h = layer_norm(ALPHA * h + sq_relu_mlp(h, mlp_w1[layer], mlp_w2[layer]), ln2_g[layer], ln2_b[layer])
        if layer == N_A_LAYERS - 1:
            kv = shared_kv(h, kv_w, cmp_pos_k, cmp_pos_v, cmp_k_w1, cmp_k_w2, cmp_v_w1, cmp_v_w2)
    return h
```

```python
import functools
import math

import numpy as np
import jax
import jax.numpy as jnp
from jax import lax
from jax.experimental import pallas as pl
from jax.experimental.pallas import tpu as pltpu

D_MODEL = 2048
DEPTH = 4
HEAD_DIM = 128
N_HEADS = D_MODEL // HEAD_DIM
N_KV_GROUPS = 4
HEADS_PER_GROUP = N_HEADS // N_KV_GROUPS
D_FF = 4 * D_MODEL
N_A_LAYERS = DEPTH // 2
Q_BLOCK = 128
CMP_BLOCK = 32
CMP_STRIDE = 16
CMP_HIDDEN = 256
SEL_BLOCK = 64
N_SELECTED = 16
SEL_RATIO = SEL_BLOCK // CMP_STRIDE
WINDOW = 512
N_BUCKETS = 32
REL_MAX_DIST = 2048
ALPHA = (2.0 * DEPTH) ** 0.25
LN_EPS = 1e-5
FORCED_SCORE = 1e4
NEG_BIG = -1e30

LANES = 128
VMEM_LIMIT = 56 * 1024 * 1024
BF16 = jnp.bfloat16
F32 = jnp.float32

_EXACT = N_BUCKETS // 2
_BUCKET_THRESHOLDS = tuple(
    int(math.ceil(_EXACT * (REL_MAX_DIST / _EXACT) ** (k / (N_BUCKETS - _EXACT))))
    for k in range(1, N_BUCKETS - _EXACT))
BIAS_RANGE = 2048
NEAR_CHUNKS = BIAS_RANGE // LANES
FAR_TILE = 512
NEAR_SPAN = 1536
WIN_CHUNKS = WINDOW // Q_BLOCK + 1
CMP_TABLES = 32


def _nt_dot(a, b):
    return lax.dot_general(a, b, (((1,), (1,)), ((), ())), preferred_element_type=F32)


def _params(sem):
    return pltpu.CompilerParams(dimension_semantics=sem, vmem_limit_bytes=VMEM_LIMIT)


def _layer_norm(y, g, b):
    mu = jnp.mean(y, axis=-1, keepdims=True)
    yc = y - mu
    var = jnp.mean(yc * yc, axis=-1, keepdims=True)
    return yc * lax.rsqrt(var + LN_EPS) * g + b


def _matmul_kernel(x_ref, w_ref, o_ref, *, tn, scaled_cols, scale, head_major):
    acc = jnp.dot(x_ref[...], w_ref[...], preferred_element_type=F32)
    if scaled_cols:
        j = pl.program_id(1)
        acc = acc * jnp.where(j * tn < scaled_cols, scale, 1.0).astype(F32)
    if head_major:
        for c in range(tn // LANES):
            o_ref[c] = acc[:, c * LANES:(c + 1) * LANES].astype(o_ref.dtype)
    else:
        o_ref[...] = acc.astype(o_ref.dtype)


def _matmul(x, w, *, out_dtype, head_major, tm, tn, scaled_cols=0, scale=1.0):
    m, k = x.shape
    n = w.shape[1]
    assert m % tm == 0 and n % tn == 0 and scaled_cols % tn == 0
    if head_major:
        out_shape = jax.ShapeDtypeStruct((n // LANES, m, LANES), out_dtype)
        out_spec = pl.BlockSpec((tn // LANES, tm, LANES), lambda i, j: (j, i, 0))
    else:
        out_shape = jax.ShapeDtypeStruct((m, n), out_dtype)
        out_spec = pl.BlockSpec((tm, tn), lambda i, j: (i, j))
    return pl.pallas_call(
        functools.partial(_matmul_kernel, tn=tn, scaled_cols=scaled_cols, scale=scale,
                          head_major=head_major),
        grid=(m // tm, n // tn),
        in_specs=[pl.BlockSpec((tm, k), lambda i, j: (i, 0)),
                  pl.BlockSpec((k, tn), lambda i, j: (0, j))],
        out_specs=out_spec,
        out_shape=out_shape,
        compiler_params=_params(("parallel", "arbitrary")),
        name="proj_matmul",
    )(x, w)


def _cum_kernel(gl_ref, b_ref, o_ref, *, nblk):
    r = lax.broadcasted_iota(jnp.int32, (LANES, LANES), 0)
    c = lax.broadcasted_iota(jnp.int32, (LANES, LANES), 1)
    tri = (c <= r).astype(F32)
    bias = b_ref[...]

    def body(i, carry):
        rows = pl.ds(pl.multiple_of(i * LANES, LANES), LANES)
        lf = jax.nn.log_sigmoid(gl_ref[rows, :] + bias)
        cum = jnp.dot(tri, lf, preferred_element_type=F32,
                      precision=lax.Precision.HIGHEST) + carry
        o_ref[rows, :] = cum
        return cum[LANES - 1:LANES, :]

    lax.fori_loop(0, nblk, body, jnp.zeros((1, LANES), F32))


def _forget_cumsum(gate_logits, b_f):
    s = gate_logits.shape[0]
    bias = jnp.zeros((1, LANES), F32).at[0, :N_HEADS].set(b_f)
    return pl.pallas_call(
        functools.partial(_cum_kernel, nblk=s // LANES),
        out_shape=jax.ShapeDtypeStruct((s, LANES), F32),
        compiler_params=pltpu.CompilerParams(vmem_limit_bytes=VMEM_LIMIT),
        name="forget_cumsum",
    )(gate_logits, bias)


def _online_update(s, v, m_ref, l_ref, acc_ref):
    m_old = m_ref[...]
    m_new = jnp.maximum(m_old, jnp.max(s, axis=1, keepdims=True))
    alpha = jnp.exp(m_old - m_new)
    p = jnp.exp(s - m_new)
    l_ref[...] = alpha * l_ref[...] + jnp.sum(p, axis=1, keepdims=True)
    acc_ref[...] = alpha * acc_ref[...] + jnp.dot(p.astype(BF16), v, preferred_element_type=F32)
    m_ref[...] = m_new


def _fox_kernel(q_ref, k_ref, v_ref, c_ref, o_ref, m_ref, l_ref, acc_ref, *, t):
    i = pl.program_id(1)
    q = q_ref[...]
    c0 = jnp.max(c_ref[i], axis=1, keepdims=True)
    m_ref[...] = jnp.full_like(m_ref, NEG_BIG)
    l_ref[...] = jnp.zeros_like(l_ref)
    acc_ref[...] = jnp.zeros_like(acc_ref)

    def tile(j, masked):
        rows = pl.ds(pl.multiple_of(j * t, t), t)
        s = _nt_dot(q, k_ref[rows, :]) + (c0 - c_ref[j])
        if masked:
            r = lax.broadcasted_iota(jnp.int32, (t, t), 0)
            c = lax.broadcasted_iota(jnp.int32, (t, t), 1)
            s = jnp.where(c <= r, s, NEG_BIG)
        _online_update(s, v_ref[rows, :], m_ref, l_ref, acc_ref)

    tile(i, True)

    def body(j, carry):
        tile(j, False)
        return carry

    lax.fori_loop(0, i, body, 0)
    o_ref[...] = (acc_ref[...] / l_ref[...]).astype(o_ref.dtype)


def _fox_attention(qkv, cum_rows, *, t):
    _, s, _ = qkv.shape
    h = N_HEADS
    return pl.pallas_call(
        functools.partial(_fox_kernel, t=t),
        grid=(h, s // t),
        in_specs=[pl.BlockSpec((None, t, HEAD_DIM), lambda hh, i: (hh, i, 0)),
                  pl.BlockSpec((None, s, HEAD_DIM), lambda hh, i: (h + hh, 0, 0)),
                  pl.BlockSpec((None, s, HEAD_DIM), lambda hh, i: (2 * h + hh, 0, 0)),
                  pl.BlockSpec((None, s // t, 1, t), lambda hh, i: (hh, 0, 0, 0))],
        out_specs=pl.BlockSpec((t, HEAD_DIM), lambda hh, i: (i, hh)),
        out_shape=jax.ShapeDtypeStruct((s, h * HEAD_DIM), BF16),
        scratch_shapes=[pltpu.VMEM((t, 1), F32), pltpu.VMEM((t, 1), F32),
                        pltpu.VMEM((t, HEAD_DIM), F32)],
        compiler_params=_params(("parallel", "arbitrary")),
        name="fox_attention",
    )(qkv, qkv, qkv, cum_rows)


def _emit_norm(y, g_ref, b_ref, of_ref, ob_ref):
    hn = _layer_norm(y, g_ref[...], b_ref[...])
    of_ref[...] = hn
    ob_ref[...] = hn.astype(BF16)


def _oproj_kernel(a_ref, w_ref, h_ref, g_ref, b_ref, of_ref, ob_ref):
    mix = jnp.dot(a_ref[...], w_ref[...], preferred_element_type=F32)
    _emit_norm(ALPHA * h_ref[...] + mix, g_ref, b_ref, of_ref, ob_ref)


def _nsa_oproj_kernel(oc_ref, os_ref, ow_ref, gl_ref, w_ref, h_ref, g_ref, b_ref,
                      of_ref, ob_ref, x_ref):
    gates = jax.nn.sigmoid(gl_ref[...])
    for hh in range(N_HEADS):
        cols = slice(hh * HEAD_DIM, (hh + 1) * HEAD_DIM)
        x = (gates[:, 3 * hh:3 * hh + 1] * oc_ref[:, cols]
             + gates[:, 3 * hh + 1:3 * hh + 2] * os_ref[:, cols]
             + gates[:, 3 * hh + 2:3 * hh + 3] * ow_ref[:, cols])
        x_ref[:, cols] = x.astype(BF16)
    mix = jnp.dot(x_ref[...], w_ref[...], preferred_element_type=F32)
    _emit_norm(ALPHA * h_ref[...] + mix, g_ref, b_ref, of_ref, ob_ref)


def _row_spec(tm, n):
    return pl.BlockSpec((tm, n), lambda i: (i, 0))


def _const_spec(shape):
    return pl.BlockSpec(shape, lambda i: (0,) * len(shape))


def _norm_outs(s, tm):
    d = D_MODEL
    return dict(
        out_specs=[_row_spec(tm, d), _row_spec(tm, d)],
        out_shape=[jax.ShapeDtypeStruct((s, d), F32), jax.ShapeDtypeStruct((s, d), BF16)])


def _oproj(a, w, h, g, b, *, tm):
    s, d = h.shape
    return pl.pallas_call(
        _oproj_kernel,
        grid=(s // tm,),
        in_specs=[_row_spec(tm, d), _const_spec((d, d)), _row_spec(tm, d),
                  _const_spec((1, d)), _const_spec((1, d))],
        compiler_params=_params(("parallel",)),
        name="oproj_norm",
        **_norm_outs(s, tm),
    )(a, w, h, g, b)


def _nsa_oproj(o_cmp, o_sel, o_win, gate_logits, w, h, g, b, *, tm):
    s, d = h.shape
    return pl.pallas_call(
        _nsa_oproj_kernel,
        grid=(s // tm,),
        in_specs=[_row_spec(tm, d), _row_spec(tm, d), _row_spec(tm, d), _row_spec(tm, LANES),
                  _const_spec((d, d)), _row_spec(tm, d), _const_spec((1, d)), _const_spec((1, d))],
        scratch_shapes=[pltpu.VMEM((tm, d), BF16)],
        compiler_params=_params(("parallel",)),
        name="nsa_oproj_norm",
        **_norm_outs(s, tm),
    )(o_cmp, o_sel, o_win, gate_logits, w, h, g, b)


def _mlp_kernel(x_ref, w1_ref, w2_ref, h_ref, g_ref, b_ref, of_ref, ob_ref, acc_ref):
    f = pl.program_id(1)

    @pl.when(f == 0)
    def _():
        acc_ref[...] = jnp.zeros_like(acc_ref)

    u = jnp.maximum(jnp.dot(x_ref[...], w1_ref[...], preferred_element_type=F32), 0.0)
    acc_ref[...] += jnp.dot((u * u).astype(BF16), w2_ref[...], preferred_element_type=F32)

    @pl.when(f == pl.num_programs(1) - 1)
    def _():
        _emit_norm(ALPHA * h_ref[...] + acc_ref[...], g_ref, b_ref, of_ref, ob_ref)


def _mlp(xb, w1, w2, h, g, b, *, tm, tf):
    s, d = h.shape
    ff = w1.shape[1]
    row = lambda n: pl.BlockSpec((tm, n), lambda i, f: (i, 0))
    vec = pl.BlockSpec((1, d), lambda i, f: (0, 0))
    return pl.pallas_call(
        _mlp_kernel,
        grid=(s // tm, ff // tf),
        in_specs=[row(d), pl.BlockSpec((d, tf), lambda i, f: (0, f)),
                  pl.BlockSpec((tf, d), lambda i, f: (f, 0)), row(d), vec, vec],
        out_specs=[row(d), row(d)],
        out_shape=[jax.ShapeDtypeStruct((s, d), F32), jax.ShapeDtypeStruct((s, d), BF16)],
        scratch_shapes=[pltpu.VMEM((tm, d), F32)],
        compiler_params=_params(("parallel", "arbitrary")),
        name="mlp_norm",
    )(xb, w1, w2, h, g, b)


def _compress_kernel(x_ref, pos_ref, w1_ref, w2_ref, o_ref):
    half = CMP_STRIDE * HEAD_DIM
    x = x_ref[...]
    first = jnp.dot((x + pos_ref[:, :half]).astype(BF16), w1_ref[:half, :],
                    preferred_element_type=F32)
    second = jnp.dot((x + pos_ref[:, half:]).astype(BF16), w1_ref[half:, :],
                     preferred_element_type=F32)
    n = x.shape[0]
    hidden = first + pltpu.roll(second, n - 1, 0)
    act = jax.nn.gelu(hidden)
    o_ref[...] = jnp.dot(act.astype(BF16), w2_ref[...], preferred_element_type=F32).astype(BF16)


def _compress(kv_raw, pos, w1, w2):
    g = N_KV_GROUPS
    s = kv_raw.shape[1]
    nc = s // CMP_STRIDE
    wide = CMP_STRIDE * HEAD_DIM
    x = kv_raw.reshape(2 * g, nc, wide)
    return pl.pallas_call(
        _compress_kernel,
        grid=(2, g),
        in_specs=[pl.BlockSpec((None, nc, wide), lambda a, gg: (a * g + gg, 0, 0)),
                  pl.BlockSpec((None, 1, 2 * wide), lambda a, gg: (a, 0, 0)),
                  pl.BlockSpec((None, 2 * wide, CMP_HIDDEN), lambda a, gg: (a, 0, 0)),
                  pl.BlockSpec((None, CMP_HIDDEN, HEAD_DIM), lambda a, gg: (a, 0, 0))],
        out_specs=pl.BlockSpec((None, None, nc, HEAD_DIM), lambda a, gg: (a, gg, 0, 0)),
        out_shape=jax.ShapeDtypeStruct((2, g, nc, HEAD_DIM), BF16),
        compiler_params=_params(("parallel", "parallel")),
        name="compress_kv",
    )(x, pos, w1, w2)


def _bias_vec_kernel(t_ref, o_ref):
    d = lax.broadcasted_iota(jnp.int32, (1, BIAS_RANGE), 1)
    large = jnp.full_like(d, _EXACT)
    for thr in _BUCKET_THRESHOLDS:
        large = large + (d >= thr).astype(jnp.int32)
    bucket = jnp.where(d < _EXACT, d, large)
    table = t_ref[...]
    acc = jnp.zeros((N_HEADS, BIAS_RANGE), F32)
    for bkt in range(N_BUCKETS):
        acc = jnp.where(bucket == bkt, table[:, bkt:bkt + 1], acc)
    o_ref[...] = acc - table[:, N_BUCKETS - 1:N_BUCKETS]


def _bias_vec(rel_bias):
    return pl.pallas_call(
        _bias_vec_kernel,
        out_shape=jax.ShapeDtypeStruct((N_HEADS, BIAS_RANGE), F32),
        name="rel_bias_by_distance",
    )(rel_bias.T)


def _expand_bias(vec, dist, lo_fill, hi_limit):
    n = dist.shape[0]
    idx = jnp.asarray(np.clip(dist, 0, BIAS_RANGE - 1).astype(np.int32))
    vals = vec[:, idx]
    if hi_limit is None:
        vals = jnp.where(jnp.asarray(dist >= BIAS_RANGE), 0.0, vals)
    else:
        vals = jnp.where(jnp.asarray(dist >= hi_limit), NEG_BIG, vals)
    vals = jnp.where(jnp.asarray(dist < 0), lo_fill, vals)
    vals = vals.reshape(N_KV_GROUPS, HEADS_PER_GROUP, n, Q_BLOCK, LANES)
    return vals.transpose(0, 2, 1, 3, 4).reshape(N_KV_GROUPS, n, HEADS_PER_GROUP * Q_BLOCK, LANES)


def _bias_tables(vec):
    q = np.arange(Q_BLOCK)[None, :, None]
    k = np.arange(LANES)[None, None, :]
    j = np.arange(NEAR_CHUNKS)[:, None, None]
    d_tok = LANES * j + q - k
    e = np.arange(CMP_TABLES)[:, None, None]
    d_cmp = Q_BLOCK * e + q - CMP_STRIDE * k - (CMP_BLOCK - 1)
    t_sel = _expand_bias(vec, d_tok, NEG_BIG, None)
    t_win = _expand_bias(vec, d_tok[:WIN_CHUNKS], NEG_BIG, WINDOW)
    t_cmp = _expand_bias(vec, d_cmp, NEG_BIG, None)
    return t_sel, t_win, t_cmp


def _cmp_kernel(q_ref, kc_ref, vc_ref, t0_ref, t1_ref, a_ref, o_ref, sel_ref, *, n_top):
    i = pl.program_id(1)
    r4 = HEADS_PER_GROUP
    rows = r4 * Q_BLOCK
    ncp = kc_ref.shape[0]
    nselp = a_ref.shape[1]
    chunks_per_tile = Q_BLOCK // CMP_STRIDE
    q = q_ref[...].reshape(rows, HEAD_DIM)
    lg = _nt_dot(q, kc_ref[...])
    c_hi = (i * chunks_per_tile) // LANES
    chunk = lax.broadcasted_iota(jnp.int32, (1, ncp), 1) >> (LANES.bit_length() - 1)
    reps = ncp // LANES
    b0 = jnp.concatenate([t0_ref[...]] * reps, axis=1)
    b1 = jnp.concatenate([t1_ref[...]] * reps, axis=1)
    bias = jnp.where(chunk == c_hi, b0,
                     jnp.where(chunk == c_hi - 1, b1,
                               jnp.where(chunk > c_hi, NEG_BIG, 0.0)))
    lg = lg + bias
    valid = lg > 0.5 * NEG_BIG
    m = jnp.max(lg, axis=1, keepdims=True)
    e = jnp.where(valid, jnp.exp(lg - m), 0.0)
    p = e / jnp.maximum(jnp.sum(e, axis=1, keepdims=True), 1e-30)
    o = jnp.dot(p.astype(BF16), vc_ref[...], preferred_element_type=F32)
    for r in range(r4):
        o_ref[:, r * HEAD_DIM:(r + 1) * HEAD_DIM] = o[r * Q_BLOCK:(r + 1) * Q_BLOCK, :]

    imp = p[0:Q_BLOCK]
    for r in range(1, r4):
        imp = imp + p[r * Q_BLOCK:(r + 1) * Q_BLOCK]
    imp_sel = jnp.dot(imp, a_ref[...], preferred_element_type=F32,
                      precision=lax.Precision.HIGHEST)
    t = i * Q_BLOCK + lax.broadcasted_iota(jnp.int32, (Q_BLOCK, 1), 0)
    blk_t = t >> (SEL_BLOCK.bit_length() - 1)
    j = lax.broadcasted_iota(jnp.int32, (1, nselp), 1)
    forced = (j == 0) | (j == blk_t) | (j == blk_t - 1)
    score = jnp.where(forced, FORCED_SCORE, jnp.where(j <= blk_t, imp_sel, -1.0))
    jf = j.astype(F32)

    def pick(_, carry):
        score, keep = carry
        top = jnp.max(score, axis=1, keepdims=True)
        first = jnp.min(jnp.where(score == top, jf, float(nselp)), axis=1, keepdims=True)
        hit = jf == first
        return jnp.where(hit, -2.0, score), jnp.where(hit, 0.0, keep)

    _, keep = lax.fori_loop(0, n_top, pick, (score, jnp.full((Q_BLOCK, nselp), NEG_BIG, F32)))
    sel_ref[...] = keep.astype(BF16)


def _cmp_select(q, kv_cmp, t_cmp, agg, *, n_top):
    _, s, _ = q.shape
    g, r4 = N_KV_GROUPS, HEADS_PER_GROUP
    ncp = kv_cmp.shape[2]
    nselp = agg.shape[1]
    rows = r4 * Q_BLOCK
    tiles_per_chunk = LANES * CMP_STRIDE // Q_BLOCK
    return pl.pallas_call(
        functools.partial(_cmp_kernel, n_top=n_top),
        grid=(g, s // Q_BLOCK),
        in_specs=[pl.BlockSpec((r4, Q_BLOCK, HEAD_DIM), lambda gg, i: (gg, i, 0)),
                  pl.BlockSpec((None, None, ncp, HEAD_DIM), lambda gg, i: (0, gg, 0, 0)),
                  pl.BlockSpec((None, None, ncp, HEAD_DIM), lambda gg, i: (1, gg, 0, 0)),
                  pl.BlockSpec((None, None, rows, LANES),
                               lambda gg, i: (gg, i % tiles_per_chunk, 0, 0)),
                  pl.BlockSpec((None, None, rows, LANES),
                               lambda gg, i: (gg, i % tiles_per_chunk + tiles_per_chunk, 0, 0)),
                  pl.BlockSpec((ncp, nselp), lambda gg, i: (0, 0))],
        out_specs=[pl.BlockSpec((Q_BLOCK, rows), lambda gg, i: (i, gg)),
                   pl.BlockSpec((None, Q_BLOCK, nselp), lambda gg, i: (gg, i, 0))],
        out_shape=[jax.ShapeDtypeStruct((s, N_HEADS * HEAD_DIM), F32),
                   jax.ShapeDtypeStruct((g, s, nselp), BF16)],
        compiler_params=_params(("parallel", "arbitrary")),
        name="nsa_compressed_select",
    )(q, kv_cmp, kv_cmp, t_cmp, t_cmp, agg)


def _finish(o_ref, l_ref, acc_ref):
    o = acc_ref[...] / l_ref[...]
    for r in range(HEADS_PER_GROUP):
        o_ref[:, r * HEAD_DIM:(r + 1) * HEAD_DIM] = o[r * Q_BLOCK:(r + 1) * Q_BLOCK, :]


def _reset(m_ref, l_ref, acc_ref):
    m_ref[...] = jnp.full_like(m_ref, NEG_BIG)
    l_ref[...] = jnp.zeros_like(l_ref)
    acc_ref[...] = jnp.zeros_like(acc_ref)


def _sel_win_kernel(*refs):
    (q_ref, sb_ref, ka_ref, vs_ref, ts_ref, tw_ref) = refs[:6]
    kw_refs = refs[6:6 + WIN_CHUNKS]
    vw_refs = refs[6 + WIN_CHUNKS:6 + 2 * WIN_CHUNKS]
    os_ref, ow_ref, qa_ref, m_ref, l_ref, acc_ref = refs[6 + 2 * WIN_CHUNKS:]
    i = pl.program_id(1)
    r4 = HEADS_PER_GROUP
    rows = r4 * Q_BLOCK
    q = q_ref[...].reshape(rows, HEAD_DIM)
    keys_per_chunk = LANES * SEL_BLOCK
    for c in range(qa_ref.shape[0]):
        qa_ref[c, :, :HEAD_DIM] = q
        sb = sb_ref[:, c * LANES:(c + 1) * LANES]
        for r in range(r4):
            qa_ref[c, r * Q_BLOCK:(r + 1) * Q_BLOCK, HEAD_DIM:] = sb

    _reset(m_ref, l_ref, acc_ref)
    t0 = i * Q_BLOCK
    n_far = jnp.maximum(t0 - NEAR_SPAN, 0) // FAR_TILE
    far_end = n_far * FAR_TILE

    def far(j, carry):
        start = pl.multiple_of(j * FAR_TILE, FAR_TILE)
        s = _nt_dot(qa_ref[start // keys_per_chunk], ka_ref[pl.ds(start, FAR_TILE), :])
        _online_update(s, vs_ref[pl.ds(start, FAR_TILE), :], m_ref, l_ref, acc_ref)
        return carry

    lax.fori_loop(0, n_far, far, 0)

    def near(n, carry):
        start = pl.multiple_of(far_end + n * LANES, LANES)
        behind = (t0 - start) // LANES
        s = _nt_dot(qa_ref[start // keys_per_chunk], ka_ref[pl.ds(start, LANES), :]) + ts_ref[behind]
        _online_update(s, vs_ref[pl.ds(start, LANES), :], m_ref, l_ref, acc_ref)
        return carry

    lax.fori_loop(0, (t0 + Q_BLOCK - far_end) // LANES, near, 0)
    _finish(os_ref, l_ref, acc_ref)

    _reset(m_ref, l_ref, acc_ref)
    for behind in range(WIN_CHUNKS):
        before_start = jnp.where(i >= behind, 0.0, NEG_BIG).astype(F32)
        s = _nt_dot(q, kw_refs[behind][...]) + (tw_ref[behind] + before_start)
        _online_update(s, vw_refs[behind][...], m_ref, l_ref, acc_ref)
    _finish(ow_ref, l_ref, acc_ref)


def _sel_win(q, selbias, k_aug, kv, t_sel, t_win):
    _, s, _ = q.shape
    g, r4 = N_KV_GROUPS, HEADS_PER_GROUP
    rows = r4 * Q_BLOCK
    nselp = selbias.shape[2]
    whole = lambda slot: pl.BlockSpec((None, s, HEAD_DIM), lambda gg, i: (slot * g + gg, 0, 0))

    def win(slot, behind):
        return pl.BlockSpec((None, Q_BLOCK, HEAD_DIM),
                            lambda gg, i: (slot * g + gg, jnp.maximum(i - behind, 0), 0))

    in_specs = [pl.BlockSpec((r4, Q_BLOCK, HEAD_DIM), lambda gg, i: (gg, i, 0)),
                pl.BlockSpec((None, Q_BLOCK, nselp), lambda gg, i: (gg, i, 0)),
                pl.BlockSpec((None, s, 2 * HEAD_DIM), lambda gg, i: (gg, 0, 0)),
                whole(1),
                pl.BlockSpec((None, NEAR_CHUNKS, rows, LANES), lambda gg, i: (gg, 0, 0, 0)),
                pl.BlockSpec((None, WIN_CHUNKS, rows, LANES), lambda gg, i: (gg, 0, 0, 0))]
    in_specs += [win(2, b) for b in range(WIN_CHUNKS)] + [win(3, b) for b in range(WIN_CHUNKS)]
    out = pl.BlockSpec((Q_BLOCK, rows), lambda gg, i: (i, gg))
    return pl.pallas_call(
        _sel_win_kernel,
        grid=(g, s // Q_BLOCK),
        in_specs=in_specs,
        out_specs=[out, out],
        out_shape=[jax.ShapeDtypeStruct((s, N_HEADS * HEAD_DIM), F32)] * 2,
        scratch_shapes=[pltpu.VMEM((nselp // LANES, rows, 2 * HEAD_DIM), BF16),
                        pltpu.VMEM((rows, 1), F32), pltpu.VMEM((rows, 1), F32),
                        pltpu.VMEM((rows, HEAD_DIM), F32)],
        compiler_params=_params(("parallel", "arbitrary")),
        name="nsa_selected_window",
    )(q, selbias, k_aug, kv, t_sel, t_win, *([kv] * (2 * WIN_CHUNKS)))


def _selection_aggregator(ncp, nselp):
    n = np.arange(ncp)[:, None]
    j = np.arange(nselp)[None, :]
    hit = (n >= SEL_RATIO * j - 1) & (n <= SEL_RATIO * j + SEL_RATIO - 1) & (n < ncp - 1)
    return jnp.asarray(hit.astype(np.float32))


def _block_onehot(s):
    blk = (np.arange(s) // SEL_BLOCK) % LANES
    return jnp.asarray((blk[:, None] == np.arange(LANES)[None, :]).astype(np.float32), dtype=BF16)


def _round_up(x, m):
    return (x + m - 1) // m * m


def kernel(x, fox_w_in, fox_b_f, fox_w_o, nsa_w_in, nsa_w_o, kv_w, cmp_pos_k, cmp_pos_v, cmp_k_w1, cmp_k_w2, cmp_v_w1, cmp_v_w2, rel_bias, mlp_w1, mlp_w2, ln1_g, ln1_b, ln2_g, ln2_b):
    b, s, d = x.shape
    assert b == 1 and d == D_MODEL and s % (CMP_STRIDE * LANES) == 0
    hd = N_HEADS * HEAD_DIM
    scale = HEAD_DIM ** -0.5
    tm = min(512, s)
    fox_t = min(512, s)
    pad_cols = lambda w: jnp.pad(w, ((0, 0), (0, LANES - w.shape[1])))

    h = x[0]
    hb = h.astype(BF16)
    kv_state = None
    for layer in range(DEPTH):
        g1, b1 = ln1_g[layer][None, :], ln1_b[layer][None, :]
        g2, b2 = ln2_g[layer][None, :], ln2_b[layer][None, :]
        if layer < N_A_LAYERS:
            w_in = fox_w_in[layer]
            qkv = _matmul(hb, w_in[:, :3 * hd].astype(BF16), out_dtype=BF16, head_major=True,
                          tm=tm, tn=512, scaled_cols=hd, scale=scale)
            gate_logits = _matmul(hb, pad_cols(w_in[:, 3 * hd:]).astype(BF16), out_dtype=F32,
                                  head_major=False, tm=tm, tn=LANES)
            cum = _forget_cumsum(gate_logits, fox_b_f[layer])
            cum_rows = cum[:, :N_HEADS].T.reshape(N_HEADS, s // fox_t, 1, fox_t)
            attn = _fox_attention(qkv, cum_rows, t=fox_t)
            h, hb = _oproj(attn, fox_w_o[layer].astype(BF16), h, g1, b1, tm=tm)
        else:
            k_cmp_v_cmp, kv_tok, k_aug, tables, agg, n_top = kv_state
            t_sel, t_win, t_cmp = tables
            w_in = nsa_w_in[layer - N_A_LAYERS]
            q = _matmul(hb, w_in[:, :hd].astype(BF16), out_dtype=BF16, head_major=True,
                        tm=tm, tn=512, scaled_cols=hd, scale=scale)
            gate_logits = _matmul(hb, pad_cols(w_in[:, hd:]).astype(BF16), out_dtype=F32,
                                  head_major=False, tm=tm, tn=LANES)
            o_cmp, selbias = _cmp_select(q, k_cmp_v_cmp, t_cmp, agg, n_top=n_top)
            o_sel, o_win = _sel_win(q, selbias, k_aug, kv_tok, t_sel, t_win)
            h, hb = _nsa_oproj(o_cmp, o_sel, o_win, gate_logits,
                               nsa_w_o[layer - N_A_LAYERS].astype(BF16), h, g1, b1, tm=min(256, s))
        h, hb = _mlp(hb, mlp_w1[layer].astype(BF16), mlp_w2[layer].astype(BF16), h, g2, b2,
                     tm=tm, tf=512)
        if layer == N_A_LAYERS - 1:
            gd = N_KV_GROUPS * HEAD_DIM
            kvw = kv_w.astype(BF16)
            kv_raw = _matmul(hb, kvw[:, :2 * gd], out_dtype=F32, head_major=True, tm=tm, tn=512)
            kv_tok = _matmul(hb, kvw[:, 2 * gd:], out_dtype=BF16, head_major=True, tm=tm, tn=512)
            pos = jnp.stack([cmp_pos_k.reshape(1, -1), cmp_pos_v.reshape(1, -1)])
            w1 = jnp.stack([cmp_k_w1, cmp_v_w1]).astype(BF16)
            w2 = jnp.stack([cmp_k_w2, cmp_v_w2]).astype(BF16)
            kv_cmp = _compress(kv_raw, pos, w1, w2)
            n_sel = s // SEL_BLOCK
            nselp = _round_up(n_sel, LANES)
            onehot = jnp.broadcast_to(_block_onehot(s), (N_KV_GROUPS, s, LANES))
            k_aug = jnp.concatenate([kv_tok[:N_KV_GROUPS], onehot], axis=-1)
            tables = _bias_tables(_bias_vec(rel_bias))
            agg = _selection_aggregator(s // CMP_STRIDE, nselp)
            kv_state = (kv_cmp, kv_tok, k_aug, tables, agg, min(N_SELECTED, n_sel))
    return h[None]
```

```python
import functools
import math

import numpy as np
import jax
import jax.numpy as jnp
from jax import lax
from jax.experimental import pallas as pl
from jax.experimental.pallas import tpu as pltpu

D_MODEL = 2048
DEPTH = 4
HEAD_DIM = 128
N_HEADS = D_MODEL // HEAD_DIM
N_KV_GROUPS = 4
HEADS_PER_GROUP = N_HEADS // N_KV_GROUPS
D_FF = 4 * D_MODEL
N_A_LAYERS = DEPTH // 2
Q_BLOCK = 128
CMP_BLOCK = 32
CMP_STRIDE = 16
CMP_HIDDEN = 256
SEL_BLOCK = 64
N_SELECTED = 16
SEL_RATIO = SEL_BLOCK // CMP_STRIDE
WINDOW = 512
N_BUCKETS = 32
REL_MAX_DIST = 2048
ALPHA = (2.0 * DEPTH) ** 0.25
LN_EPS = 1e-5
FORCED_SCORE = 1e4
NEG_BIG = -1e30
LOG2E = math.log2(math.e)

LANES = 128
VMEM_LIMIT = 56 * 1024 * 1024
BF16 = jnp.bfloat16
F32 = jnp.float32

_EXACT = N_BUCKETS // 2
_BUCKET_THRESHOLDS = tuple(
    int(math.ceil(_EXACT * (REL_MAX_DIST / _EXACT) ** (k / (N_BUCKETS - _EXACT))))
    for k in range(1, N_BUCKETS - _EXACT))
BIAS_RANGE = 2048
KEY_TILE = 512
TILE_CHUNKS = KEY_TILE // LANES
SEL_BAND_BEHIND = BIAS_RANGE // LANES + TILE_CHUNKS - 2
SEL_BAND_CHUNKS = SEL_BAND_BEHIND + TILE_CHUNKS
WIN_CHUNKS = WINDOW // Q_BLOCK + 1
CMP_TABLES = 32
SUB_ROWS = 128


def _nt_dot(a, b):
    return lax.dot_general(a, b, (((1,), (1,)), ((), ())), preferred_element_type=F32)


def _params(sem):
    return pltpu.CompilerParams(dimension_semantics=sem, vmem_limit_bytes=VMEM_LIMIT)


def _layer_norm(y, g, b):
    mu = jnp.mean(y, axis=-1, keepdims=True)
    yc = y - mu
    var = jnp.mean(yc * yc, axis=-1, keepdims=True)
    return yc * lax.rsqrt(var + LN_EPS) * g + b


def _matmul_kernel(x_ref, w_ref, o_ref, *, tn, scaled_cols, scale, head_major):
    acc = jnp.dot(x_ref[...], w_ref[...], preferred_element_type=F32)
    if scaled_cols:
        j = pl.program_id(1)
        acc = acc * jnp.where(j * tn < scaled_cols, scale, 1.0).astype(F32)
    if head_major:
        for c in range(tn // LANES):
            o_ref[c] = acc[:, c * LANES:(c + 1) * LANES].astype(o_ref.dtype)
    else:
        o_ref[...] = acc.astype(o_ref.dtype)


def _matmul(x, w, *, out_dtype, head_major, tm, tn, scaled_cols=0, scale=1.0):
    m, k = x.shape
    n = w.shape[1]
    assert m % tm == 0 and n % tn == 0 and scaled_cols % tn == 0
    if head_major:
        out_shape = jax.ShapeDtypeStruct((n // LANES, m, LANES), out_dtype)
        out_spec = pl.BlockSpec((tn // LANES, tm, LANES), lambda i, j: (j, i, 0))
    else:
        out_shape = jax.ShapeDtypeStruct((m, n), out_dtype)
        out_spec = pl.BlockSpec((tm, tn), lambda i, j: (i, j))
    return pl.pallas_call(
        functools.partial(_matmul_kernel, tn=tn, scaled_cols=scaled_cols, scale=scale,
                          head_major=head_major),
        grid=(m // tm, n // tn),
        in_specs=[pl.BlockSpec((tm, k), lambda i, j: (i, 0)),
                  pl.BlockSpec((k, tn), lambda i, j: (0, j))],
        out_specs=out_spec,
        out_shape=out_shape,
        compiler_params=_params(("parallel", "arbitrary")),
        name="proj_matmul",
    )(x, w)


def _cum_kernel(gl_ref, b_ref, o_ref, *, nblk):
    r = lax.broadcasted_iota(jnp.int32, (LANES, LANES), 0)
    c = lax.broadcasted_iota(jnp.int32, (LANES, LANES), 1)
    tri = (c <= r).astype(F32)
    bias = b_ref[...]

    def body(i, carry):
        rows = pl.ds(pl.multiple_of(i * LANES, LANES), LANES)
        lf = jax.nn.log_sigmoid(gl_ref[rows, :] + bias)
        cum = jnp.dot(tri, lf, preferred_element_type=F32,
                      precision=lax.Precision.HIGHEST) + carry
        o_ref[rows, :] = cum
        return cum[LANES - 1:LANES, :]

    lax.fori_loop(0, nblk, body, jnp.zeros((1, LANES), F32))


def _forget_cumsum(gate_logits, b_f):
    s = gate_logits.shape[0]
    bias = jnp.zeros((1, LANES), F32).at[0, :N_HEADS].set(b_f)
    return pl.pallas_call(
        functools.partial(_cum_kernel, nblk=s // LANES),
        out_shape=jax.ShapeDtypeStruct((s, LANES), F32),
        compiler_params=pltpu.CompilerParams(vmem_limit_bytes=VMEM_LIMIT),
        name="forget_cumsum",
    )(gate_logits, bias)


def _online_update(s, v, m_ref, l_ref, acc_ref, rows):
    m_prev = m_ref[rows, :]
    m_next = jnp.maximum(m_prev, jnp.max(s, axis=1, keepdims=True))
    alpha = jnp.exp2(m_prev - m_next)
    reps = s.shape[1] // LANES
    p = jnp.exp2(s - jnp.concatenate([m_next] * reps, axis=1))
    part = p[:, :LANES]
    for c in range(1, reps):
        part = part + p[:, c * LANES:(c + 1) * LANES]
    l_ref[rows, :] = alpha * l_ref[rows, :] + part
    acc_ref[rows, :] = alpha * acc_ref[rows, :] + jnp.dot(p.astype(BF16), v,
                                                          preferred_element_type=F32)
    m_ref[rows, :] = m_next


def _reset(m_ref, l_ref, acc_ref):
    m_ref[...] = jnp.full_like(m_ref, NEG_BIG)
    l_ref[...] = jnp.zeros_like(l_ref)
    acc_ref[...] = jnp.zeros_like(acc_ref)


def _normalized(l_ref, acc_ref):
    return acc_ref[...] / jnp.sum(l_ref[...], axis=1, keepdims=True)


def _fox_kernel(q_ref, k_ref, v_ref, c_ref, o_ref, sa_ref, sb_ref, m_ref, l_ref, acc_ref, *, t):
    i = pl.program_id(1)
    c0 = jnp.max(c_ref[i], axis=1, keepdims=True)
    _reset(m_ref, l_ref, acc_ref)

    def keys(j):
        return pl.ds(pl.multiple_of(j * t, t), t)

    def logits(j, buf):
        buf[...] = _nt_dot(q_ref[...], k_ref[keys(j), :]) + (c0 - c_ref[j]) * LOG2E

    def attend(j, buf, masked):
        v = v_ref[keys(j), :]
        for r in range(t // SUB_ROWS):
            rows = slice(r * SUB_ROWS, (r + 1) * SUB_ROWS)
            s = buf[rows, :]
            if masked:
                row = r * SUB_ROWS + lax.broadcasted_iota(jnp.int32, (SUB_ROWS, t), 0)
                col = lax.broadcasted_iota(jnp.int32, (SUB_ROWS, t), 1)
                s = jnp.where(col <= row, s, NEG_BIG)
            _online_update(s, v, m_ref, l_ref, acc_ref, rows)

    def step(j, cur, nxt):
        logits(j + 1, nxt)
        attend(j, cur, False)

    logits(0, sa_ref)

    def pair(p, carry):
        step(2 * p, sa_ref, sb_ref)
        step(2 * p + 1, sb_ref, sa_ref)
        return carry

    lax.fori_loop(0, i // 2, pair, 0)

    @pl.when(i % 2 == 0)
    def _():
        attend(i, sa_ref, True)

    @pl.when(i % 2 == 1)
    def _():
        step(i - 1, sa_ref, sb_ref)
        attend(i, sb_ref, True)

    o_ref[...] = _normalized(l_ref, acc_ref).astype(o_ref.dtype)


def _fox_attention(qkv, cum_rows, *, t):
    _, s, _ = qkv.shape
    h = N_HEADS
    return pl.pallas_call(
        functools.partial(_fox_kernel, t=t),
        grid=(h, s // t),
        in_specs=[pl.BlockSpec((None, t, HEAD_DIM), lambda hh, i: (hh, i, 0)),
                  pl.BlockSpec((None, s, HEAD_DIM), lambda hh, i: (h + hh, 0, 0)),
                  pl.BlockSpec((None, s, HEAD_DIM), lambda hh, i: (2 * h + hh, 0, 0)),
                  pl.BlockSpec((None, s // t, 1, t), lambda hh, i: (hh, 0, 0, 0))],
        out_specs=pl.BlockSpec((t, HEAD_DIM), lambda hh, i: (i, hh)),
        out_shape=jax.ShapeDtypeStruct((s, h * HEAD_DIM), BF16),
        scratch_shapes=[pltpu.VMEM((t, t), F32), pltpu.VMEM((t, t), F32),
                        pltpu.VMEM((t, LANES), F32), pltpu.VMEM((t, LANES), F32),
                        pltpu.VMEM((t, HEAD_DIM), F32)],
        compiler_params=_params(("parallel", "arbitrary")),
        name="fox_attention",
    )(qkv, qkv, qkv, cum_rows)


def _emit_norm(y, g_ref, b_ref, of_ref, ob_ref):
    hn = _layer_norm(y, g_ref[...], b_ref[...])
    of_ref[...] = hn
    ob_ref[...] = hn.astype(BF16)


def _oproj_kernel(a_ref, w_ref, h_ref, g_ref, b_ref, of_ref, ob_ref):
    mix = jnp.dot(a_ref[...], w_ref[...], preferred_element_type=F32)
    _emit_norm(ALPHA * h_ref[...] + mix, g_ref, b_ref, of_ref, ob_ref)


def _nsa_oproj_kernel(oc_ref, os_ref, ow_ref, gl_ref, w_ref, h_ref, g_ref, b_ref,
                      of_ref, ob_ref, x_ref):
    gates = jax.nn.sigmoid(gl_ref[...])
    for hh in range(N_HEADS):
        cols = slice(hh * HEAD_DIM, (hh + 1) * HEAD_DIM)
        x = (gates[:, 3 * hh:3 * hh + 1] * oc_ref[:, cols]
             + gates[:, 3 * hh + 1:3 * hh + 2] * os_ref[:, cols]
             + gates[:, 3 * hh + 2:3 * hh + 3] * ow_ref[:, cols])
        x_ref[:, cols] = x.astype(BF16)
    mix = jnp.dot(x_ref[...], w_ref[...], preferred_element_type=F32)
    _emit_norm(ALPHA * h_ref[...] + mix, g_ref, b_ref, of_ref, ob_ref)


def _row_spec(tm, n):
    return pl.BlockSpec((tm, n), lambda i: (i, 0))


def _const_spec(shape):
    return pl.BlockSpec(shape, lambda i: (0,) * len(shape))


def _norm_outs(s, tm):
    d = D_MODEL
    return dict(
        out_specs=[_row_spec(tm, d), _row_spec(tm, d)],
        out_shape=[jax.ShapeDtypeStruct((s, d), F32), jax.ShapeDtypeStruct((s, d), BF16)])


def _oproj(a, w, h, g, b, *, tm):
    s, d = h.shape
    return pl.pallas_call(
        _oproj_kernel,
        grid=(s // tm,),
        in_specs=[_row_spec(tm, d), _const_spec((d, d)), _row_spec(tm, d),
                  _const_spec((1, d)), _const_spec((1, d))],
        compiler_params=_params(("parallel",)),
        name="oproj_norm",
        **_norm_outs(s, tm),
    )(a, w, h, g, b)


def _nsa_oproj(o_cmp, o_sel, o_win, gate_logits, w, h, g, b, *, tm):
    s, d = h.shape
    return pl.pallas_call(
        _nsa_oproj_kernel,
        grid=(s // tm,),
        in_specs=[_row_spec(tm, d), _row_spec(tm, d), _row_spec(tm, d), _row_spec(tm, LANES),
                  _const_spec((d, d)), _row_spec(tm, d), _const_spec((1, d)), _const_spec((1, d))],
        scratch_shapes=[pltpu.VMEM((tm, d), BF16)],
        compiler_params=_params(("parallel",)),
        name="nsa_oproj_norm",
        **_norm_outs(s, tm),
    )(o_cmp, o_sel, o_win, gate_logits, w, h, g, b)


def _mlp_kernel(x_ref, w1_ref, w2_ref, h_ref, g_ref, b_ref, of_ref, ob_ref, acc_ref):
    f = pl.program_id(1)

    @pl.when(f == 0)
    def _():
        acc_ref[...] = jnp.zeros_like(acc_ref)

    u = jnp.maximum(jnp.dot(x_ref[...], w1_ref[...], preferred_element_type=F32), 0.0)
    acc_ref[...] += jnp.dot((u * u).astype(BF16), w2_ref[...], preferred_element_type=F32)

    @pl.when(f == pl.num_programs(1) - 1)
    def _():
        _emit_norm(ALPHA * h_ref[...] + acc_ref[...], g_ref, b_ref, of_ref, ob_ref)


def _mlp(xb, w1, w2, h, g, b, *, tm, tf):
    s, d = h.shape
    ff = w1.shape[1]
    row = lambda n: pl.BlockSpec((tm, n), lambda i, f: (i, 0))
    vec = pl.BlockSpec((1, d), lambda i, f: (0, 0))
    return pl.pallas_call(
        _mlp_kernel,
        grid=(s // tm, ff // tf),
        in_specs=[row(d), pl.BlockSpec((d, tf), lambda i, f: (0, f)),
                  pl.BlockSpec((tf, d), lambda i, f: (f, 0)), row(d), vec, vec],
        out_specs=[row(d), row(d)],
        out_shape=[jax.ShapeDtypeStruct((s, d), F32), jax.ShapeDtypeStruct((s, d), BF16)],
        scratch_shapes=[pltpu.VMEM((tm, d), F32)],
        compiler_params=_params(("parallel", "arbitrary")),
        name="mlp_norm",
    )(xb, w1, w2, h, g, b)


def _compress_kernel(x_ref, pos_ref, w1_ref, w2_ref, o_ref):
    half = CMP_STRIDE * HEAD_DIM
    x = x_ref[...]
    first = jnp.dot((x + pos_ref[:, :half]).astype(BF16), w1_ref[:half, :],
                    preferred_element_type=F32)
    second = jnp.dot((x + pos_ref[:, half:]).astype(BF16), w1_ref[half:, :],
                     preferred_element_type=F32)
    n = x.shape[0]
    hidden = first + pltpu.roll(second, n - 1, 0)
    act = jax.nn.gelu(hidden)
    o_ref[...] = jnp.dot(act.astype(BF16), w2_ref[...], preferred_element_type=F32).astype(BF16)


def _compress(kv_raw, pos, w1, w2):
    g = N_KV_GROUPS
    s = kv_raw.shape[1]
    nc = s // CMP_STRIDE
    wide = CMP_STRIDE * HEAD_DIM
    x = kv_raw.reshape(2 * g, nc, wide)
    return pl.pallas_call(
        _compress_kernel,
        grid=(2, g),
        in_specs=[pl.BlockSpec((None, nc, wide), lambda a, gg: (a * g + gg, 0, 0)),
                  pl.BlockSpec((None, 1, 2 * wide), lambda a, gg: (a, 0, 0)),
                  pl.BlockSpec((None, 2 * wide, CMP_HIDDEN), lambda a, gg: (a, 0, 0)),
                  pl.BlockSpec((None, CMP_HIDDEN, HEAD_DIM), lambda a, gg: (a, 0, 0))],
        out_specs=pl.BlockSpec((None, None, nc, HEAD_DIM), lambda a, gg: (a, gg, 0, 0)),
        out_shape=jax.ShapeDtypeStruct((2, g, nc, HEAD_DIM), BF16),
        compiler_params=_params(("parallel", "parallel")),
        name="compress_kv",
    )(x, pos, w1, w2)


def _bias_vec_kernel(t_ref, o_ref):
    d = lax.broadcasted_iota(jnp.int32, (1, BIAS_RANGE), 1)
    large = jnp.full_like(d, _EXACT)
    for thr in _BUCKET_THRESHOLDS:
        large = large + (d >= thr).astype(jnp.int32)
    bucket = jnp.where(d < _EXACT, d, large)
    table = t_ref[...]
    acc = jnp.zeros((N_HEADS, BIAS_RANGE), F32)
    for bkt in range(N_BUCKETS):
        acc = jnp.where(bucket == bkt, table[:, bkt:bkt + 1], acc)
    o_ref[...] = (acc - table[:, N_BUCKETS - 1:N_BUCKETS]) * LOG2E


def _bias_vec(rel_bias):
    return pl.pallas_call(
        _bias_vec_kernel,
        out_shape=jax.ShapeDtypeStruct((N_HEADS, BIAS_RANGE), F32),
        name="rel_bias_by_distance",
    )(rel_bias.T)


def _toeplitz(w, rows, cols):
    h, width = w.shape
    flat = jnp.tile(w, (1, rows))[:, :rows * (width - 1)]
    return flat.reshape(h, rows, width - 1)[:, :, :cols]


def _by_distance(vec, d_hi, d_lo, window=None):
    h = vec.shape[0]
    limit = BIAS_RANGE if window is None else window
    above = jnp.full((h, d_hi - limit + 1), 0.0 if window is None else NEG_BIG, F32)
    below = jnp.full((h, -d_lo), NEG_BIG, F32)
    return jnp.concatenate([above, vec[:, limit - 1::-1], below], axis=1)


def _per_group(t):
    n = t.shape[2] // LANES
    t = t.reshape(N_KV_GROUPS, HEADS_PER_GROUP * Q_BLOCK, n, LANES)
    return t.transpose(0, 2, 1, 3)


def _bias_tables(vec):
    pad = jnp.zeros((N_HEADS, Q_BLOCK), F32)
    behind = SEL_BAND_BEHIND * LANES
    ahead = (SEL_BAND_CHUNKS - SEL_BAND_BEHIND) * LANES
    w = jnp.concatenate([_by_distance(vec, behind, 1 - ahead), pad], axis=1)
    t_sel = _per_group(_toeplitz(w, Q_BLOCK, SEL_BAND_CHUNKS * LANES))
    w = jnp.concatenate([_by_distance(vec, WINDOW, 1 - Q_BLOCK, window=WINDOW), pad + NEG_BIG], axis=1)
    t_win = _toeplitz(w, Q_BLOCK, WINDOW + Q_BLOCK)
    t_win = t_win.reshape(N_KV_GROUPS, HEADS_PER_GROUP * Q_BLOCK, WINDOW + Q_BLOCK)
    lo_rel = -(CMP_TABLES - 1) * (Q_BLOCK // CMP_STRIDE)
    n_rel = LANES - lo_rel
    end_min = CMP_STRIDE * lo_rel + CMP_BLOCK - 1
    w = jnp.concatenate([_by_distance(vec, -end_min, 1 - end_min - CMP_STRIDE * n_rel), pad], axis=1)
    band = _toeplitz(w, Q_BLOCK, CMP_STRIDE * n_rel)[:, :, ::CMP_STRIDE]
    per_tile = Q_BLOCK // CMP_STRIDE
    t_cmp = jnp.concatenate(
        [band[:, :, -lo_rel - per_tile * e:-lo_rel - per_tile * e + LANES] for e in range(CMP_TABLES)], axis=2)
    return t_sel, t_win, _per_group(t_cmp)


def _cmp_kernel(q_ref, kc_ref, vc_ref, t0_ref, t1_ref, a_ref, o_ref, sel_ref, *, n_top):
    i = pl.program_id(1)
    r4 = HEADS_PER_GROUP
    rows = r4 * Q_BLOCK
    ncp = kc_ref.shape[0]
    nselp = a_ref.shape[1]
    chunks_per_tile = Q_BLOCK // CMP_STRIDE
    q = q_ref[...].reshape(rows, HEAD_DIM)
    lg = _nt_dot(q, kc_ref[...])
    c_hi = (i * chunks_per_tile) // LANES
    chunk = lax.broadcasted_iota(jnp.int32, (1, ncp), 1) >> (LANES.bit_length() - 1)
    reps = ncp // LANES
    b0 = jnp.concatenate([t0_ref[...]] * reps, axis=1)
    b1 = jnp.concatenate([t1_ref[...]] * reps, axis=1)
    bias = jnp.where(chunk == c_hi, b0,
                     jnp.where(chunk == c_hi - 1, b1,
                               jnp.where(chunk > c_hi, NEG_BIG, 0.0)))
    lg = lg + bias
    valid = lg > 0.5 * NEG_BIG
    m = jnp.max(lg, axis=1, keepdims=True)
    e = jnp.where(valid, jnp.exp2(lg - m), 0.0)
    p = e / jnp.maximum(jnp.sum(e, axis=1, keepdims=True), 1e-30)
    o = jnp.dot(p.astype(BF16), vc_ref[...], preferred_element_type=F32)
    for r in range(r4):
        o_ref[:, r * HEAD_DIM:(r + 1) * HEAD_DIM] = o[r * Q_BLOCK:(r + 1) * Q_BLOCK, :]

    imp = p[0:Q_BLOCK]
    for r in range(1, r4):
        imp = imp + p[r * Q_BLOCK:(r + 1) * Q_BLOCK]
    imp_sel = jnp.dot(imp, a_ref[...], preferred_element_type=F32,
                      precision=lax.Precision.HIGHEST)
    t = i * Q_BLOCK + lax.broadcasted_iota(jnp.int32, (Q_BLOCK, 1), 0)
    blk_t = t >> (SEL_BLOCK.bit_length() - 1)
    j = lax.broadcasted_iota(jnp.int32, (1, nselp), 1)
    forced = (j == 0) | (j == blk_t) | (j == blk_t - 1)
    score = jnp.where(forced, FORCED_SCORE, jnp.where(j <= blk_t, imp_sel, -1.0))
    jf = j.astype(F32)

    def pick(_, carry):
        score, keep = carry
        top = jnp.max(score, axis=1, keepdims=True)
        first = jnp.min(jnp.where(score == top, jf, float(nselp)), axis=1, keepdims=True)
        hit = jf == first
        return jnp.where(hit, -2.0, score), jnp.where(hit, 0.0, keep)

    _, keep = lax.fori_loop(0, n_top, pick, (score, jnp.full((Q_BLOCK, nselp), NEG_BIG, F32)))
    sel_ref[...] = keep.astype(BF16)


def _cmp_select(q, kv_cmp, t_cmp, agg, *, n_top):
    _, s, _ = q.shape
    g, r4 = N_KV_GROUPS, HEADS_PER_GROUP
    ncp = kv_cmp.shape[2]
    nselp = agg.shape[1]
    rows = r4 * Q_BLOCK
    tiles_per_chunk = LANES * CMP_STRIDE // Q_BLOCK
    return pl.pallas_call(
        functools.partial(_cmp_kernel, n_top=n_top),
        grid=(g, s // Q_BLOCK),
        in_specs=[pl.BlockSpec((r4, Q_BLOCK, HEAD_DIM), lambda gg, i: (gg, i, 0)),
                  pl.BlockSpec((None, None, ncp, HEAD_DIM), lambda gg, i: (0, gg, 0, 0)),
                  pl.BlockSpec((None, None, ncp, HEAD_DIM), lambda gg, i: (1, gg, 0, 0)),
                  pl.BlockSpec((None, None, rows, LANES),
                               lambda gg, i: (gg, i % tiles_per_chunk, 0, 0)),
                  pl.BlockSpec((None, None, rows, LANES),
                               lambda gg, i: (gg, i % tiles_per_chunk + tiles_per_chunk, 0, 0)),
                  pl.BlockSpec((ncp, nselp), lambda gg, i: (0, 0))],
        out_specs=[pl.BlockSpec((Q_BLOCK, rows), lambda gg, i: (i, gg)),
                   pl.BlockSpec((None, Q_BLOCK, nselp), lambda gg, i: (gg, i, 0))],
        out_shape=[jax.ShapeDtypeStruct((s, N_HEADS * HEAD_DIM), F32),
                   jax.ShapeDtypeStruct((g, s, nselp), BF16)],
        compiler_params=_params(("parallel", "arbitrary")),
        name="nsa_compressed_select",
    )(q, kv_cmp, kv_cmp, t_cmp, t_cmp, agg)


def _finish(o_ref, l_ref, acc_ref):
    o = _normalized(l_ref, acc_ref)
    for r in range(HEADS_PER_GROUP):
        o_ref[:, r * HEAD_DIM:(r + 1) * HEAD_DIM] = o[r * Q_BLOCK:(r + 1) * Q_BLOCK, :]


def _sel_win_kernel(*refs):
    (q_ref, sb_ref, ka_ref, vs_ref, ts_ref, tw_ref) = refs[:6]
    kw_refs = refs[6:6 + WIN_CHUNKS]
    vw_refs = refs[6 + WIN_CHUNKS:6 + 2 * WIN_CHUNKS]
    os_ref, ow_ref, qa_ref, sa_ref, sb2_ref, m_ref, l_ref, acc_ref = refs[6 + 2 * WIN_CHUNKS:]
    i = pl.program_id(1)
    head_rows = [slice(r * Q_BLOCK, (r + 1) * Q_BLOCK) for r in range(HEADS_PER_GROUP)]
    tiles_per_chunk = LANES * SEL_BLOCK // KEY_TILE
    for c in range(qa_ref.shape[0]):
        sb = sb_ref[:, c * LANES:(c + 1) * LANES]
        for r, rows in enumerate(head_rows):
            qa_ref[c, rows, :HEAD_DIM] = q_ref[r]
            qa_ref[c, rows, HEAD_DIM:] = sb

    kw = jnp.concatenate([kw_refs[b][...] for b in reversed(range(WIN_CHUNKS))], axis=0)
    vw = jnp.concatenate([vw_refs[b][...] for b in reversed(range(WIN_CHUNKS))], axis=0)
    lane_chunk = lax.broadcasted_iota(jnp.int32, (1, kw.shape[0]), 1) >> (LANES.bit_length() - 1)
    before_start = jnp.where(WIN_CHUNKS - 1 - lane_chunk > i, NEG_BIG, 0.0)
    for r, rows in enumerate(head_rows):
        s = _nt_dot(q_ref[r], kw) + (tw_ref[rows, :] + before_start)
        p = jnp.exp2(s - jnp.max(s, axis=1, keepdims=True))
        o = jnp.dot(p.astype(BF16), vw, preferred_element_type=F32)
        ow_ref[:, r * HEAD_DIM:(r + 1) * HEAD_DIM] = o / jnp.sum(p, axis=1, keepdims=True)

    _reset(m_ref, l_ref, acc_ref)
    last = i // TILE_CHUNKS

    def keys(j):
        return pl.ds(pl.multiple_of(j * KEY_TILE, KEY_TILE), KEY_TILE)

    def logits(j, buf):
        buf[...] = _nt_dot(qa_ref[j // tiles_per_chunk], ka_ref[keys(j), :])

    def attend(j, buf):
        v = vs_ref[keys(j), :]
        first = SEL_BAND_BEHIND - (i - TILE_CHUNKS * j)
        for rows in head_rows:
            bias = jnp.concatenate(
                [ts_ref[jnp.maximum(first + c, 0), rows, :] for c in range(TILE_CHUNKS)], axis=1)
            _online_update(buf[rows, :] + bias, v, m_ref, l_ref, acc_ref, rows)

    def step(j, cur, nxt):
        logits(j + 1, nxt)
        attend(j, cur)

    logits(0, sa_ref)

    def pair(p, carry):
        step(2 * p, sa_ref, sb2_ref)
        step(2 * p + 1, sb2_ref, sa_ref)
        return carry

    lax.fori_loop(0, last // 2, pair, 0)

    @pl.when(last % 2 == 0)
    def _():
        attend(last, sa_ref)

    @pl.when(last % 2 == 1)
    def _():
        step(last - 1, sa_ref, sb2_ref)
        attend(last, sb2_ref)

    _finish(os_ref, l_ref, acc_ref)


def _sel_win(q, selbias, k_aug, kv, t_sel, t_win):
    _, s, _ = q.shape
    g, r4 = N_KV_GROUPS, HEADS_PER_GROUP
    rows = r4 * Q_BLOCK
    nselp = selbias.shape[2]
    whole = lambda slot: pl.BlockSpec((None, s, HEAD_DIM), lambda gg, i: (slot * g + gg, 0, 0))

    def win(slot, behind):
        return pl.BlockSpec((None, Q_BLOCK, HEAD_DIM),
                            lambda gg, i: (slot * g + gg, jnp.maximum(i - behind, 0), 0))

    in_specs = [pl.BlockSpec((r4, Q_BLOCK, HEAD_DIM), lambda gg, i: (gg, i, 0)),
                pl.BlockSpec((None, Q_BLOCK, nselp), lambda gg, i: (gg, i, 0)),
                pl.BlockSpec((None, s, 2 * HEAD_DIM), lambda gg, i: (gg, 0, 0)),
                whole(1),
                pl.BlockSpec((None, SEL_BAND_CHUNKS, rows, LANES), lambda gg, i: (gg, 0, 0, 0)),
                pl.BlockSpec((None, rows, WINDOW + Q_BLOCK), lambda gg, i: (gg, 0, 0))]
    in_specs += [win(2, b) for b in range(WIN_CHUNKS)] + [win(3, b) for b in range(WIN_CHUNKS)]
    out = pl.BlockSpec((Q_BLOCK, rows), lambda gg, i: (i, gg))
    return pl.pallas_call(
        _sel_win_kernel,
        grid=(g, s // Q_BLOCK),
        in_specs=in_specs,
        out_specs=[out, out],
        out_shape=[jax.ShapeDtypeStruct((s, N_HEADS * HEAD_DIM), F32)] * 2,
        scratch_shapes=[pltpu.VMEM((nselp // LANES, rows, 2 * HEAD_DIM), BF16),
                        pltpu.VMEM((rows, KEY_TILE), F32), pltpu.VMEM((rows, KEY_TILE), F32),
                        pltpu.VMEM((rows, LANES), F32), pltpu.VMEM((rows, LANES), F32),
                        pltpu.VMEM((rows, HEAD_DIM), F32)],
        compiler_params=_params(("parallel", "arbitrary")),
        name="nsa_selected_window",
    )(q, selbias, k_aug, kv, t_sel, t_win, *([kv] * (2 * WIN_CHUNKS)))


def _selection_aggregator(ncp, nselp):
    n = np.arange(ncp)[:, None]
    j = np.arange(nselp)[None, :]
    hit = (n >= SEL_RATIO * j - 1) & (n <= SEL_RATIO * j + SEL_RATIO - 1) & (n < ncp - 1)
    return jnp.asarray(hit.astype(np.float32))


def _block_onehot(s):
    blk = (np.arange(s) // SEL_BLOCK) % LANES
    return jnp.asarray((blk[:, None] == np.arange(LANES)[None, :]).astype(np.float32), dtype=BF16)


def _round_up(x, m):
    return (x + m - 1) // m * m


def kernel(x, fox_w_in, fox_b_f, fox_w_o, nsa_w_in, nsa_w_o, kv_w, cmp_pos_k, cmp_pos_v, cmp_k_w1, cmp_k_w2, cmp_v_w1, cmp_v_w2, rel_bias, mlp_w1, mlp_w2, ln1_g, ln1_b, ln2_g, ln2_b):
    b, s, d = x.shape
    assert b == 1 and d == D_MODEL and s % (CMP_STRIDE * LANES) == 0
    hd = N_HEADS * HEAD_DIM
    scale = HEAD_DIM ** -0.5 * LOG2E
    tm = min(512, s)
    fox_t = min(512, s)
    pad_cols = lambda w: jnp.pad(w, ((0, 0), (0, LANES - w.shape[1])))

    h = x[0]
    hb = h.astype(BF16)
    kv_state = None
    for layer in range(DEPTH):
        g1, b1 = ln1_g[layer][None, :], ln1_b[layer][None, :]
        g2, b2 = ln2_g[layer][None, :], ln2_b[layer][None, :]
        if layer < N_A_LAYERS:
            w_in = fox_w_in[layer]
            qkv = _matmul(hb, w_in[:, :3 * hd].astype(BF16), out_dtype=BF16, head_major=True,
                          tm=tm, tn=512, scaled_cols=hd, scale=scale)
            gate_logits = _matmul(hb, pad_cols(w_in[:, 3 * hd:]).astype(BF16), out_dtype=F32,
                                  head_major=False, tm=tm, tn=LANES)
            cum = _forget_cumsum(gate_logits, fox_b_f[layer])
            cum_rows = cum[:, :N_HEADS].T.reshape(N_HEADS, s // fox_t, 1, fox_t)
            attn = _fox_attention(qkv, cum_rows, t=fox_t)
            h, hb = _oproj(attn, fox_w_o[layer].astype(BF16), h, g1, b1, tm=tm)
        else:
            k_cmp_v_cmp, kv_tok, k_aug, tables, agg, n_top = kv_state
            t_sel, t_win, t_cmp = tables
            w_in = nsa_w_in[layer - N_A_LAYERS]
            q = _matmul(hb, w_in[:, :hd].astype(BF16), out_dtype=BF16, head_major=True,
                        tm=tm, tn=512, scaled_cols=hd, scale=scale)
            gate_logits = _matmul(hb, pad_cols(w_in[:, hd:]).astype(BF16), out_dtype=F32,
                                  head_major=False, tm=tm, tn=LANES)
            o_cmp, selbias = _cmp_select(q, k_cmp_v_cmp, t_cmp, agg, n_top=n_top)
            o_sel, o_win = _sel_win(q, selbias, k_aug, kv_tok, t_sel, t_win)
            h, hb = _nsa_oproj(o_cmp, o_sel, o_win, gate_logits,
                               nsa_w_o[layer - N_A_LAYERS].astype(BF16), h, g1, b1, tm=min(256, s))
        h, hb = _mlp(hb, mlp_w1[layer].astype(BF16), mlp_w2[layer].astype(BF16), h, g2, b2,
                     tm=tm, tf=512)
        if layer == N_A_LAYERS - 1:
            gd = N_KV_GROUPS * HEAD_DIM
            kvw = kv_w.astype(BF16)
            kv_raw = _matmul(hb, kvw[:, :2 * gd], out_dtype=F32, head_major=True, tm=tm, tn=512)
            kv_tok = _matmul(hb, kvw[:, 2 * gd:], out_dtype=BF16, head_major=True, tm=tm, tn=512)
            pos = jnp.stack([cmp_pos_k.reshape(1, -1), cmp_pos_v.reshape(1, -1)])
            w1 = jnp.stack([cmp_k_w1, cmp_v_w1]).astype(BF16)
            w2 = jnp.stack([cmp_k_w2, cmp_v_w2]).astype(BF16)
            kv_cmp = _compress(kv_raw, pos, w1, w2)
            n_sel = s // SEL_BLOCK
            nselp = _round_up(n_sel, LANES)
            onehot = jnp.broadcast_to(_block_onehot(s), (N_KV_GROUPS, s, LANES))
            k_aug = jnp.concatenate([kv_tok[:N_KV_GROUPS], onehot], axis=-1)
            tables = _bias_tables(_bias_vec(rel_bias))
            agg = _selection_aggregator(s // CMP_STRIDE, nselp)
            kv_state = (kv_cmp, kv_tok, k_aug, tables, agg, min(N_SELECTED, n_sel))
    return h[None]
```

```python
import functools
import math

import numpy as np
import jax
import jax.numpy as jnp
from jax import lax
from jax.experimental import pallas as pl
from jax.experimental.pallas import tpu as pltpu

D_MODEL = 2048
DEPTH = 4
HEAD_DIM = 128
N_HEADS = D_MODEL // HEAD_DIM
N_KV_GROUPS = 4
HEADS_PER_GROUP = N_HEADS // N_KV_GROUPS
D_FF = 4 * D_MODEL
N_A_LAYERS = DEPTH // 2
Q_BLOCK = 128
CMP_BLOCK = 32
CMP_STRIDE = 16
CMP_HIDDEN = 256
SEL_BLOCK = 64
N_SELECTED = 16
SEL_RATIO = SEL_BLOCK // CMP_STRIDE
WINDOW = 512
N_BUCKETS = 32
REL_MAX_DIST = 2048
ALPHA = (2.0 * DEPTH) ** 0.25
LN_EPS = 1e-5
FORCED_SCORE = 1e4
PICKED = -2.0
NEG_BIG = -1e30
LOG2E = math.log2(math.e)

LANES = 128
VMEM_LIMIT = 56 * 1024 * 1024
BF16 = jnp.bfloat16
F32 = jnp.float32

_EXACT = N_BUCKETS // 2
_BUCKET_THRESHOLDS = tuple(
    int(math.ceil(_EXACT * (REL_MAX_DIST / _EXACT) ** (k / (N_BUCKETS - _EXACT))))
    for k in range(1, N_BUCKETS - _EXACT))
BIAS_RANGE = 2048
KEY_TILE = 512
TILE_CHUNKS = KEY_TILE // LANES
SEL_BAND_BEHIND = BIAS_RANGE // LANES + TILE_CHUNKS - 2
SEL_BAND_CHUNKS = SEL_BAND_BEHIND + TILE_CHUNKS
WIN_CHUNKS = WINDOW // Q_BLOCK + 1
CMP_TABLES = 32
SUB_ROWS = 128
FOX_SKIP_MARGIN = 160.0


def _nt_dot(a, b):
    return lax.dot_general(a, b, (((1,), (1,)), ((), ())), preferred_element_type=F32)


def _params(sem):
    return pltpu.CompilerParams(dimension_semantics=sem, vmem_limit_bytes=VMEM_LIMIT)


def _layer_norm(y, g, b):
    mu = jnp.mean(y, axis=-1, keepdims=True)
    yc = y - mu
    var = jnp.mean(yc * yc, axis=-1, keepdims=True)
    return yc * lax.rsqrt(var + LN_EPS) * g + b


def _matmul_kernel(x_ref, w_ref, o_ref, *, tn, scaled_cols, scale, head_major):
    acc = jnp.dot(x_ref[...], w_ref[...], preferred_element_type=F32)
    if scaled_cols:
        j = pl.program_id(1)
        acc = acc * jnp.where(j * tn < scaled_cols, scale, 1.0).astype(F32)
    if head_major:
        for c in range(tn // LANES):
            o_ref[c] = acc[:, c * LANES:(c + 1) * LANES].astype(o_ref.dtype)
    else:
        o_ref[...] = acc.astype(o_ref.dtype)


def _matmul(x, w, *, out_dtype, head_major, tm, tn, scaled_cols=0, scale=1.0):
    m, k = x.shape
    n = w.shape[1]
    assert m % tm == 0 and n % tn == 0 and scaled_cols % tn == 0
    if head_major:
        out_shape = jax.ShapeDtypeStruct((n // LANES, m, LANES), out_dtype)
        out_spec = pl.BlockSpec((tn // LANES, tm, LANES), lambda i, j: (j, i, 0))
    else:
        out_shape = jax.ShapeDtypeStruct((m, n), out_dtype)
        out_spec = pl.BlockSpec((tm, tn), lambda i, j: (i, j))
    return pl.pallas_call(
        functools.partial(_matmul_kernel, tn=tn, scaled_cols=scaled_cols, scale=scale,
                          head_major=head_major),
        grid=(m // tm, n // tn),
        in_specs=[pl.BlockSpec((tm, k), lambda i, j: (i, 0)),
                  pl.BlockSpec((k, tn), lambda i, j: (0, j))],
        out_specs=out_spec,
        out_shape=out_shape,
        compiler_params=_params(("parallel", "arbitrary")),
        name="proj_matmul",
    )(x, w)


def _cum_kernel(gl_ref, b_ref, o_ref, *, nblk):
    r = lax.broadcasted_iota(jnp.int32, (LANES, LANES), 0)
    c = lax.broadcasted_iota(jnp.int32, (LANES, LANES), 1)
    tri = (c <= r).astype(F32)
    bias = b_ref[...]

    def body(i, carry):
        rows = pl.ds(pl.multiple_of(i * LANES, LANES), LANES)
        lf = jax.nn.log_sigmoid(gl_ref[rows, :] + bias)
        cum = jnp.dot(tri, lf, preferred_element_type=F32,
                      precision=lax.Precision.HIGHEST) + carry
        o_ref[rows, :] = cum
        return cum[LANES - 1:LANES, :]

    lax.fori_loop(0, nblk, body, jnp.zeros((1, LANES), F32))


def _forget_cumsum(gate_logits, b_f):
    s = gate_logits.shape[0]
    bias = jnp.zeros((1, LANES), F32).at[0, :N_HEADS].set(b_f)
    return pl.pallas_call(
        functools.partial(_cum_kernel, nblk=s // LANES),
        out_shape=jax.ShapeDtypeStruct((s, LANES), F32),
        compiler_params=pltpu.CompilerParams(vmem_limit_bytes=VMEM_LIMIT),
        name="forget_cumsum",
    )(gate_logits, bias)


def _online_update(s, v, m_ref, l_ref, acc_ref, rows):
    m_prev = m_ref[rows, :]
    m_next = jnp.maximum(m_prev, jnp.max(s, axis=1, keepdims=True))
    alpha = jnp.exp2(m_prev - m_next)
    reps = s.shape[1] // LANES
    p = jnp.exp2(s - jnp.concatenate([m_next] * reps, axis=1))
    part = p[:, :LANES]
    for c in range(1, reps):
        part = part + p[:, c * LANES:(c + 1) * LANES]
    l_ref[rows, :] = alpha * l_ref[rows, :] + part
    acc_ref[rows, :] = alpha * acc_ref[rows, :] + jnp.dot(p.astype(BF16), v,
                                                          preferred_element_type=F32)
    m_ref[rows, :] = m_next


def _reset(m_ref, l_ref, acc_ref):
    m_ref[...] = jnp.full_like(m_ref, NEG_BIG)
    l_ref[...] = jnp.zeros_like(l_ref)
    acc_ref[...] = jnp.zeros_like(acc_ref)


def _normalized(l_ref, acc_ref):
    return acc_ref[...] / jnp.sum(l_ref[...], axis=1, keepdims=True)


def _fox_kernel(first_ref, q_ref, k_ref, v_ref, c_ref, o_ref, sa_ref, sb_ref, m_ref, l_ref, acc_ref, *, t):
    i = pl.program_id(1)
    c0 = jnp.max(c_ref[i], axis=1, keepdims=True)
    _reset(m_ref, l_ref, acc_ref)

    def keys(j):
        return pl.ds(pl.multiple_of(j * t, t), t)

    def logits(j, buf):
        buf[...] = _nt_dot(q_ref[...], k_ref[keys(j), :]) + (c0 - c_ref[j]) * LOG2E

    def attend(j, buf, masked):
        v = v_ref[keys(j), :]
        for r in range(t // SUB_ROWS):
            rows = slice(r * SUB_ROWS, (r + 1) * SUB_ROWS)
            s = buf[rows, :]
            if masked:
                row = r * SUB_ROWS + lax.broadcasted_iota(jnp.int32, (SUB_ROWS, t), 0)
                col = lax.broadcasted_iota(jnp.int32, (SUB_ROWS, t), 1)
                s = jnp.where(col <= row, s, NEG_BIG)
            _online_update(s, v, m_ref, l_ref, acc_ref, rows)

    def step(j, cur, nxt):
        logits(j + 1, nxt)
        attend(j, cur, False)

    first = first_ref[pl.program_id(0), i]
    n = i - first
    logits(first, sa_ref)

    def pair(p, carry):
        step(first + 2 * p, sa_ref, sb_ref)
        step(first + 2 * p + 1, sb_ref, sa_ref)
        return carry

    lax.fori_loop(0, n // 2, pair, 0)

    @pl.when(n % 2 == 0)
    def _():
        attend(i, sa_ref, True)

    @pl.when(n % 2 == 1)
    def _():
        step(i - 1, sa_ref, sb_ref)
        attend(i, sb_ref, True)

    o_ref[...] = _normalized(l_ref, acc_ref).astype(o_ref.dtype)


def _first_fox_tile(qkv, cum, *, t):
    h = N_HEADS
    s = qkv.shape[1]
    norm = lambda a: jnp.sqrt(jnp.max(jnp.sum(jnp.square(a.astype(F32)), axis=-1), axis=-1))
    bound = FOX_SKIP_MARGIN + 2.0 * norm(qkv[:h]) * norm(qkv[h:2 * h])
    ct = cum[:, :h].T
    at_query_start = ct[:, ::t]
    at_key_end = ct[:, t - 1::t]
    gap = (at_key_end[:, None, :] - at_query_start[:, :, None]) * LOG2E
    idx = np.arange(s // t)
    earlier = jnp.asarray(idx[None, :] < idx[:, None])
    skip = earlier[None] & (gap >= bound[:, None, None])
    return jnp.sum(skip, axis=-1).astype(jnp.int32)


def _fox_attention(qkv, cum_rows, first_tile, *, t):
    _, s, _ = qkv.shape
    h = N_HEADS
    grid_spec = pltpu.PrefetchScalarGridSpec(
        num_scalar_prefetch=1,
        grid=(h, s // t),
        in_specs=[pl.BlockSpec((None, t, HEAD_DIM), lambda hh, i, f: (hh, i, 0)),
                  pl.BlockSpec((None, s, HEAD_DIM), lambda hh, i, f: (h + hh, 0, 0)),
                  pl.BlockSpec((None, s, HEAD_DIM), lambda hh, i, f: (2 * h + hh, 0, 0)),
                  pl.BlockSpec((None, s // t, 1, t), lambda hh, i, f: (hh, 0, 0, 0))],
        out_specs=pl.BlockSpec((t, HEAD_DIM), lambda hh, i, f: (i, hh)),
        scratch_shapes=[pltpu.VMEM((t, t), F32), pltpu.VMEM((t, t), F32),
                        pltpu.VMEM((t, LANES), F32), pltpu.VMEM((t, LANES), F32),
                        pltpu.VMEM((t, HEAD_DIM), F32)])
    return pl.pallas_call(
        functools.partial(_fox_kernel, t=t),
        grid_spec=grid_spec,
        out_shape=jax.ShapeDtypeStruct((s, h * HEAD_DIM), BF16),
        compiler_params=_params(("parallel", "arbitrary")),
        name="fox_attention",
    )(first_tile, qkv, qkv, qkv, cum_rows)


def _emit_norm(y, g_ref, b_ref, of_ref, ob_ref):
    hn = _layer_norm(y, g_ref[...], b_ref[...])
    of_ref[...] = hn
    ob_ref[...] = hn.astype(BF16)


def _oproj_kernel(a_ref, w_ref, h_ref, g_ref, b_ref, of_ref, ob_ref):
    mix = jnp.dot(a_ref[...], w_ref[...], preferred_element_type=F32)
    _emit_norm(ALPHA * h_ref[...] + mix, g_ref, b_ref, of_ref, ob_ref)


def _nsa_oproj_kernel(oc_ref, os_ref, ow_ref, gl_ref, w_ref, h_ref, g_ref, b_ref,
                      of_ref, ob_ref, x_ref):
    gates = jax.nn.sigmoid(gl_ref[...])
    for hh in range(N_HEADS):
        cols = slice(hh * HEAD_DIM, (hh + 1) * HEAD_DIM)
        x = (gates[:, 3 * hh:3 * hh + 1] * oc_ref[:, cols]
             + gates[:, 3 * hh + 1:3 * hh + 2] * os_ref[:, cols]
             + gates[:, 3 * hh + 2:3 * hh + 3] * ow_ref[:, cols])
        x_ref[:, cols] = x.astype(BF16)
    mix = jnp.dot(x_ref[...], w_ref[...], preferred_element_type=F32)
    _emit_norm(ALPHA * h_ref[...] + mix, g_ref, b_ref, of_ref, ob_ref)


def _row_spec(tm, n):
    return pl.BlockSpec((tm, n), lambda i: (i, 0))


def _const_spec(shape):
    return pl.BlockSpec(shape, lambda i: (0,) * len(shape))


def _norm_outs(s, tm):
    d = D_MODEL
    return dict(
        out_specs=[_row_spec(tm, d), _row_spec(tm, d)],
        out_shape=[jax.ShapeDtypeStruct((s, d), F32), jax.ShapeDtypeStruct((s, d), BF16)])


def _oproj(a, w, h, g, b, *, tm):
    s, d = h.shape
    return pl.pallas_call(
        _oproj_kernel,
        grid=(s // tm,),
        in_specs=[_row_spec(tm, d), _const_spec((d, d)), _row_spec(tm, d),
                  _const_spec((1, d)), _const_spec((1, d))],
        compiler_params=_params(("parallel",)),
        name="oproj_norm",
        **_norm_outs(s, tm),
    )(a, w, h, g, b)


def _nsa_oproj(o_cmp, o_sel, o_win, gate_logits, w, h, g, b, *, tm):
    s, d = h.shape
    return pl.pallas_call(
        _nsa_oproj_kernel,
        grid=(s // tm,),
        in_specs=[_row_spec(tm, d), _row_spec(tm, d), _row_spec(tm, d), _row_spec(tm, LANES),
                  _const_spec((d, d)), _row_spec(tm, d), _const_spec((1, d)), _const_spec((1, d))],
        scratch_shapes=[pltpu.VMEM((tm, d), BF16)],
        compiler_params=_params(("parallel",)),
        name="nsa_oproj_norm",
        **_norm_outs(s, tm),
    )(o_cmp, o_sel, o_win, gate_logits, w, h, g, b)


def _mlp_kernel(x_ref, w1_ref, w2_ref, h_ref, g_ref, b_ref, of_ref, ob_ref, acc_ref):
    f = pl.program_id(1)

    @pl.when(f == 0)
    def _():
        acc_ref[...] = jnp.zeros_like(acc_ref)

    u = jnp.maximum(jnp.dot(x_ref[...], w1_ref[...], preferred_element_type=F32), 0.0)
    acc_ref[...] += jnp.dot((u * u).astype(BF16), w2_ref[...], preferred_element_type=F32)

    @pl.when(f == pl.num_programs(1) - 1)
    def _():
        _emit_norm(ALPHA * h_ref[...] + acc_ref[...], g_ref, b_ref, of_ref, ob_ref)


def _mlp(xb, w1, w2, h, g, b, *, tm, tf):
    s, d = h.shape
    ff = w1.shape[1]
    row = lambda n: pl.BlockSpec((tm, n), lambda i, f: (i, 0))
    vec = pl.BlockSpec((1, d), lambda i, f: (0, 0))
    return pl.pallas_call(
        _mlp_kernel,
        grid=(s // tm, ff // tf),
        in_specs=[row(d), pl.BlockSpec((d, tf), lambda i, f: (0, f)),
                  pl.BlockSpec((tf, d), lambda i, f: (f, 0)), row(d), vec, vec],
        out_specs=[row(d), row(d)],
        out_shape=[jax.ShapeDtypeStruct((s, d), F32), jax.ShapeDtypeStruct((s, d), BF16)],
        scratch_shapes=[pltpu.VMEM((tm, d), F32)],
        compiler_params=_params(("parallel", "arbitrary")),
        name="mlp_norm",
    )(xb, w1, w2, h, g, b)


def _compress_kernel(x_ref, pos_ref, w1_ref, w2_ref, o_ref):
    half = CMP_STRIDE * HEAD_DIM
    x = x_ref[...]
    first = jnp.dot((x + pos_ref[:, :half]).astype(BF16), w1_ref[:half, :],
                    preferred_element_type=F32)
    second = jnp.dot((x + pos_ref[:, half:]).astype(BF16), w1_ref[half:, :],
                     preferred_element_type=F32)
    n = x.shape[0]
    hidden = first + pltpu.roll(second, n - 1, 0)
    act = jax.nn.gelu(hidden)
    o_ref[...] = jnp.dot(act.astype(BF16), w2_ref[...], preferred_element_type=F32).astype(BF16)


def _compress(kv_raw, pos, w1, w2):
    g = N_KV_GROUPS
    s = kv_raw.shape[1]
    nc = s // CMP_STRIDE
    wide = CMP_STRIDE * HEAD_DIM
    x = kv_raw.reshape(2 * g, nc, wide)
    return pl.pallas_call(
        _compress_kernel,
        grid=(2, g),
        in_specs=[pl.BlockSpec((None, nc, wide), lambda a, gg: (a * g + gg, 0, 0)),
                  pl.BlockSpec((None, 1, 2 * wide), lambda a, gg: (a, 0, 0)),
                  pl.BlockSpec((None, 2 * wide, CMP_HIDDEN), lambda a, gg: (a, 0, 0)),
                  pl.BlockSpec((None, CMP_HIDDEN, HEAD_DIM), lambda a, gg: (a, 0, 0))],
        out_specs=pl.BlockSpec((None, None, nc, HEAD_DIM), lambda a, gg: (a, gg, 0, 0)),
        out_shape=jax.ShapeDtypeStruct((2, g, nc, HEAD_DIM), BF16),
        compiler_params=_params(("parallel", "parallel")),
        name="compress_kv",
    )(x, pos, w1, w2)


def _bias_vec_kernel(t_ref, o_ref):
    d = lax.broadcasted_iota(jnp.int32, (1, BIAS_RANGE), 1)
    large = jnp.full_like(d, _EXACT)
    for thr in _BUCKET_THRESHOLDS:
        large = large + (d >= thr).astype(jnp.int32)
    bucket = jnp.where(d < _EXACT, d, large)
    table = t_ref[...]
    acc = jnp.zeros((N_HEADS, BIAS_RANGE), F32)
    for bkt in range(N_BUCKETS):
        acc = jnp.where(bucket == bkt, table[:, bkt:bkt + 1], acc)
    o_ref[...] = (acc - table[:, N_BUCKETS - 1:N_BUCKETS]) * LOG2E


def _bias_vec(rel_bias):
    return pl.pallas_call(
        _bias_vec_kernel,
        out_shape=jax.ShapeDtypeStruct((N_HEADS, BIAS_RANGE), F32),
        name="rel_bias_by_distance",
    )(rel_bias.T)


def _toeplitz(w, rows, cols):
    h, width = w.shape
    flat = jnp.tile(w, (1, rows))[:, :rows * (width - 1)]
    return flat.reshape(h, rows, width - 1)[:, :, :cols]


def _by_distance(vec, d_hi, d_lo, window=None):
    h = vec.shape[0]
    limit = BIAS_RANGE if window is None else window
    above = jnp.full((h, d_hi - limit + 1), 0.0 if window is None else NEG_BIG, F32)
    below = jnp.full((h, -d_lo), NEG_BIG, F32)
    return jnp.concatenate([above, vec[:, limit - 1::-1], below], axis=1)


def _per_group(t):
    n = t.shape[2] // LANES
    t = t.reshape(N_KV_GROUPS, HEADS_PER_GROUP * Q_BLOCK, n, LANES)
    return t.transpose(0, 2, 1, 3)


def _bias_tables(vec):
    pad = jnp.zeros((N_HEADS, Q_BLOCK), F32)
    behind = SEL_BAND_BEHIND * LANES
    ahead = (SEL_BAND_CHUNKS - SEL_BAND_BEHIND) * LANES
    w = jnp.concatenate([_by_distance(vec, behind, 1 - ahead), pad], axis=1)
    t_sel = _per_group(_toeplitz(w, Q_BLOCK, SEL_BAND_CHUNKS * LANES))
    w = jnp.concatenate([_by_distance(vec, WINDOW, 1 - Q_BLOCK, window=WINDOW), pad + NEG_BIG], axis=1)
    t_win = _toeplitz(w, Q_BLOCK, WINDOW + Q_BLOCK)
    t_win = t_win.reshape(N_KV_GROUPS, HEADS_PER_GROUP * Q_BLOCK, WINDOW + Q_BLOCK)
    lo_rel = -(CMP_TABLES - 1) * (Q_BLOCK // CMP_STRIDE)
    n_rel = LANES - lo_rel
    end_min = CMP_STRIDE * lo_rel + CMP_BLOCK - 1
    w = jnp.concatenate([_by_distance(vec, -end_min, 1 - end_min - CMP_STRIDE * n_rel), pad], axis=1)
    band = _toeplitz(w, Q_BLOCK, CMP_STRIDE * n_rel)[:, :, ::CMP_STRIDE]
    per_tile = Q_BLOCK // CMP_STRIDE
    t_cmp = jnp.concatenate(
        [band[:, :, -lo_rel - per_tile * e:-lo_rel - per_tile * e + LANES] for e in range(CMP_TABLES)], axis=2)
    return t_sel, t_win, _per_group(t_cmp)


def _cmp_kernel(q_ref, kc_ref, vc_ref, t0_ref, t1_ref, a_ref, o_ref, sel_ref, lg_ref, p_ref, imp_ref,
                *, n_top):
    i = pl.program_id(1)
    ncp = kc_ref.shape[0]
    nselp = a_ref.shape[1]
    chunks_per_tile = Q_BLOCK // CMP_STRIDE
    c_hi = (i * chunks_per_tile) // LANES
    lg_ref[...] = _nt_dot(q_ref[...].reshape(HEADS_PER_GROUP * Q_BLOCK, HEAD_DIM), kc_ref[...])
    chunks = [slice(c * LANES, (c + 1) * LANES) for c in range(ncp // LANES)]
    for r in range(HEADS_PER_GROUP):
        rows = slice(r * Q_BLOCK, (r + 1) * Q_BLOCK)
        b0, b1 = t0_ref[rows, :], t1_ref[rows, :]
        top = None
        for c, cols in enumerate(chunks):
            bias = jnp.where(c == c_hi, b0, jnp.where(c == c_hi - 1, b1,
                                                      jnp.where(c > c_hi, NEG_BIG, 0.0)))
            x = lg_ref[rows, cols] + bias
            lg_ref[rows, cols] = x
            top = x if top is None else jnp.maximum(top, x)
        m = jnp.max(top, axis=1, keepdims=True)
        part = None
        for cols in chunks:
            e = jnp.exp2(lg_ref[rows, cols] - m)
            lg_ref[rows, cols] = e
            part = e if part is None else part + e
        total = jnp.maximum(jnp.sum(part, axis=1, keepdims=True), 1e-30)
        inv = jnp.where(m > 0.5 * NEG_BIG, 1.0 / total, 0.0)
        for cols in chunks:
            p = lg_ref[rows, cols] * inv
            p_ref[rows, cols] = p.astype(BF16)
            imp_ref[:, cols] = p if r == 0 else imp_ref[:, cols] + p
    o = jnp.dot(p_ref[...], vc_ref[...], preferred_element_type=F32)
    for r in range(HEADS_PER_GROUP):
        o_ref[:, r * HEAD_DIM:(r + 1) * HEAD_DIM] = o[r * Q_BLOCK:(r + 1) * Q_BLOCK, :]
    imp = imp_ref[...]

    hi = imp.astype(BF16)
    rest = imp - hi.astype(F32)
    mid = rest.astype(BF16)
    low = (rest - mid.astype(F32)).astype(BF16)
    a = a_ref[...]
    imp_sel = (jnp.dot(hi, a, preferred_element_type=F32) + jnp.dot(mid, a, preferred_element_type=F32)
               + jnp.dot(low, a, preferred_element_type=F32))
    imp_sel = imp_sel.T
    t = i * Q_BLOCK + lax.broadcasted_iota(jnp.int32, (1, Q_BLOCK), 1)
    blk_t = t >> (SEL_BLOCK.bit_length() - 1)
    j = lax.broadcasted_iota(jnp.int32, (nselp, 1), 0)
    forced = (j == 0) | (j == blk_t) | (j == blk_t - 1)
    score = jnp.where(forced, FORCED_SCORE, jnp.where(j <= blk_t, imp_sel, -1.0))
    jf = jnp.broadcast_to(j.astype(F32), (nselp, Q_BLOCK))

    def pick(_, score):
        top = jnp.max(score, axis=0, keepdims=True)
        first = jnp.min(jnp.where(score == top, jf, float(nselp)), axis=0, keepdims=True)
        return jnp.where(jf == first, PICKED, score)

    score = lax.fori_loop(0, n_top, pick, score)
    sel_ref[...] = jnp.where(score == PICKED, 0.0, NEG_BIG).T.astype(BF16)


def _cmp_select(q, kv_cmp, t_cmp, agg, *, n_top):
    _, s, _ = q.shape
    g, r4 = N_KV_GROUPS, HEADS_PER_GROUP
    ncp = kv_cmp.shape[2]
    nselp = agg.shape[1]
    rows = r4 * Q_BLOCK
    tiles_per_chunk = LANES * CMP_STRIDE // Q_BLOCK
    return pl.pallas_call(
        functools.partial(_cmp_kernel, n_top=n_top),
        grid=(g, s // Q_BLOCK),
        in_specs=[pl.BlockSpec((r4, Q_BLOCK, HEAD_DIM), lambda gg, i: (gg, i, 0)),
                  pl.BlockSpec((None, None, ncp, HEAD_DIM), lambda gg, i: (0, gg, 0, 0)),
                  pl.BlockSpec((None, None, ncp, HEAD_DIM), lambda gg, i: (1, gg, 0, 0)),
                  pl.BlockSpec((None, None, rows, LANES),
                               lambda gg, i: (gg, i % tiles_per_chunk, 0, 0)),
                  pl.BlockSpec((None, None, rows, LANES),
                               lambda gg, i: (gg, i % tiles_per_chunk + tiles_per_chunk, 0, 0)),
                  pl.BlockSpec((ncp, nselp), lambda gg, i: (0, 0))],
        out_specs=[pl.BlockSpec((Q_BLOCK, rows), lambda gg, i: (i, gg)),
                   pl.BlockSpec((None, Q_BLOCK, nselp), lambda gg, i: (gg, i, 0))],
        out_shape=[jax.ShapeDtypeStruct((s, N_HEADS * HEAD_DIM), F32),
                   jax.ShapeDtypeStruct((g, s, nselp), BF16)],
        scratch_shapes=[pltpu.VMEM((rows, ncp), F32), pltpu.VMEM((rows, ncp), BF16),
                        pltpu.VMEM((Q_BLOCK, ncp), F32)],
        compiler_params=_params(("parallel", "arbitrary")),
        name="nsa_compressed_select",
    )(q, kv_cmp, kv_cmp, t_cmp, t_cmp, agg)


def _finish(o_ref, l_ref, acc_ref):
    o = _normalized(l_ref, acc_ref)
    for r in range(HEADS_PER_GROUP):
        o_ref[:, r * HEAD_DIM:(r + 1) * HEAD_DIM] = o[r * Q_BLOCK:(r + 1) * Q_BLOCK, :]


def _sel_win_kernel(*refs):
    (q_ref, sb_ref, ka_ref, vs_ref, ts_ref, tw_ref) = refs[:6]
    kw_refs = refs[6:6 + WIN_CHUNKS]
    vw_refs = refs[6 + WIN_CHUNKS:6 + 2 * WIN_CHUNKS]
    os_ref, ow_ref, qa_ref, sa_ref, sb2_ref, sw_ref, m_ref, l_ref, acc_ref = refs[6 + 2 * WIN_CHUNKS:]
    i = pl.program_id(1)
    head_rows = [slice(r * Q_BLOCK, (r + 1) * Q_BLOCK) for r in range(HEADS_PER_GROUP)]
    tiles_per_chunk = LANES * SEL_BLOCK // KEY_TILE
    for c in range(qa_ref.shape[0]):
        sb = sb_ref[:, c * LANES:(c + 1) * LANES]
        for r, rows in enumerate(head_rows):
            qa_ref[c, rows, :HEAD_DIM] = q_ref[r]
            qa_ref[c, rows, HEAD_DIM:] = sb

    _reset(m_ref, l_ref, acc_ref)
    last = i // TILE_CHUNKS

    def keys(j):
        return pl.ds(pl.multiple_of(j * KEY_TILE, KEY_TILE), KEY_TILE)

    def logits(j, buf):
        buf[...] = _nt_dot(qa_ref[j // tiles_per_chunk], ka_ref[keys(j), :])

    def attend(j, buf):
        v = vs_ref[keys(j), :]
        first = SEL_BAND_BEHIND - (i - TILE_CHUNKS * j)
        for rows in head_rows:
            bias = jnp.concatenate(
                [ts_ref[jnp.maximum(first + c, 0), rows, :] for c in range(TILE_CHUNKS)], axis=1)
            _online_update(buf[rows, :] + bias, v, m_ref, l_ref, acc_ref, rows)

    def step(j, cur, nxt):
        logits(j + 1, nxt)
        attend(j, cur)

    kw = jnp.concatenate([kw_refs[b][...] for b in reversed(range(WIN_CHUNKS))], axis=0)
    vw = jnp.concatenate([vw_refs[b][...] for b in reversed(range(WIN_CHUNKS))], axis=0)
    lane_chunk = lax.broadcasted_iota(jnp.int32, (1, kw.shape[0]), 1) >> (LANES.bit_length() - 1)
    before_start = jnp.where(WIN_CHUNKS - 1 - lane_chunk > i, NEG_BIG, 0.0)
    sw_ref[...] = _nt_dot(q_ref[...].reshape(HEADS_PER_GROUP * Q_BLOCK, HEAD_DIM), kw)
    logits(0, sa_ref)
    for r, rows in enumerate(head_rows):
        s = sw_ref[rows, :] + (tw_ref[rows, :] + before_start)
        p = jnp.exp2(s - jnp.max(s, axis=1, keepdims=True))
        o = jnp.dot(p.astype(BF16), vw, preferred_element_type=F32)
        ow_ref[:, r * HEAD_DIM:(r + 1) * HEAD_DIM] = o / jnp.sum(p, axis=1, keepdims=True)

    def pair(p, carry):
        step(2 * p, sa_ref, sb2_ref)
        step(2 * p + 1, sb2_ref, sa_ref)
        return carry

    lax.fori_loop(0, last // 2, pair, 0)

    @pl.when(last % 2 == 0)
    def _():
        attend(last, sa_ref)

    @pl.when(last % 2 == 1)
    def _():
        step(last - 1, sa_ref, sb2_ref)
        attend(last, sb2_ref)

    _finish(os_ref, l_ref, acc_ref)


def _sel_win(q, selbias, k_aug, kv, t_sel, t_win):
    _, s, _ = q.shape
    g, r4 = N_KV_GROUPS, HEADS_PER_GROUP
    rows = r4 * Q_BLOCK
    nselp = selbias.shape[2]
    whole = lambda slot: pl.BlockSpec((None, s, HEAD_DIM), lambda gg, i: (slot * g + gg, 0, 0))

    def win(slot, behind):
        return pl.BlockSpec((None, Q_BLOCK, HEAD_DIM),
                            lambda gg, i: (slot * g + gg, jnp.maximum(i - behind, 0), 0))

    in_specs = [pl.BlockSpec((r4, Q_BLOCK, HEAD_DIM), lambda gg, i: (gg, i, 0)),
                pl.BlockSpec((None, Q_BLOCK, nselp), lambda gg, i: (gg, i, 0)),
                pl.BlockSpec((None, s, 2 * HEAD_DIM), lambda gg, i: (gg, 0, 0)),
                whole(1),
                pl.BlockSpec((None, SEL_BAND_CHUNKS, rows, LANES), lambda gg, i: (gg, 0, 0, 0)),
                pl.BlockSpec((None, rows, WINDOW + Q_BLOCK), lambda gg, i: (gg, 0, 0))]
    in_specs += [win(2, b) for b in range(WIN_CHUNKS)] + [win(3, b) for b in range(WIN_CHUNKS)]
    out = pl.BlockSpec((Q_BLOCK, rows), lambda gg, i: (i, gg))
    return pl.pallas_call(
        _sel_win_kernel,
        grid=(g, s // Q_BLOCK),
        in_specs=in_specs,
        out_specs=[out, out],
        out_shape=[jax.ShapeDtypeStruct((s, N_HEADS * HEAD_DIM), F32)] * 2,
        scratch_shapes=[pltpu.VMEM((nselp // LANES, rows, 2 * HEAD_DIM), BF16),
                        pltpu.VMEM((rows, KEY_TILE), F32), pltpu.VMEM((rows, KEY_TILE), F32),
                        pltpu.VMEM((rows, WINDOW + Q_BLOCK), F32),
                        pltpu.VMEM((rows, LANES), F32), pltpu.VMEM((rows, LANES), F32),
                        pltpu.VMEM((rows, HEAD_DIM), F32)],
        compiler_params=_params(("parallel", "arbitrary")),
        name="nsa_selected_window",
    )(q, selbias, k_aug, kv, t_sel, t_win, *([kv] * (2 * WIN_CHUNKS)))


def _selection_aggregator(ncp, nselp):
    n = np.arange(ncp)[:, None]
    j = np.arange(nselp)[None, :]
    hit = (n >= SEL_RATIO * j - 1) & (n <= SEL_RATIO * j + SEL_RATIO - 1) & (n < ncp - 1)
    return jnp.asarray(hit.astype(np.float32), dtype=BF16)


def _block_onehot(s):
    blk = (np.arange(s) // SEL_BLOCK) % LANES
    return jnp.asarray((blk[:, None] == np.arange(LANES)[None, :]).astype(np.float32), dtype=BF16)


def _round_up(x, m):
    return (x + m - 1) // m * m


def kernel(x, fox_w_in, fox_b_f, fox_w_o, nsa_w_in, nsa_w_o, kv_w, cmp_pos_k, cmp_pos_v, cmp_k_w1, cmp_k_w2, cmp_v_w1, cmp_v_w2, rel_bias, mlp_w1, mlp_w2, ln1_g, ln1_b, ln2_g, ln2_b):
    b, s, d = x.shape
    assert b == 1 and d == D_MODEL and s % (CMP_STRIDE * LANES) == 0
    hd = N_HEADS * HEAD_DIM
    scale = HEAD_DIM ** -0.5 * LOG2E
    tm = min(512, s)
    fox_t = min(512, s)
    pad_cols = lambda w: jnp.pad(w, ((0, 0), (0, LANES - w.shape[1])))

    h = x[0]
    hb = h.astype(BF16)
    kv_state = None
    for layer in range(DEPTH):
        g1, b1 = ln1_g[layer][None, :], ln1_b[layer][None, :]
        g2, b2 = ln2_g[layer][None, :], ln2_b[layer][None, :]
        if layer < N_A_LAYERS:
            w_in = fox_w_in[layer]
            qkv = _matmul(hb, w_in[:, :3 * hd].astype(BF16), out_dtype=BF16, head_major=True,
                          tm=tm, tn=512, scaled_cols=hd, scale=scale)
            gate_logits = _matmul(hb, pad_cols(w_in[:, 3 * hd:]).astype(BF16), out_dtype=F32,
                                  head_major=False, tm=tm, tn=LANES)
            cum = _forget_cumsum(gate_logits, fox_b_f[layer])
            cum_rows = cum[:, :N_HEADS].T.reshape(N_HEADS, s // fox_t, 1, fox_t)
            attn = _fox_attention(qkv, cum_rows, _first_fox_tile(qkv, cum, t=fox_t), t=fox_t)
            h, hb = _oproj(attn, fox_w_o[layer].astype(BF16), h, g1, b1, tm=tm)
        else:
            k_cmp_v_cmp, kv_tok, k_aug, tables, agg, n_top = kv_state
            t_sel, t_win, t_cmp = tables
            w_in = nsa_w_in[layer - N_A_LAYERS]
            q = _matmul(hb, w_in[:, :hd].astype(BF16), out_dtype=BF16, head_major=True,
                        tm=tm, tn=512, scaled_cols=hd, scale=scale)
            gate_logits = _matmul(hb, pad_cols(w_in[:, hd:]).astype(BF16), out_dtype=F32,
                                  head_major=False, tm=tm, tn=LANES)
            o_cmp, selbias = _cmp_select(q, k_cmp_v_cmp, t_cmp, agg, n_top=n_top)
            o_sel, o_win = _sel_win(q, selbias, k_aug, kv_tok, t_sel, t_win)
            h, hb = _nsa_oproj(o_cmp, o_sel, o_win, gate_logits,
                               nsa_w_o[layer - N_A_LAYERS].astype(BF16), h, g1, b1, tm=min(256, s))
        h, hb = _mlp(hb, mlp_w1[layer].astype(BF16), mlp_w2[layer].astype(BF16), h, g2, b2,
                     tm=tm, tf=512)
        if layer == N_A_LAYERS - 1:
            gd = N_KV_GROUPS * HEAD_DIM
            kvw = kv_w.astype(BF16)
            kv_raw = _matmul(hb, kvw[:, :2 * gd], out_dtype=F32, head_major=True, tm=tm, tn=512)
            kv_tok = _matmul(hb, kvw[:, 2 * gd:], out_dtype=BF16, head_major=True, tm=tm, tn=512)
            pos = jnp.stack([cmp_pos_k.reshape(1, -1), cmp_pos_v.reshape(1, -1)])
            w1 = jnp.stack([cmp_k_w1, cmp_v_w1]).astype(BF16)
            w2 = jnp.stack([cmp_k_w2, cmp_v_w2]).astype(BF16)
            kv_cmp = _compress(kv_raw, pos, w1, w2)
            n_sel = s // SEL_BLOCK
            nselp = _round_up(n_sel, LANES)
            onehot = jnp.broadcast_to(_block_onehot(s), (N_KV_GROUPS, s, LANES))
            k_aug = jnp.concatenate([kv_tok[:N_KV_GROUPS], onehot], axis=-1)
            tables = _bias_tables(_bias_vec(rel_bias))
            agg = _selection_aggregator(s // CMP_STRIDE, nselp)
            kv_state = (kv_cmp, kv_tok, k_aug, tables, agg, min(N_SELECTED, n_sel))
    return h[None]
```

```python
import functools
import math

import numpy as np
import jax
import jax.numpy as jnp
from jax import lax
from jax.experimental import pallas as pl
from jax.experimental.pallas import tpu as pltpu

D_MODEL = 2048
DEPTH = 4
HEAD_DIM = 128
N_HEADS = D_MODEL // HEAD_DIM
N_KV_GROUPS = 4
HEADS_PER_GROUP = N_HEADS // N_KV_GROUPS
D_FF = 4 * D_MODEL
N_A_LAYERS = DEPTH // 2
Q_BLOCK = 128
CMP_BLOCK = 32
CMP_STRIDE = 16
CMP_HIDDEN = 256
SEL_BLOCK = 64
N_SELECTED = 16
SEL_RATIO = SEL_BLOCK // CMP_STRIDE
WINDOW = 512
N_BUCKETS = 32
REL_MAX_DIST = 2048
ALPHA = (2.0 * DEPTH) ** 0.25
LN_EPS = 1e-5
FORCED_SCORE = 1e4
PICKED = -2.0
NEG_BIG = -1e30
LOG2E = math.log2(math.e)

LANES = 128
MXU_WIDTH = 256
VMEM_LIMIT = 56 * 1024 * 1024
BF16 = jnp.bfloat16
F32 = jnp.float32

_EXACT = N_BUCKETS // 2
_BUCKET_THRESHOLDS = tuple(
    int(math.ceil(_EXACT * (REL_MAX_DIST / _EXACT) ** (k / (N_BUCKETS - _EXACT))))
    for k in range(1, N_BUCKETS - _EXACT))
BIAS_RANGE = 2048
KEY_TILE = 512
TILE_CHUNKS = KEY_TILE // LANES
SEL_BAND_BEHIND = BIAS_RANGE // LANES + TILE_CHUNKS - 2
SEL_BAND_CHUNKS = SEL_BAND_BEHIND + TILE_CHUNKS
WIN_CHUNKS = WINDOW // Q_BLOCK + 1
CMP_TABLES = 32
SUB_ROWS = 128
FOX_SKIP_MARGIN = 160.0


def _nt_dot(a, b):
    return lax.dot_general(a, b, (((1,), (1,)), ((), ())), preferred_element_type=F32)


def _params(sem):
    return pltpu.CompilerParams(dimension_semantics=sem, vmem_limit_bytes=VMEM_LIMIT)


def _layer_norm(y, g, b):
    mu = jnp.mean(y, axis=-1, keepdims=True)
    yc = y - mu
    var = jnp.mean(yc * yc, axis=-1, keepdims=True)
    return yc * lax.rsqrt(var + LN_EPS) * g + b


def _matmul_kernel(x_ref, w_ref, o_ref, *, tn, scaled_cols, scale, head_major):
    acc = jnp.dot(x_ref[...], w_ref[...], preferred_element_type=F32)
    if scaled_cols:
        j = pl.program_id(1)
        acc = acc * jnp.where(j * tn < scaled_cols, scale, 1.0).astype(F32)
    if head_major:
        for c in range(tn // LANES):
            o_ref[c] = acc[:, c * LANES:(c + 1) * LANES].astype(o_ref.dtype)
    else:
        o_ref[...] = acc.astype(o_ref.dtype)


def _weight_spec(w, layer, block, index_map):
    if w.ndim == 2:
        return pl.BlockSpec(block, index_map)
    return pl.BlockSpec((None,) + block, lambda *idx: (layer,) + index_map(*idx))


def _matmul(x, w, *, n=None, first_col=0, layer=None, out_dtype, head_major, tm, tn,
            scaled_cols=0, scale=1.0):
    m, k = x.shape
    n = w.shape[-1] if n is None else n
    assert m % tm == 0 and n % tn == 0 and scaled_cols % tn == 0 and first_col % tn == 0
    col0 = first_col // tn
    w_spec = _weight_spec(w, layer, (k, tn), lambda i, j: (0, col0 + j))
    if head_major:
        out_shape = jax.ShapeDtypeStruct((n // LANES, m, LANES), out_dtype)
        out_spec = pl.BlockSpec((tn // LANES, tm, LANES), lambda i, j: (j, i, 0))
    else:
        out_shape = jax.ShapeDtypeStruct((m, n), out_dtype)
        out_spec = pl.BlockSpec((tm, tn), lambda i, j: (i, j))
    return pl.pallas_call(
        functools.partial(_matmul_kernel, tn=tn, scaled_cols=scaled_cols, scale=scale,
                          head_major=head_major),
        grid=(m // tm, n // tn),
        in_specs=[pl.BlockSpec((tm, k), lambda i, j: (i, 0)), w_spec],
        out_specs=out_spec,
        out_shape=out_shape,
        compiler_params=_params(("parallel", "arbitrary")),
        name="proj_matmul",
    )(x, w)


def _cum_kernel(gl_ref, b_ref, o_ref, *, nblk):
    r = lax.broadcasted_iota(jnp.int32, (LANES, LANES), 0)
    c = lax.broadcasted_iota(jnp.int32, (LANES, LANES), 1)
    tri = (c <= r).astype(F32)
    bias = b_ref[...]

    def body(i, carry):
        rows = pl.ds(pl.multiple_of(i * LANES, LANES), LANES)
        lf = jax.nn.log_sigmoid(gl_ref[rows, :] + bias)
        cum = jnp.dot(tri, lf, preferred_element_type=F32,
                      precision=lax.Precision.HIGHEST) + carry
        o_ref[rows, :] = cum
        return cum[LANES - 1:LANES, :]

    lax.fori_loop(0, nblk, body, jnp.zeros((1, LANES), F32))


def _forget_cumsum(gate_logits, b_f):
    s = gate_logits.shape[0]
    bias = jnp.zeros((1, LANES), F32).at[0, :N_HEADS].set(b_f)
    return pl.pallas_call(
        functools.partial(_cum_kernel, nblk=s // LANES),
        out_shape=jax.ShapeDtypeStruct((s, LANES), F32),
        compiler_params=pltpu.CompilerParams(vmem_limit_bytes=VMEM_LIMIT),
        name="forget_cumsum",
    )(gate_logits, bias)


def _online_update(s, v, m_ref, l_ref, acc_ref, rows):
    m_prev = m_ref[rows, :]
    m_next = jnp.maximum(m_prev, jnp.max(s, axis=1, keepdims=True))
    alpha = jnp.exp2(m_prev - m_next)
    reps = s.shape[1] // LANES
    p = jnp.exp2(s - jnp.concatenate([m_next] * reps, axis=1))
    part = p[:, :LANES]
    for c in range(1, reps):
        part = part + p[:, c * LANES:(c + 1) * LANES]
    l_ref[rows, :] = alpha * l_ref[rows, :] + part
    acc_ref[rows, :] = alpha * acc_ref[rows, :] + jnp.dot(p.astype(BF16), v,
                                                          preferred_element_type=F32)
    m_ref[rows, :] = m_next


def _reset(m_ref, l_ref, acc_ref):
    m_ref[...] = jnp.full_like(m_ref, NEG_BIG)
    l_ref[...] = jnp.zeros_like(l_ref)
    acc_ref[...] = jnp.zeros_like(acc_ref)


def _normalized(l_ref, acc_ref):
    return acc_ref[...] / jnp.sum(l_ref[...], axis=1, keepdims=True)


def _fox_kernel(first_ref, q_ref, k_ref, v_ref, c_ref, o_ref, sa_ref, sb_ref, m_ref, l_ref, acc_ref, *, t):
    i = pl.program_id(1)
    c0 = jnp.max(c_ref[i], axis=1, keepdims=True)
    _reset(m_ref, l_ref, acc_ref)

    def keys(j):
        return pl.ds(pl.multiple_of(j * t, t), t)

    def logits(j, buf):
        buf[...] = _nt_dot(q_ref[...], k_ref[keys(j), :]) + (c0 - c_ref[j]) * LOG2E

    def attend(j, buf, masked):
        v = v_ref[keys(j), :]
        for r in range(t // SUB_ROWS):
            rows = slice(r * SUB_ROWS, (r + 1) * SUB_ROWS)
            s = buf[rows, :]
            if masked:
                row = r * SUB_ROWS + lax.broadcasted_iota(jnp.int32, (SUB_ROWS, t), 0)
                col = lax.broadcasted_iota(jnp.int32, (SUB_ROWS, t), 1)
                s = jnp.where(col <= row, s, NEG_BIG)
            _online_update(s, v, m_ref, l_ref, acc_ref, rows)

    def step(j, cur, nxt):
        logits(j + 1, nxt)
        attend(j, cur, False)

    first = first_ref[pl.program_id(0), i]
    n = i - first
    logits(first, sa_ref)

    def pair(p, carry):
        step(first + 2 * p, sa_ref, sb_ref)
        step(first + 2 * p + 1, sb_ref, sa_ref)
        return carry

    lax.fori_loop(0, n // 2, pair, 0)

    @pl.when(n % 2 == 0)
    def _():
        attend(i, sa_ref, True)

    @pl.when(n % 2 == 1)
    def _():
        step(i - 1, sa_ref, sb_ref)
        attend(i, sb_ref, True)

    o_ref[...] = _normalized(l_ref, acc_ref).astype(o_ref.dtype)


def _first_fox_tile(qkv, cum, *, t):
    h = N_HEADS
    s = qkv.shape[1]
    norm = lambda a: jnp.sqrt(jnp.max(jnp.sum(jnp.square(a.astype(F32)), axis=-1), axis=-1))
    bound = FOX_SKIP_MARGIN + 2.0 * norm(qkv[:h]) * norm(qkv[h:2 * h])
    ct = cum[:, :h].T
    at_query_start = ct[:, ::t]
    at_key_end = ct[:, t - 1::t]
    gap = (at_key_end[:, None, :] - at_query_start[:, :, None]) * LOG2E
    idx = np.arange(s // t)
    earlier = jnp.asarray(idx[None, :] < idx[:, None])
    skip = earlier[None] & (gap >= bound[:, None, None])
    return jnp.sum(skip, axis=-1).astype(jnp.int32)


def _fox_attention(qkv, cum_rows, first_tile, *, t):
    _, s, _ = qkv.shape
    h = N_HEADS
    grid_spec = pltpu.PrefetchScalarGridSpec(
        num_scalar_prefetch=1,
        grid=(h, s // t),
        in_specs=[pl.BlockSpec((None, t, HEAD_DIM), lambda hh, i, f: (hh, i, 0)),
                  pl.BlockSpec((None, s, HEAD_DIM), lambda hh, i, f: (h + hh, 0, 0)),
                  pl.BlockSpec((None, s, HEAD_DIM), lambda hh, i, f: (2 * h + hh, 0, 0)),
                  pl.BlockSpec((None, s // t, 1, t), lambda hh, i, f: (hh, 0, 0, 0))],
        out_specs=pl.BlockSpec((t, HEAD_DIM), lambda hh, i, f: (i, hh)),
        scratch_shapes=[pltpu.VMEM((t, t), F32), pltpu.VMEM((t, t), F32),
                        pltpu.VMEM((t, LANES), F32), pltpu.VMEM((t, LANES), F32),
                        pltpu.VMEM((t, HEAD_DIM), F32)])
    return pl.pallas_call(
        functools.partial(_fox_kernel, t=t),
        grid_spec=grid_spec,
        out_shape=jax.ShapeDtypeStruct((s, h * HEAD_DIM), BF16),
        compiler_params=_params(("parallel", "arbitrary")),
        name="fox_attention",
    )(first_tile, qkv, qkv, qkv, cum_rows)


def _emit_norm(y, g_ref, b_ref, of_ref, ob_ref):
    hn = _layer_norm(y, g_ref[...], b_ref[...])
    of_ref[...] = hn
    ob_ref[...] = hn.astype(BF16)


def _oproj_kernel(a_ref, w_ref, h_ref, g_ref, b_ref, of_ref, ob_ref):
    mix = jnp.dot(a_ref[...], w_ref[...], preferred_element_type=F32)
    _emit_norm(ALPHA * h_ref[...] + mix, g_ref, b_ref, of_ref, ob_ref)


def _nsa_oproj_kernel(oc_ref, os_ref, ow_ref, gl_ref, w_ref, h_ref, g_ref, b_ref,
                      of_ref, ob_ref, x_ref):
    gates = jax.nn.sigmoid(gl_ref[...])
    for hh in range(N_HEADS):
        cols = slice(hh * HEAD_DIM, (hh + 1) * HEAD_DIM)
        x = (gates[:, 3 * hh:3 * hh + 1] * oc_ref[:, cols]
             + gates[:, 3 * hh + 1:3 * hh + 2] * os_ref[:, cols]
             + gates[:, 3 * hh + 2:3 * hh + 3] * ow_ref[:, cols])
        x_ref[:, cols] = x.astype(BF16)
    mix = jnp.dot(x_ref[...], w_ref[...], preferred_element_type=F32)
    _emit_norm(ALPHA * h_ref[...] + mix, g_ref, b_ref, of_ref, ob_ref)


def _row_spec(tm, n):
    return pl.BlockSpec((tm, n), lambda i: (i, 0))


def _const_spec(shape):
    return pl.BlockSpec(shape, lambda i: (0,) * len(shape))


def _norm_outs(s, tm):
    d = D_MODEL
    return dict(
        out_specs=[_row_spec(tm, d), _row_spec(tm, d)],
        out_shape=[jax.ShapeDtypeStruct((s, d), F32), jax.ShapeDtypeStruct((s, d), BF16)])


def _oproj(a, w, layer, h, g, b, *, tm):
    s, d = h.shape
    return pl.pallas_call(
        _oproj_kernel,
        grid=(s // tm,),
        in_specs=[_row_spec(tm, d), _weight_spec(w, layer, (d, d), lambda i: (0, 0)), _row_spec(tm, d),
                  _const_spec((1, d)), _const_spec((1, d))],
        compiler_params=_params(("parallel",)),
        name="oproj_norm",
        **_norm_outs(s, tm),
    )(a, w, h, g, b)


def _nsa_oproj(o_cmp, o_sel, o_win, gate_logits, w, layer, h, g, b, *, tm):
    s, d = h.shape
    return pl.pallas_call(
        _nsa_oproj_kernel,
        grid=(s // tm,),
        in_specs=[_row_spec(tm, d), _row_spec(tm, d), _row_spec(tm, d), _row_spec(tm, LANES),
                  _weight_spec(w, layer, (d, d), lambda i: (0, 0)), _row_spec(tm, d),
                  _const_spec((1, d)), _const_spec((1, d))],
        scratch_shapes=[pltpu.VMEM((tm, d), BF16)],
        compiler_params=_params(("parallel",)),
        name="nsa_oproj_norm",
        **_norm_outs(s, tm),
    )(o_cmp, o_sel, o_win, gate_logits, w, h, g, b)


def _mlp_kernel(x_ref, w1_ref, w2_ref, h_ref, g_ref, b_ref, of_ref, ob_ref, acc_ref):
    f = pl.program_id(1)

    @pl.when(f == 0)
    def _():
        acc_ref[...] = jnp.zeros_like(acc_ref)

    x = x_ref[...]
    chunks = [slice(c * MXU_WIDTH, (c + 1) * MXU_WIDTH) for c in range(w1_ref.shape[1] // MXU_WIDTH)]
    hidden = [jnp.dot(x, w1_ref[:, cols], preferred_element_type=F32) for cols in chunks]
    for cols, u in zip(chunks, hidden):
        u = jnp.maximum(u, 0.0)
        acc_ref[...] += jnp.dot((u * u).astype(BF16), w2_ref[cols, :], preferred_element_type=F32)

    @pl.when(f == pl.num_programs(1) - 1)
    def _():
        _emit_norm(ALPHA * h_ref[...] + acc_ref[...], g_ref, b_ref, of_ref, ob_ref)


def _mlp(xb, w1, w2, layer, h, g, b, *, tm, tf):
    s, d = h.shape
    ff = w1.shape[-1]
    row = lambda n: pl.BlockSpec((tm, n), lambda i, f: (i, 0))
    vec = pl.BlockSpec((1, d), lambda i, f: (0, 0))
    return pl.pallas_call(
        _mlp_kernel,
        grid=(s // tm, ff // tf),
        in_specs=[row(d), _weight_spec(w1, layer, (d, tf), lambda i, f: (0, f)),
                  _weight_spec(w2, layer, (tf, d), lambda i, f: (f, 0)), row(d), vec, vec],
        out_specs=[row(d), row(d)],
        out_shape=[jax.ShapeDtypeStruct((s, d), F32), jax.ShapeDtypeStruct((s, d), BF16)],
        scratch_shapes=[pltpu.VMEM((tm, d), F32)],
        compiler_params=_params(("parallel", "arbitrary")),
        name="mlp_norm",
    )(xb, w1, w2, h, g, b)


def _compress_kernel(x_ref, pos_ref, w1_ref, w2_ref, o_ref):
    half = CMP_STRIDE * HEAD_DIM
    x = x_ref[...]
    first = jnp.dot((x + pos_ref[:, :half]).astype(BF16), w1_ref[:half, :],
                    preferred_element_type=F32)
    second = jnp.dot((x + pos_ref[:, half:]).astype(BF16), w1_ref[half:, :],
                     preferred_element_type=F32)
    n = x.shape[0]
    hidden = first + pltpu.roll(second, n - 1, 0)
    act = jax.nn.gelu(hidden)
    o_ref[...] = jnp.dot(act.astype(BF16), w2_ref[...], preferred_element_type=F32).astype(BF16)


def _compress(kv_raw, pos, w1, w2):
    g = N_KV_GROUPS
    s = kv_raw.shape[1]
    nc = s // CMP_STRIDE
    wide = CMP_STRIDE * HEAD_DIM
    x = kv_raw.reshape(2 * g, nc, wide)
    return pl.pallas_call(
        _compress_kernel,
        grid=(2, g),
        in_specs=[pl.BlockSpec((None, nc, wide), lambda a, gg: (a * g + gg, 0, 0)),
                  pl.BlockSpec((None, 1, 2 * wide), lambda a, gg: (a, 0, 0)),
                  pl.BlockSpec((None, 2 * wide, CMP_HIDDEN), lambda a, gg: (a, 0, 0)),
                  pl.BlockSpec((None, CMP_HIDDEN, HEAD_DIM), lambda a, gg: (a, 0, 0))],
        out_specs=pl.BlockSpec((None, None, nc, HEAD_DIM), lambda a, gg: (a, gg, 0, 0)),
        out_shape=jax.ShapeDtypeStruct((2, g, nc, HEAD_DIM), BF16),
        compiler_params=_params(("parallel", "parallel")),
        name="compress_kv",
    )(x, pos, w1, w2)


def _bias_vec_kernel(t_ref, o_ref):
    d = lax.broadcasted_iota(jnp.int32, (1, BIAS_RANGE), 1)
    large = jnp.full_like(d, _EXACT)
    for thr in _BUCKET_THRESHOLDS:
        large = large + (d >= thr).astype(jnp.int32)
    bucket = jnp.where(d < _EXACT, d, large)
    table = t_ref[...]
    acc = jnp.zeros((N_HEADS, BIAS_RANGE), F32)
    for bkt in range(N_BUCKETS):
        acc = jnp.where(bucket == bkt, table[:, bkt:bkt + 1], acc)
    o_ref[...] = (acc - table[:, N_BUCKETS - 1:N_BUCKETS]) * LOG2E


def _bias_vec(rel_bias):
    return pl.pallas_call(
        _bias_vec_kernel,
        out_shape=jax.ShapeDtypeStruct((N_HEADS, BIAS_RANGE), F32),
        name="rel_bias_by_distance",
    )(rel_bias.T)


def _toeplitz(w, rows, cols):
    h, width = w.shape
    flat = jnp.tile(w, (1, rows))[:, :rows * (width - 1)]
    return flat.reshape(h, rows, width - 1)[:, :, :cols]


def _by_distance(vec, d_hi, d_lo, window=None):
    h = vec.shape[0]
    limit = BIAS_RANGE if window is None else window
    above = jnp.full((h, d_hi - limit + 1), 0.0 if window is None else NEG_BIG, F32)
    below = jnp.full((h, -d_lo), NEG_BIG, F32)
    return jnp.concatenate([above, vec[:, limit - 1::-1], below], axis=1)


def _per_group(t):
    n = t.shape[2] // LANES
    t = t.reshape(N_KV_GROUPS, HEADS_PER_GROUP * Q_BLOCK, n, LANES)
    return t.transpose(0, 2, 1, 3)


def _bias_tables(vec):
    pad = jnp.zeros((N_HEADS, Q_BLOCK), F32)
    behind = SEL_BAND_BEHIND * LANES
    ahead = (SEL_BAND_CHUNKS - SEL_BAND_BEHIND) * LANES
    w = jnp.concatenate([_by_distance(vec, behind, 1 - ahead), pad], axis=1)
    t_sel = _per_group(_toeplitz(w, Q_BLOCK, SEL_BAND_CHUNKS * LANES))
    w = jnp.concatenate([_by_distance(vec, WINDOW, 1 - Q_BLOCK, window=WINDOW), pad + NEG_BIG], axis=1)
    t_win = _toeplitz(w, Q_BLOCK, WINDOW + Q_BLOCK)
    t_win = t_win.reshape(N_KV_GROUPS, HEADS_PER_GROUP * Q_BLOCK, WINDOW + Q_BLOCK)
    lo_rel = -(CMP_TABLES - 1) * (Q_BLOCK // CMP_STRIDE)
    n_rel = LANES - lo_rel
    end_min = CMP_STRIDE * lo_rel + CMP_BLOCK - 1
    w = jnp.concatenate([_by_distance(vec, -end_min, 1 - end_min - CMP_STRIDE * n_rel), pad], axis=1)
    band = _toeplitz(w, Q_BLOCK, CMP_STRIDE * n_rel)[:, :, ::CMP_STRIDE]
    per_tile = Q_BLOCK // CMP_STRIDE
    t_cmp = jnp.concatenate(
        [band[:, :, -lo_rel - per_tile * e:-lo_rel - per_tile * e + LANES] for e in range(CMP_TABLES)], axis=2)
    return t_sel, t_win, _per_group(t_cmp)


def _cmp_kernel(q_ref, kc_ref, vc_ref, t0_ref, t1_ref, a_ref, o_ref, sel_ref, lg_ref, p_ref, imp_ref,
                *, n_top):
    i = pl.program_id(1)
    ncp = kc_ref.shape[0]
    nselp = a_ref.shape[1]
    chunks_per_tile = Q_BLOCK // CMP_STRIDE
    c_hi = (i * chunks_per_tile) // LANES
    lg_ref[...] = _nt_dot(q_ref[...].reshape(HEADS_PER_GROUP * Q_BLOCK, HEAD_DIM), kc_ref[...])
    chunks = [slice(c * LANES, (c + 1) * LANES) for c in range(ncp // LANES)]
    heads = [slice(r * Q_BLOCK, (r + 1) * Q_BLOCK) for r in range(HEADS_PER_GROUP)]
    row_max = []
    for rows in heads:
        b0, b1 = t0_ref[rows, :], t1_ref[rows, :]
        top = None
        for c, cols in enumerate(chunks):
            bias = jnp.where(c == c_hi, b0, jnp.where(c == c_hi - 1, b1,
                                                      jnp.where(c > c_hi, NEG_BIG, 0.0)))
            x = lg_ref[rows, cols] + bias
            lg_ref[rows, cols] = x
            top = x if top is None else jnp.maximum(top, x)
        row_max.append(jnp.max(top, axis=1, keepdims=True))
    scale = []
    for rows, m in zip(heads, row_max):
        part = None
        for cols in chunks:
            e = jnp.exp2(lg_ref[rows, cols] - m)
            lg_ref[rows, cols] = e
            part = e if part is None else part + e
        total = jnp.maximum(jnp.sum(part, axis=1, keepdims=True), 1e-30)
        scale.append(jnp.where(m > 0.5 * NEG_BIG, 1.0 / total, 0.0))
    for cols in chunks:
        imp = None
        for rows, inv in zip(heads, scale):
            p = lg_ref[rows, cols] * inv
            p_ref[rows, cols] = p.astype(BF16)
            imp = p if imp is None else imp + p
        imp_ref[:, cols] = imp
    o = jnp.dot(p_ref[...], vc_ref[...], preferred_element_type=F32)
    for r in range(HEADS_PER_GROUP):
        o_ref[:, r * HEAD_DIM:(r + 1) * HEAD_DIM] = o[r * Q_BLOCK:(r + 1) * Q_BLOCK, :]
    imp = imp_ref[...]

    hi = imp.astype(BF16)
    rest = imp - hi.astype(F32)
    mid = rest.astype(BF16)
    low = (rest - mid.astype(F32)).astype(BF16)
    a = a_ref[...]
    imp_sel = (jnp.dot(hi, a, preferred_element_type=F32) + jnp.dot(mid, a, preferred_element_type=F32)
               + jnp.dot(low, a, preferred_element_type=F32))
    imp_sel = imp_sel.T
    t = i * Q_BLOCK + lax.broadcasted_iota(jnp.int32, (1, Q_BLOCK), 1)
    blk_t = t >> (SEL_BLOCK.bit_length() - 1)
    j = lax.broadcasted_iota(jnp.int32, (nselp, 1), 0)
    forced = (j == 0) | (j == blk_t) | (j == blk_t - 1)
    score = jnp.where(forced, FORCED_SCORE, jnp.where(j <= blk_t, imp_sel, -1.0))
    jf = jnp.broadcast_to(j.astype(F32), (nselp, Q_BLOCK))

    def pick(_, score):
        top = jnp.max(score, axis=0, keepdims=True)
        first = jnp.min(jnp.where(score == top, jf, float(nselp)), axis=0, keepdims=True)
        return jnp.where(jf == first, PICKED, score)

    score = lax.fori_loop(0, n_top, pick, score)
    sel_ref[...] = jnp.where(score == PICKED, 0.0, NEG_BIG).T.astype(BF16)


def _cmp_select(q, kv_cmp, t_cmp, agg, *, n_top):
    _, s, _ = q.shape
    g, r4 = N_KV_GROUPS, HEADS_PER_GROUP
    ncp = kv_cmp.shape[2]
    nselp = agg.shape[1]
    rows = r4 * Q_BLOCK
    tiles_per_chunk = LANES * CMP_STRIDE // Q_BLOCK
    return pl.pallas_call(
        functools.partial(_cmp_kernel, n_top=n_top),
        grid=(g, s // Q_BLOCK),
        in_specs=[pl.BlockSpec((r4, Q_BLOCK, HEAD_DIM), lambda gg, i: (gg, i, 0)),
                  pl.BlockSpec((None, None, ncp, HEAD_DIM), lambda gg, i: (0, gg, 0, 0)),
                  pl.BlockSpec((None, None, ncp, HEAD_DIM), lambda gg, i: (1, gg, 0, 0)),
                  pl.BlockSpec((None, None, rows, LANES),
                               lambda gg, i: (gg, i % tiles_per_chunk, 0, 0)),
                  pl.BlockSpec((None, None, rows, LANES),
                               lambda gg, i: (gg, i % tiles_per_chunk + tiles_per_chunk, 0, 0)),
                  pl.BlockSpec((ncp, nselp), lambda gg, i: (0, 0))],
        out_specs=[pl.BlockSpec((Q_BLOCK, rows), lambda gg, i: (i, gg)),
                   pl.BlockSpec((None, Q_BLOCK, nselp), lambda gg, i: (gg, i, 0))],
        out_shape=[jax.ShapeDtypeStruct((s, N_HEADS * HEAD_DIM), F32),
                   jax.ShapeDtypeStruct((g, s, nselp), BF16)],
        scratch_shapes=[pltpu.VMEM((rows, ncp), F32), pltpu.VMEM((rows, ncp), BF16),
                        pltpu.VMEM((Q_BLOCK, ncp), F32)],
        compiler_params=_params(("parallel", "arbitrary")),
        name="nsa_compressed_select",
    )(q, kv_cmp, kv_cmp, t_cmp, t_cmp, agg)


def _finish(o_ref, l_ref, acc_ref):
    o = _normalized(l_ref, acc_ref)
    for r in range(HEADS_PER_GROUP):
        o_ref[:, r * HEAD_DIM:(r + 1) * HEAD_DIM] = o[r * Q_BLOCK:(r + 1) * Q_BLOCK, :]


def _sel_win_kernel(*refs):
    (q_ref, sb_ref, ka_ref, vs_ref, ts_ref, tw_ref) = refs[:6]
    kw_refs = refs[6:6 + WIN_CHUNKS]
    vw_refs = refs[6 + WIN_CHUNKS:6 + 2 * WIN_CHUNKS]
    os_ref, ow_ref, qa_ref, sa_ref, sb2_ref, sw_ref, m_ref, l_ref, acc_ref = refs[6 + 2 * WIN_CHUNKS:]
    i = pl.program_id(1)
    head_rows = [slice(r * Q_BLOCK, (r + 1) * Q_BLOCK) for r in range(HEADS_PER_GROUP)]
    tiles_per_chunk = LANES * SEL_BLOCK // KEY_TILE
    for c in range(qa_ref.shape[0]):
        sb = sb_ref[:, c * LANES:(c + 1) * LANES]
        for r, rows in enumerate(head_rows):
            qa_ref[c, rows, :HEAD_DIM] = q_ref[r]
            qa_ref[c, rows, HEAD_DIM:] = sb

    _reset(m_ref, l_ref, acc_ref)
    last = i // TILE_CHUNKS

    def keys(j):
        return pl.ds(pl.multiple_of(j * KEY_TILE, KEY_TILE), KEY_TILE)

    def logits(j, buf):
        buf[...] = _nt_dot(qa_ref[j // tiles_per_chunk], ka_ref[keys(j), :])

    def attend(j, buf, biased=True):
        v = vs_ref[keys(j), :]
        first = SEL_BAND_BEHIND - (i - TILE_CHUNKS * j)
        for rows in head_rows:
            s = buf[rows, :]
            if biased:
                s = s + jnp.concatenate(
                    [ts_ref[jnp.maximum(first + c, 0), rows, :] for c in range(TILE_CHUNKS)], axis=1)
            _online_update(s, v, m_ref, l_ref, acc_ref, rows)

    def step(j, cur, nxt, biased=True):
        logits(j + 1, nxt)
        attend(j, cur, biased)

    kw = jnp.concatenate([kw_refs[b][...] for b in reversed(range(WIN_CHUNKS))], axis=0)
    vw = jnp.concatenate([vw_refs[b][...] for b in reversed(range(WIN_CHUNKS))], axis=0)
    lane_chunk = lax.broadcasted_iota(jnp.int32, (1, kw.shape[0]), 1) >> (LANES.bit_length() - 1)
    before_start = jnp.where(WIN_CHUNKS - 1 - lane_chunk > i, NEG_BIG, 0.0)
    sw_ref[...] = _nt_dot(q_ref[...].reshape(HEADS_PER_GROUP * Q_BLOCK, HEAD_DIM), kw)
    logits(0, sa_ref)
    for r, rows in enumerate(head_rows):
        s = sw_ref[rows, :] + (tw_ref[rows, :] + before_start)
        p = jnp.exp2(s - jnp.max(s, axis=1, keepdims=True))
        o = jnp.dot(p.astype(BF16), vw, preferred_element_type=F32)
        ow_ref[:, r * HEAD_DIM:(r + 1) * HEAD_DIM] = o / jnp.sum(p, axis=1, keepdims=True)

    far_tiles = jnp.maximum(i - SEL_BAND_BEHIND - 1 + TILE_CHUNKS, 0) // TILE_CHUNKS
    far_pairs = far_tiles // 2

    def far_pair(p, carry):
        step(2 * p, sa_ref, sb2_ref, biased=False)
        step(2 * p + 1, sb2_ref, sa_ref, biased=False)
        return carry

    lax.fori_loop(0, far_pairs, far_pair, 0)
    start = 2 * far_pairs
    left = last - start + 1

    def pair(p, carry):
        step(start + 2 * p, sa_ref, sb2_ref)
        step(start + 2 * p + 1, sb2_ref, sa_ref)
        return carry

    lax.fori_loop(0, (left - 1) // 2, pair, 0)

    @pl.when(left % 2 == 1)
    def _():
        attend(last, sa_ref)

    @pl.when(left % 2 == 0)
    def _():
        step(last - 1, sa_ref, sb2_ref)
        attend(last, sb2_ref)

    _finish(os_ref, l_ref, acc_ref)


def _sel_win(q, selbias, k_aug, kv, t_sel, t_win):
    _, s, _ = q.shape
    g, r4 = N_KV_GROUPS, HEADS_PER_GROUP
    rows = r4 * Q_BLOCK
    nselp = selbias.shape[2]
    whole = lambda slot: pl.BlockSpec((None, s, HEAD_DIM), lambda gg, i: (slot * g + gg, 0, 0))

    def win(slot, behind):
        return pl.BlockSpec((None, Q_BLOCK, HEAD_DIM),
                            lambda gg, i: (slot * g + gg, jnp.maximum(i - behind, 0), 0))

    in_specs = [pl.BlockSpec((r4, Q_BLOCK, HEAD_DIM), lambda gg, i: (gg, i, 0)),
                pl.BlockSpec((None, Q_BLOCK, nselp), lambda gg, i: (gg, i, 0)),
                pl.BlockSpec((None, s, 2 * HEAD_DIM), lambda gg, i: (gg, 0, 0)),
                whole(1),
                pl.BlockSpec((None, SEL_BAND_CHUNKS, rows, LANES), lambda gg, i: (gg, 0, 0, 0)),
                pl.BlockSpec((None, rows, WINDOW + Q_BLOCK), lambda gg, i: (gg, 0, 0))]
    in_specs += [win(2, b) for b in range(WIN_CHUNKS)] + [win(3, b) for b in range(WIN_CHUNKS)]
    out = pl.BlockSpec((Q_BLOCK, rows), lambda gg, i: (i, gg))
    return pl.pallas_call(
        _sel_win_kernel,
        grid=(g, s // Q_BLOCK),
        in_specs=in_specs,
        out_specs=[out, out],
        out_shape=[jax.ShapeDtypeStruct((s, N_HEADS * HEAD_DIM), F32)] * 2,
        scratch_shapes=[pltpu.VMEM((nselp // LANES, rows, 2 * HEAD_DIM), BF16),
                        pltpu.VMEM((rows, KEY_TILE), F32), pltpu.VMEM((rows, KEY_TILE), F32),
                        pltpu.VMEM((rows, WINDOW + Q_BLOCK), F32),
                        pltpu.VMEM((rows, LANES), F32), pltpu.VMEM((rows, LANES), F32),
                        pltpu.VMEM((rows, HEAD_DIM), F32)],
        compiler_params=_params(("parallel", "arbitrary")),
        name="nsa_selected_window",
    )(q, selbias, k_aug, kv, t_sel, t_win, *([kv] * (2 * WIN_CHUNKS)))


def _selection_aggregator(ncp, nselp):
    n = np.arange(ncp)[:, None]
    j = np.arange(nselp)[None, :]
    hit = (n >= SEL_RATIO * j - 1) & (n <= SEL_RATIO * j + SEL_RATIO - 1) & (n < ncp - 1)
    return jnp.asarray(hit.astype(np.float32), dtype=BF16)


def _block_onehot(s):
    blk = (np.arange(s) // SEL_BLOCK) % LANES
    return jnp.asarray((blk[:, None] == np.arange(LANES)[None, :]).astype(np.float32), dtype=BF16)


def _round_up(x, m):
    return (x + m - 1) // m * m


def kernel(x, fox_w_in, fox_b_f, fox_w_o, nsa_w_in, nsa_w_o, kv_w, cmp_pos_k, cmp_pos_v, cmp_k_w1, cmp_k_w2, cmp_v_w1, cmp_v_w2, rel_bias, mlp_w1, mlp_w2, ln1_g, ln1_b, ln2_g, ln2_b):
    b, s, d = x.shape
    assert b == 1 and d == D_MODEL and s % (CMP_STRIDE * LANES) == 0
    hd = N_HEADS * HEAD_DIM
    scale = HEAD_DIM ** -0.5 * LOG2E
    tm = min(512, s)
    tp = min(1024, s)
    fox_t = min(512, s)
    pad_cols = lambda w: jnp.pad(w, ((0, 0), (0, LANES - w.shape[1])))

    fox_in, fox_out = fox_w_in.astype(BF16), fox_w_o.astype(BF16)
    nsa_in, nsa_out = nsa_w_in.astype(BF16), nsa_w_o.astype(BF16)
    w1_all, w2_all = mlp_w1.astype(BF16), mlp_w2.astype(BF16)
    project = functools.partial(_matmul, tm=tp, tn=512, out_dtype=BF16, head_major=True)
    gate_project = functools.partial(_matmul, tm=tp, tn=LANES, out_dtype=F32, head_major=False)

    h = x[0]
    hb = h.astype(BF16)
    kv_state = None
    for layer in range(DEPTH):
        g1, b1 = ln1_g[layer][None, :], ln1_b[layer][None, :]
        g2, b2 = ln2_g[layer][None, :], ln2_b[layer][None, :]
        if layer < N_A_LAYERS:
            qkv = project(hb, fox_in, layer=layer, n=3 * hd, scaled_cols=hd, scale=scale)
            gate_logits = gate_project(hb, pad_cols(fox_in[layer][:, 3 * hd:]))
            cum = _forget_cumsum(gate_logits, fox_b_f[layer])
            cum_rows = cum[:, :N_HEADS].T.reshape(N_HEADS, s // fox_t, 1, fox_t)
            attn = _fox_attention(qkv, cum_rows, _first_fox_tile(qkv, cum, t=fox_t), t=fox_t)
            h, hb = _oproj(attn, fox_out, layer, h, g1, b1, tm=tm)
        else:
            nsa_layer = layer - N_A_LAYERS
            k_cmp_v_cmp, kv_tok, k_aug, tables, agg, n_top = kv_state
            t_sel, t_win, t_cmp = tables
            q = project(hb, nsa_in, layer=nsa_layer, n=hd, scaled_cols=hd, scale=scale)
            gate_logits = gate_project(hb, pad_cols(nsa_in[nsa_layer][:, hd:]))
            o_cmp, selbias = _cmp_select(q, k_cmp_v_cmp, t_cmp, agg, n_top=n_top)
            o_sel, o_win = _sel_win(q, selbias, k_aug, kv_tok, t_sel, t_win)
            h, hb = _nsa_oproj(o_cmp, o_sel, o_win, gate_logits, nsa_out, nsa_layer, h, g1, b1,
                               tm=min(256, s))
        h, hb = _mlp(hb, w1_all, w2_all, layer, h, g2, b2, tm=tm, tf=1024)
        if layer == N_A_LAYERS - 1:
            gd = N_KV_GROUPS * HEAD_DIM
            kvw = kv_w.astype(BF16)
            kv_raw = project(hb, kvw, n=2 * gd, out_dtype=F32)
            kv_tok = project(hb, kvw, n=4 * gd, first_col=2 * gd)
            pos = jnp.stack([cmp_pos_k.reshape(1, -1), cmp_pos_v.reshape(1, -1)])
            w1 = jnp.stack([cmp_k_w1, cmp_v_w1]).astype(BF16)
            w2 = jnp.stack([cmp_k_w2, cmp_v_w2]).astype(BF16)
            kv_cmp = _compress(kv_raw, pos, w1, w2)
            n_sel = s // SEL_BLOCK
            nselp = _round_up(n_sel, LANES)
            onehot = jnp.broadcast_to(_block_onehot(s), (N_KV_GROUPS, s, LANES))
            k_aug = jnp.concatenate([kv_tok[:N_KV_GROUPS], onehot], axis=-1)
            tables = _bias_tables(_bias_vec(rel_bias))
            agg = _selection_aggregator(s // CMP_STRIDE, nselp)
            kv_state = (kv_cmp, kv_tok, k_aug, tables, agg, min(N_SELECTED, n_sel))
    return h[None]
```

```python
import functools
import math

import numpy as np
import jax
import jax.numpy as jnp
from jax import lax
from jax.experimental import pallas as pl
from jax.experimental.pallas import tpu as pltpu

D_MODEL = 2048
DEPTH = 4
HEAD_DIM = 128
N_HEADS = D_MODEL // HEAD_DIM
N_KV_GROUPS = 4
HEADS_PER_GROUP = N_HEADS // N_KV_GROUPS
D_FF = 4 * D_MODEL
N_A_LAYERS = DEPTH // 2
Q_BLOCK = 128
CMP_BLOCK = 32
CMP_STRIDE = 16
CMP_HIDDEN = 256
SEL_BLOCK = 64
N_SELECTED = 16
SEL_RATIO = SEL_BLOCK // CMP_STRIDE
WINDOW = 512
N_BUCKETS = 32
REL_MAX_DIST = 2048
ALPHA = (2.0 * DEPTH) ** 0.25
LN_EPS = 1e-5
N_FORCED = 3
PICKED = -2.0
NEG_BIG = -1e30
LOG2E = math.log2(math.e)

LANES = 128
MXU_WIDTH = 256
VMEM_LIMIT = 56 * 1024 * 1024
BF16 = jnp.bfloat16
F32 = jnp.float32

_EXACT = N_BUCKETS // 2
_BUCKET_THRESHOLDS = tuple(
    int(math.ceil(_EXACT * (REL_MAX_DIST / _EXACT) ** (k / (N_BUCKETS - _EXACT))))
    for k in range(1, N_BUCKETS - _EXACT))
BIAS_RANGE = 2048
KEY_TILE = 512
TILE_CHUNKS = KEY_TILE // LANES
SEL_BAND_BEHIND = BIAS_RANGE // LANES + TILE_CHUNKS - 2
SEL_BAND_CHUNKS = SEL_BAND_BEHIND + TILE_CHUNKS
WIN_CHUNKS = WINDOW // Q_BLOCK + 1
UNROLL = 4
CMP_TABLES = 32
SUB_ROWS = 128
FOX_SKIP_MARGIN = 160.0


def _nt_dot(a, b):
    return lax.dot_general(a, b, (((1,), (1,)), ((), ())), preferred_element_type=F32)


def _params(sem):
    return pltpu.CompilerParams(dimension_semantics=sem, vmem_limit_bytes=VMEM_LIMIT)


def _layer_norm(y, g, b):
    mu = jnp.mean(y, axis=-1, keepdims=True)
    yc = y - mu
    var = jnp.mean(yc * yc, axis=-1, keepdims=True)
    return yc * lax.rsqrt(var + LN_EPS) * g + b


def _matmul_kernel(x_ref, w_ref, o_ref, *, tn, scaled_cols, scale, head_major):
    acc = jnp.dot(x_ref[...], w_ref[...], preferred_element_type=F32)
    if scaled_cols:
        j = pl.program_id(1)
        acc = acc * jnp.where(j * tn < scaled_cols, scale, 1.0).astype(F32)
    if head_major:
        for c in range(tn // LANES):
            o_ref[c] = acc[:, c * LANES:(c + 1) * LANES].astype(o_ref.dtype)
    else:
        o_ref[...] = acc.astype(o_ref.dtype)


def _weight_spec(w, layer, block, index_map):
    if w.ndim == 2:
        return pl.BlockSpec(block, index_map)
    return pl.BlockSpec((None,) + block, lambda *idx: (layer,) + index_map(*idx))


def _matmul(x, w, *, n=None, first_col=0, layer=None, out_dtype, head_major, tm, tn,
            scaled_cols=0, scale=1.0):
    m, k = x.shape
    n = w.shape[-1] if n is None else n
    assert m % tm == 0 and n % tn == 0 and scaled_cols % tn == 0 and first_col % tn == 0
    col0 = first_col // tn
    w_spec = _weight_spec(w, layer, (k, tn), lambda i, j: (0, col0 + j))
    if head_major:
        out_shape = jax.ShapeDtypeStruct((n // LANES, m, LANES), out_dtype)
        out_spec = pl.BlockSpec((tn // LANES, tm, LANES), lambda i, j: (j, i, 0))
    else:
        out_shape = jax.ShapeDtypeStruct((m, n), out_dtype)
        out_spec = pl.BlockSpec((tm, tn), lambda i, j: (i, j))
    return pl.pallas_call(
        functools.partial(_matmul_kernel, tn=tn, scaled_cols=scaled_cols, scale=scale,
                          head_major=head_major),
        grid=(m // tm, n // tn),
        in_specs=[pl.BlockSpec((tm, k), lambda i, j: (i, 0)), w_spec],
        out_specs=out_spec,
        out_shape=out_shape,
        compiler_params=_params(("parallel", "arbitrary")),
        name="proj_matmul",
    )(x, w)


def _cum_kernel(gl_ref, b_ref, o_ref, *, nblk):
    r = lax.broadcasted_iota(jnp.int32, (LANES, LANES), 0)
    c = lax.broadcasted_iota(jnp.int32, (LANES, LANES), 1)
    tri = (c <= r).astype(F32)
    bias = b_ref[...]

    def body(i, carry):
        rows = pl.ds(pl.multiple_of(i * LANES, LANES), LANES)
        lf = jax.nn.log_sigmoid(gl_ref[rows, :] + bias)
        cum = jnp.dot(tri, lf, preferred_element_type=F32,
                      precision=lax.Precision.HIGHEST) + carry
        o_ref[rows, :] = cum
        return cum[LANES - 1:LANES, :]

    lax.fori_loop(0, nblk, body, jnp.zeros((1, LANES), F32))


def _forget_cumsum(gate_logits, b_f):
    s = gate_logits.shape[0]
    bias = jnp.zeros((1, LANES), F32).at[0, :N_HEADS].set(b_f)
    return pl.pallas_call(
        functools.partial(_cum_kernel, nblk=s // LANES),
        out_shape=jax.ShapeDtypeStruct((s, LANES), F32),
        compiler_params=pltpu.CompilerParams(vmem_limit_bytes=VMEM_LIMIT),
        name="forget_cumsum",
    )(gate_logits, bias)


def _online_update(s, v, m_ref, l_ref, acc_ref, rows):
    m_prev = m_ref[rows, :]
    m_next = jnp.maximum(m_prev, jnp.max(s, axis=1, keepdims=True))
    alpha = jnp.exp2(m_prev - m_next)
    reps = s.shape[1] // LANES
    p = jnp.exp2(s - jnp.concatenate([m_next] * reps, axis=1))
    part = p[:, :LANES]
    for c in range(1, reps):
        part = part + p[:, c * LANES:(c + 1) * LANES]
    l_ref[rows, :] = alpha * l_ref[rows, :] + part
    acc_ref[rows, :] = alpha * acc_ref[rows, :] + jnp.dot(p.astype(BF16), v,
                                                          preferred_element_type=F32)
    m_ref[rows, :] = m_next


def _reset(m_ref, l_ref, acc_ref):
    m_ref[...] = jnp.full_like(m_ref, NEG_BIG)
    l_ref[...] = jnp.zeros_like(l_ref)
    acc_ref[...] = jnp.zeros_like(acc_ref)


def _normalized(l_ref, acc_ref):
    return acc_ref[...] / jnp.sum(l_ref[...], axis=1, keepdims=True)


def _fox_kernel(first_ref, q_ref, k_ref, v_ref, c_ref, o_ref, sa_ref, sb_ref, m_ref, l_ref, acc_ref, *, t):
    i = pl.program_id(1)
    c0 = jnp.max(c_ref[i], axis=1, keepdims=True)
    _reset(m_ref, l_ref, acc_ref)

    def keys(j):
        return pl.ds(pl.multiple_of(j * t, t), t)

    def logits(j, buf):
        buf[...] = _nt_dot(q_ref[...], k_ref[keys(j), :]) + (c0 - c_ref[j]) * LOG2E

    def attend(j, buf, masked):
        v = v_ref[keys(j), :]
        for r in range(t // SUB_ROWS):
            rows = slice(r * SUB_ROWS, (r + 1) * SUB_ROWS)
            s = buf[rows, :]
            if masked:
                row = r * SUB_ROWS + lax.broadcasted_iota(jnp.int32, (SUB_ROWS, t), 0)
                col = lax.broadcasted_iota(jnp.int32, (SUB_ROWS, t), 1)
                s = jnp.where(col <= row, s, NEG_BIG)
            _online_update(s, v, m_ref, l_ref, acc_ref, rows)

    def step(j, cur, nxt):
        logits(j + 1, nxt)
        attend(j, cur, False)

    first = first_ref[pl.program_id(0), i]
    n = i - first
    logits(first, sa_ref)

    def pair(p, carry):
        step(first + 2 * p, sa_ref, sb_ref)
        step(first + 2 * p + 1, sb_ref, sa_ref)
        return carry

    lax.fori_loop(0, n // 2, pair, 0)

    @pl.when(n % 2 == 0)
    def _():
        attend(i, sa_ref, True)

    @pl.when(n % 2 == 1)
    def _():
        step(i - 1, sa_ref, sb_ref)
        attend(i, sb_ref, True)

    o_ref[...] = _normalized(l_ref, acc_ref).astype(o_ref.dtype)


def _first_fox_tile(qkv, cum, *, t):
    h = N_HEADS
    s = qkv.shape[1]
    norm = lambda a: jnp.sqrt(jnp.max(jnp.sum(jnp.square(a.astype(F32)), axis=-1), axis=-1))
    bound = FOX_SKIP_MARGIN + 2.0 * norm(qkv[:h]) * norm(qkv[h:2 * h])
    ct = cum[:, :h].T
    at_query_start = ct[:, ::t]
    at_key_end = ct[:, t - 1::t]
    gap = (at_key_end[:, None, :] - at_query_start[:, :, None]) * LOG2E
    idx = np.arange(s // t)
    earlier = jnp.asarray(idx[None, :] < idx[:, None])
    skip = earlier[None] & (gap >= bound[:, None, None])
    return jnp.sum(skip, axis=-1).astype(jnp.int32)


def _fox_attention(qkv, cum_rows, first_tile, *, t):
    _, s, _ = qkv.shape
    h = N_HEADS
    grid_spec = pltpu.PrefetchScalarGridSpec(
        num_scalar_prefetch=1,
        grid=(h, s // t),
        in_specs=[pl.BlockSpec((None, t, HEAD_DIM), lambda hh, i, f: (hh, i, 0)),
                  pl.BlockSpec((None, s, HEAD_DIM), lambda hh, i, f: (h + hh, 0, 0)),
                  pl.BlockSpec((None, s, HEAD_DIM), lambda hh, i, f: (2 * h + hh, 0, 0)),
                  pl.BlockSpec((None, s // t, 1, t), lambda hh, i, f: (hh, 0, 0, 0))],
        out_specs=pl.BlockSpec((t, HEAD_DIM), lambda hh, i, f: (i, hh)),
        scratch_shapes=[pltpu.VMEM((t, t), F32), pltpu.VMEM((t, t), F32),
                        pltpu.VMEM((t, LANES), F32), pltpu.VMEM((t, LANES), F32),
                        pltpu.VMEM((t, HEAD_DIM), F32)])
    return pl.pallas_call(
        functools.partial(_fox_kernel, t=t),
        grid_spec=grid_spec,
        out_shape=jax.ShapeDtypeStruct((s, h * HEAD_DIM), BF16),
        compiler_params=_params(("parallel", "arbitrary")),
        name="fox_attention",
    )(first_tile, qkv, qkv, qkv, cum_rows)


def _emit_norm(y, g_ref, b_ref, of_ref, ob_ref):
    hn = _layer_norm(y, g_ref[...], b_ref[...])
    of_ref[...] = hn
    ob_ref[...] = hn.astype(BF16)


def _oproj_kernel(a_ref, w_ref, h_ref, g_ref, b_ref, of_ref, ob_ref):
    mix = jnp.dot(a_ref[...], w_ref[...], preferred_element_type=F32)
    _emit_norm(ALPHA * h_ref[...] + mix, g_ref, b_ref, of_ref, ob_ref)


def _nsa_oproj_kernel(oc_ref, os_ref, ow_ref, gl_ref, w_ref, h_ref, g_ref, b_ref,
                      of_ref, ob_ref, x_ref):
    gates = jax.nn.sigmoid(gl_ref[...])
    for hh in range(N_HEADS):
        cols = slice(hh * HEAD_DIM, (hh + 1) * HEAD_DIM)
        x = (gates[:, 3 * hh:3 * hh + 1] * oc_ref[:, cols]
             + gates[:, 3 * hh + 1:3 * hh + 2] * os_ref[:, cols]
             + gates[:, 3 * hh + 2:3 * hh + 3] * ow_ref[:, cols])
        x_ref[:, cols] = x.astype(BF16)
    mix = jnp.dot(x_ref[...], w_ref[...], preferred_element_type=F32)
    _emit_norm(ALPHA * h_ref[...] + mix, g_ref, b_ref, of_ref, ob_ref)


def _row_spec(tm, n):
    return pl.BlockSpec((tm, n), lambda i: (i, 0))


def _const_spec(shape):
    return pl.BlockSpec(shape, lambda i: (0,) * len(shape))


def _norm_outs(s, tm):
    d = D_MODEL
    return dict(
        out_specs=[_row_spec(tm, d), _row_spec(tm, d)],
        out_shape=[jax.ShapeDtypeStruct((s, d), F32), jax.ShapeDtypeStruct((s, d), BF16)])


def _oproj(a, w, layer, h, g, b, *, tm):
    s, d = h.shape
    return pl.pallas_call(
        _oproj_kernel,
        grid=(s // tm,),
        in_specs=[_row_spec(tm, d), _weight_spec(w, layer, (d, d), lambda i: (0, 0)), _row_spec(tm, d),
                  _const_spec((1, d)), _const_spec((1, d))],
        compiler_params=_params(("parallel",)),
        name="oproj_norm",
        **_norm_outs(s, tm),
    )(a, w, h, g, b)


def _nsa_oproj(o_cmp, o_sel, o_win, gate_logits, w, layer, h, g, b, *, tm):
    s, d = h.shape
    return pl.pallas_call(
        _nsa_oproj_kernel,
        grid=(s // tm,),
        in_specs=[_row_spec(tm, d), _row_spec(tm, d), _row_spec(tm, d), _row_spec(tm, LANES),
                  _weight_spec(w, layer, (d, d), lambda i: (0, 0)), _row_spec(tm, d),
                  _const_spec((1, d)), _const_spec((1, d))],
        scratch_shapes=[pltpu.VMEM((tm, d), BF16)],
        compiler_params=_params(("parallel",)),
        name="nsa_oproj_norm",
        **_norm_outs(s, tm),
    )(o_cmp, o_sel, o_win, gate_logits, w, h, g, b)


def _mlp_kernel(x_ref, w1_ref, w2_ref, h_ref, g_ref, b_ref, of_ref, ob_ref, acc_ref):
    f = pl.program_id(1)

    @pl.when(f == 0)
    def _():
        acc_ref[...] = jnp.zeros_like(acc_ref)

    x = x_ref[...]
    chunks = [slice(c * MXU_WIDTH, (c + 1) * MXU_WIDTH) for c in range(w1_ref.shape[1] // MXU_WIDTH)]
    hidden = [jnp.dot(x, w1_ref[:, cols], preferred_element_type=F32) for cols in chunks]
    for cols, u in zip(chunks, hidden):
        u = jnp.maximum(u, 0.0)
        acc_ref[...] += jnp.dot((u * u).astype(BF16), w2_ref[cols, :], preferred_element_type=F32)

    @pl.when(f == pl.num_programs(1) - 1)
    def _():
        _emit_norm(ALPHA * h_ref[...] + acc_ref[...], g_ref, b_ref, of_ref, ob_ref)


def _mlp(xb, w1, w2, layer, h, g, b, *, tm, tf):
    s, d = h.shape
    ff = w1.shape[-1]
    row = lambda n: pl.BlockSpec((tm, n), lambda i, f: (i, 0))
    vec = pl.BlockSpec((1, d), lambda i, f: (0, 0))
    return pl.pallas_call(
        _mlp_kernel,
        grid=(s // tm, ff // tf),
        in_specs=[row(d), _weight_spec(w1, layer, (d, tf), lambda i, f: (0, f)),
                  _weight_spec(w2, layer, (tf, d), lambda i, f: (f, 0)), row(d), vec, vec],
        out_specs=[row(d), row(d)],
        out_shape=[jax.ShapeDtypeStruct((s, d), F32), jax.ShapeDtypeStruct((s, d), BF16)],
        scratch_shapes=[pltpu.VMEM((tm, d), F32)],
        compiler_params=_params(("parallel", "arbitrary")),
        name="mlp_norm",
    )(xb, w1, w2, h, g, b)


def _compress_kernel(x_ref, pos_ref, w1_ref, w2_ref, o_ref):
    half = CMP_STRIDE * HEAD_DIM
    x = x_ref[...]
    first = jnp.dot((x + pos_ref[:, :half]).astype(BF16), w1_ref[:half, :],
                    preferred_element_type=F32)
    second = jnp.dot((x + pos_ref[:, half:]).astype(BF16), w1_ref[half:, :],
                     preferred_element_type=F32)
    n = x.shape[0]
    hidden = first + pltpu.roll(second, n - 1, 0)
    act = jax.nn.gelu(hidden)
    o_ref[...] = jnp.dot(act.astype(BF16), w2_ref[...], preferred_element_type=F32).astype(BF16)


def _compress(kv_raw, pos, w1, w2):
    g = N_KV_GROUPS
    s = kv_raw.shape[1]
    nc = s // CMP_STRIDE
    wide = CMP_STRIDE * HEAD_DIM
    x = kv_raw.reshape(2 * g, nc, wide)
    return pl.pallas_call(
        _compress_kernel,
        grid=(2, g),
        in_specs=[pl.BlockSpec((None, nc, wide), lambda a, gg: (a * g + gg, 0, 0)),
                  pl.BlockSpec((None, 1, 2 * wide), lambda a, gg: (a, 0, 0)),
                  pl.BlockSpec((None, 2 * wide, CMP_HIDDEN), lambda a, gg: (a, 0, 0)),
                  pl.BlockSpec((None, CMP_HIDDEN, HEAD_DIM), lambda a, gg: (a, 0, 0))],
        out_specs=pl.BlockSpec((None, None, nc, HEAD_DIM), lambda a, gg: (a, gg, 0, 0)),
        out_shape=jax.ShapeDtypeStruct((2, g, nc, HEAD_DIM), BF16),
        compiler_params=_params(("parallel", "parallel")),
        name="compress_kv",
    )(x, pos, w1, w2)


def _bias_vec_kernel(t_ref, o_ref):
    d = lax.broadcasted_iota(jnp.int32, (1, BIAS_RANGE), 1)
    large = jnp.full_like(d, _EXACT)
    for thr in _BUCKET_THRESHOLDS:
        large = large + (d >= thr).astype(jnp.int32)
    bucket = jnp.where(d < _EXACT, d, large)
    table = t_ref[...]
    acc = jnp.zeros((N_HEADS, BIAS_RANGE), F32)
    for bkt in range(N_BUCKETS):
        acc = jnp.where(bucket == bkt, table[:, bkt:bkt + 1], acc)
    o_ref[...] = (acc - table[:, N_BUCKETS - 1:N_BUCKETS]) * LOG2E


def _bias_vec(rel_bias):
    return pl.pallas_call(
        _bias_vec_kernel,
        out_shape=jax.ShapeDtypeStruct((N_HEADS, BIAS_RANGE), F32),
        name="rel_bias_by_distance",
    )(rel_bias.T)


def _toeplitz(w, rows, cols):
    h, width = w.shape
    flat = jnp.tile(w, (1, rows))[:, :rows * (width - 1)]
    return flat.reshape(h, rows, width - 1)[:, :, :cols]


def _by_distance(vec, d_hi, d_lo, window=None):
    h = vec.shape[0]
    limit = BIAS_RANGE if window is None else window
    above = jnp.full((h, d_hi - limit + 1), 0.0 if window is None else NEG_BIG, F32)
    below = jnp.full((h, -d_lo), NEG_BIG, F32)
    return jnp.concatenate([above, vec[:, limit - 1::-1], below], axis=1)


def _per_group(t):
    n = t.shape[2] // LANES
    t = t.reshape(N_KV_GROUPS, HEADS_PER_GROUP * Q_BLOCK, n, LANES)
    return t.transpose(0, 2, 1, 3)


def _bias_tables(vec):
    pad = jnp.zeros((N_HEADS, Q_BLOCK), F32)
    behind = SEL_BAND_BEHIND * LANES
    ahead = (SEL_BAND_CHUNKS - SEL_BAND_BEHIND) * LANES
    w = jnp.concatenate([_by_distance(vec, behind, 1 - ahead), pad], axis=1)
    t_sel = _per_group(_toeplitz(w, Q_BLOCK, SEL_BAND_CHUNKS * LANES))
    w = jnp.concatenate([_by_distance(vec, WINDOW, 1 - Q_BLOCK, window=WINDOW), pad + NEG_BIG], axis=1)
    t_win = _toeplitz(w, Q_BLOCK, WINDOW + Q_BLOCK)
    t_win = t_win.reshape(N_KV_GROUPS, HEADS_PER_GROUP * Q_BLOCK, WINDOW + Q_BLOCK)
    lo_rel = -(CMP_TABLES - 1) * (Q_BLOCK // CMP_STRIDE)
    n_rel = LANES - lo_rel
    end_min = CMP_STRIDE * lo_rel + CMP_BLOCK - 1
    w = jnp.concatenate([_by_distance(vec, -end_min, 1 - end_min - CMP_STRIDE * n_rel), pad], axis=1)
    band = _toeplitz(w, Q_BLOCK, CMP_STRIDE * n_rel)[:, :, ::CMP_STRIDE]
    per_tile = Q_BLOCK // CMP_STRIDE
    t_cmp = jnp.concatenate(
        [band[:, :, -lo_rel - per_tile * e:-lo_rel - per_tile * e + LANES] for e in range(CMP_TABLES)], axis=2)
    return t_sel, t_win, _per_group(t_cmp)


def _cmp_kernel(q_ref, kc_ref, vc_ref, t0_ref, t1_ref, a_ref, o_ref, sel_ref, lg_ref, p_ref, imp_ref,
                *, n_top):
    i = pl.program_id(1)
    ncp = kc_ref.shape[0]
    nselp = a_ref.shape[1]
    chunks_per_tile = Q_BLOCK // CMP_STRIDE
    c_hi = (i * chunks_per_tile) // LANES
    lg_ref[...] = _nt_dot(q_ref[...].reshape(HEADS_PER_GROUP * Q_BLOCK, HEAD_DIM), kc_ref[...])
    chunks = [slice(c * LANES, (c + 1) * LANES) for c in range(ncp // LANES)]
    heads = [slice(r * Q_BLOCK, (r + 1) * Q_BLOCK) for r in range(HEADS_PER_GROUP)]
    row_max = []
    for rows in heads:
        b0, b1 = t0_ref[rows, :], t1_ref[rows, :]
        top = None
        for c, cols in enumerate(chunks):
            bias = jnp.where(c == c_hi, b0, jnp.where(c == c_hi - 1, b1,
                                                      jnp.where(c > c_hi, NEG_BIG, 0.0)))
            x = lg_ref[rows, cols] + bias
            lg_ref[rows, cols] = x
            top = x if top is None else jnp.maximum(top, x)
        row_max.append(jnp.max(top, axis=1, keepdims=True))
    scale = []
    for rows, m in zip(heads, row_max):
        part = None
        for cols in chunks:
            e = jnp.exp2(lg_ref[rows, cols] - m)
            lg_ref[rows, cols] = e
            part = e if part is None else part + e
        total = jnp.maximum(jnp.sum(part, axis=1, keepdims=True), 1e-30)
        scale.append(jnp.where(m > 0.5 * NEG_BIG, 1.0 / total, 0.0))
    for cols in chunks:
        imp = None
        for rows, inv in zip(heads, scale):
            p = lg_ref[rows, cols] * inv
            p_ref[rows, cols] = p.astype(BF16)
            imp = p if imp is None else imp + p
        imp_ref[:, cols] = imp
    o = jnp.dot(p_ref[...], vc_ref[...], preferred_element_type=F32)
    for r in range(HEADS_PER_GROUP):
        o_ref[:, r * HEAD_DIM:(r + 1) * HEAD_DIM] = o[r * Q_BLOCK:(r + 1) * Q_BLOCK, :]
    imp = imp_ref[...]

    hi = imp.astype(BF16)
    rest = imp - hi.astype(F32)
    mid = rest.astype(BF16)
    low = (rest - mid.astype(F32)).astype(BF16)
    a = a_ref[...]
    imp_sel = (jnp.dot(hi, a, preferred_element_type=F32) + jnp.dot(mid, a, preferred_element_type=F32)
               + jnp.dot(low, a, preferred_element_type=F32))
    imp_sel = imp_sel.T
    t = i * Q_BLOCK + lax.broadcasted_iota(jnp.int32, (1, Q_BLOCK), 1)
    blk_t = t >> (SEL_BLOCK.bit_length() - 1)
    j = lax.broadcasted_iota(jnp.int32, (nselp, 1), 0)
    forced = (j == 0) | (j == blk_t) | (j == blk_t - 1)
    score = jnp.where(forced, PICKED, jnp.where(j <= blk_t, imp_sel, -1.0))
    jf = jnp.broadcast_to(j.astype(F32), (nselp, Q_BLOCK))

    def pick(_, score):
        top = jnp.max(score, axis=0, keepdims=True)
        first = jnp.min(jnp.where(score == top, jf, float(nselp)), axis=0, keepdims=True)
        return jnp.where(jf == first, PICKED, score)

    score = lax.fori_loop(0, n_top - N_FORCED, pick, score)
    sel_ref[...] = jnp.where(score == PICKED, 0.0, NEG_BIG).T.astype(BF16)


def _cmp_select(q, kv_cmp, t_cmp, agg, *, n_top):
    _, s, _ = q.shape
    g, r4 = N_KV_GROUPS, HEADS_PER_GROUP
    ncp = kv_cmp.shape[2]
    nselp = agg.shape[1]
    rows = r4 * Q_BLOCK
    tiles_per_chunk = LANES * CMP_STRIDE // Q_BLOCK
    return pl.pallas_call(
        functools.partial(_cmp_kernel, n_top=n_top),
        grid=(g, s // Q_BLOCK),
        in_specs=[pl.BlockSpec((r4, Q_BLOCK, HEAD_DIM), lambda gg, i: (gg, i, 0)),
                  pl.BlockSpec((None, None, ncp, HEAD_DIM), lambda gg, i: (0, gg, 0, 0)),
                  pl.BlockSpec((None, None, ncp, HEAD_DIM), lambda gg, i: (1, gg, 0, 0)),
                  pl.BlockSpec((None, None, rows, LANES),
                               lambda gg, i: (gg, i % tiles_per_chunk, 0, 0)),
                  pl.BlockSpec((None, None, rows, LANES),
                               lambda gg, i: (gg, i % tiles_per_chunk + tiles_per_chunk, 0, 0)),
                  pl.BlockSpec((ncp, nselp), lambda gg, i: (0, 0))],
        out_specs=[pl.BlockSpec((Q_BLOCK, rows), lambda gg, i: (i, gg)),
                   pl.BlockSpec((None, Q_BLOCK, nselp), lambda gg, i: (gg, i, 0))],
        out_shape=[jax.ShapeDtypeStruct((s, N_HEADS * HEAD_DIM), F32),
                   jax.ShapeDtypeStruct((g, s, nselp), BF16)],
        scratch_shapes=[pltpu.VMEM((rows, ncp), F32), pltpu.VMEM((rows, ncp), BF16),
                        pltpu.VMEM((Q_BLOCK, ncp), F32)],
        compiler_params=_params(("parallel", "arbitrary")),
        name="nsa_compressed_select",
    )(q, kv_cmp, kv_cmp, t_cmp, t_cmp, agg)


def _finish(o_ref, l_ref, acc_ref):
    o = _normalized(l_ref, acc_ref)
    for r in range(HEADS_PER_GROUP):
        o_ref[:, r * HEAD_DIM:(r + 1) * HEAD_DIM] = o[r * Q_BLOCK:(r + 1) * Q_BLOCK, :]


def _sel_win_kernel(*refs):
    (q_ref, sb_ref, ka_ref, vs_ref, ts_ref, tw_ref) = refs[:6]
    kw_refs = refs[6:6 + WIN_CHUNKS]
    vw_refs = refs[6 + WIN_CHUNKS:6 + 2 * WIN_CHUNKS]
    os_ref, ow_ref, qa_ref, sa_ref, sb2_ref, sw_ref, m_ref, l_ref, acc_ref = refs[6 + 2 * WIN_CHUNKS:]
    i = pl.program_id(1)
    head_rows = [slice(r * Q_BLOCK, (r + 1) * Q_BLOCK) for r in range(HEADS_PER_GROUP)]
    tiles_per_chunk = LANES * SEL_BLOCK // KEY_TILE
    for c in range(qa_ref.shape[0]):
        sb = sb_ref[:, c * LANES:(c + 1) * LANES]
        for r, rows in enumerate(head_rows):
            qa_ref[c, rows, :HEAD_DIM] = q_ref[r]
            qa_ref[c, rows, HEAD_DIM:] = sb

    _reset(m_ref, l_ref, acc_ref)
    last = i // TILE_CHUNKS

    def keys(j):
        return pl.ds(pl.multiple_of(j * KEY_TILE, KEY_TILE), KEY_TILE)

    def logits(j, buf):
        buf[...] = _nt_dot(qa_ref[j // tiles_per_chunk], ka_ref[keys(j), :])

    def attend(j, buf, biased=True):
        v = vs_ref[keys(j), :]
        first = SEL_BAND_BEHIND - (i - TILE_CHUNKS * j)
        for rows in head_rows:
            s = buf[rows, :]
            if biased:
                s = s + jnp.concatenate(
                    [ts_ref[jnp.maximum(first + c, 0), rows, :] for c in range(TILE_CHUNKS)], axis=1)
            _online_update(s, v, m_ref, l_ref, acc_ref, rows)

    def step(j, cur, nxt, biased=True):
        logits(j + 1, nxt)
        attend(j, cur, biased)

    kw = jnp.concatenate([kw_refs[b][...] for b in reversed(range(WIN_CHUNKS))], axis=0)
    vw = jnp.concatenate([vw_refs[b][...] for b in reversed(range(WIN_CHUNKS))], axis=0)
    lane_chunk = lax.broadcasted_iota(jnp.int32, (1, kw.shape[0]), 1) >> (LANES.bit_length() - 1)
    before_start = jnp.where(WIN_CHUNKS - 1 - lane_chunk > i, NEG_BIG, 0.0)
    sw_ref[...] = _nt_dot(q_ref[...].reshape(HEADS_PER_GROUP * Q_BLOCK, HEAD_DIM), kw)
    logits(0, sa_ref)
    for r, rows in enumerate(head_rows):
        s = sw_ref[rows, :] + (tw_ref[rows, :] + before_start)
        p = jnp.exp2(s - jnp.max(s, axis=1, keepdims=True))
        o = jnp.dot(p.astype(BF16), vw, preferred_element_type=F32)
        ow_ref[:, r * HEAD_DIM:(r + 1) * HEAD_DIM] = o / jnp.sum(p, axis=1, keepdims=True)

    far_tiles = jnp.maximum(i - SEL_BAND_BEHIND - 1 + TILE_CHUNKS, 0) // TILE_CHUNKS

    def steps(first, count, biased=True):
        for n in range(0, count, 2):
            step(first + n, sa_ref, sb2_ref, biased)
            step(first + n + 1, sb2_ref, sa_ref, biased)

    def run(first, count, biased):
        def body(p, carry):
            steps(first + UNROLL * p, UNROLL, biased)
            return carry
        lax.fori_loop(0, count, body, 0)

    far_runs = far_tiles // UNROLL
    run(0, far_runs, False)
    start = UNROLL * far_runs
    left = last - start + 1
    runs = (left - 1) // UNROLL
    run(start, runs, True)
    start = start + UNROLL * runs
    left = left - UNROLL * runs

    @pl.when(left > 2)
    def _():
        steps(start, 2)

    @pl.when(left % 2 == 1)
    def _():
        attend(last, sa_ref)

    @pl.when(left % 2 == 0)
    def _():
        step(last - 1, sa_ref, sb2_ref)
        attend(last, sb2_ref)

    _finish(os_ref, l_ref, acc_ref)


def _sel_win(q, selbias, k_aug, kv, t_sel, t_win):
    _, s, _ = q.shape
    g, r4 = N_KV_GROUPS, HEADS_PER_GROUP
    rows = r4 * Q_BLOCK
    nselp = selbias.shape[2]
    whole = lambda slot: pl.BlockSpec((None, s, HEAD_DIM), lambda gg, i: (slot * g + gg, 0, 0))

    def win(slot, behind):
        return pl.BlockSpec((None, Q_BLOCK, HEAD_DIM),
                            lambda gg, i: (slot * g + gg, jnp.maximum(i - behind, 0), 0))

    in_specs = [pl.BlockSpec((r4, Q_BLOCK, HEAD_DIM), lambda gg, i: (gg, i, 0)),
                pl.BlockSpec((None, Q_BLOCK, nselp), lambda gg, i: (gg, i, 0)),
                pl.BlockSpec((None, s, 2 * HEAD_DIM), lambda gg, i: (gg, 0, 0)),
                whole(1),
                pl.BlockSpec((None, SEL_BAND_CHUNKS, rows, LANES), lambda gg, i: (gg, 0, 0, 0)),
                pl.BlockSpec((None, rows, WINDOW + Q_BLOCK), lambda gg, i: (gg, 0, 0))]
    in_specs += [win(2, b) for b in range(WIN_CHUNKS)] + [win(3, b) for b in range(WIN_CHUNKS)]
    out = pl.BlockSpec((Q_BLOCK, rows), lambda gg, i: (i, gg))
    return pl.pallas_call(
        _sel_win_kernel,
        grid=(g, s // Q_BLOCK),
        in_specs=in_specs,
        out_specs=[out, out],
        out_shape=[jax.ShapeDtypeStruct((s, N_HEADS * HEAD_DIM), F32)] * 2,
        scratch_shapes=[pltpu.VMEM((nselp // LANES, rows, 2 * HEAD_DIM), BF16),
                        pltpu.VMEM((rows, KEY_TILE), F32), pltpu.VMEM((rows, KEY_TILE), F32),
                        pltpu.VMEM((rows, WINDOW + Q_BLOCK), F32),
                        pltpu.VMEM((rows, LANES), F32), pltpu.VMEM((rows, LANES), F32),
                        pltpu.VMEM((rows, HEAD_DIM), F32)],
        compiler_params=_params(("parallel", "arbitrary")),
        name="nsa_selected_window",
    )(q, selbias, k_aug, kv, t_sel, t_win, *([kv] * (2 * WIN_CHUNKS)))


def _selection_aggregator(ncp, nselp):
    n = np.arange(ncp)[:, None]
    j = np.arange(nselp)[None, :]
    hit = (n >= SEL_RATIO * j - 1) & (n <= SEL_RATIO * j + SEL_RATIO - 1) & (n < ncp - 1)
    return jnp.asarray(hit.astype(np.float32), dtype=BF16)


def _block_onehot(s):
    blk = (np.arange(s) // SEL_BLOCK) % LANES
    return jnp.asarray((blk[:, None] == np.arange(LANES)[None, :]).astype(np.float32), dtype=BF16)


def _round_up(x, m):
    return (x + m - 1) // m * m


def kernel(x, fox_w_in, fox_b_f, fox_w_o, nsa_w_in, nsa_w_o, kv_w, cmp_pos_k, cmp_pos_v, cmp_k_w1, cmp_k_w2, cmp_v_w1, cmp_v_w2, rel_bias, mlp_w1, mlp_w2, ln1_g, ln1_b, ln2_g, ln2_b):
    b, s, d = x.shape
    assert b == 1 and d == D_MODEL and s % (CMP_STRIDE * LANES) == 0
    hd = N_HEADS * HEAD_DIM
    scale = HEAD_DIM ** -0.5 * LOG2E
    tm = min(512, s)
    tp = min(1024, s)
    fox_t = min(512, s)
    pad_cols = lambda w: jnp.pad(w, ((0, 0), (0, LANES - w.shape[1])))

    fox_in, fox_out = fox_w_in.astype(BF16), fox_w_o.astype(BF16)
    nsa_in, nsa_out = nsa_w_in.astype(BF16), nsa_w_o.astype(BF16)
    w1_all, w2_all = mlp_w1.astype(BF16), mlp_w2.astype(BF16)
    project = functools.partial(_matmul, tm=tp, tn=512, out_dtype=BF16, head_major=True)
    gate_project = functools.partial(_matmul, tm=tp, tn=LANES, out_dtype=F32, head_major=False)

    h = x[0]
    hb = h.astype(BF16)
    kv_state = None
    for layer in range(DEPTH):
        g1, b1 = ln1_g[layer][None, :], ln1_b[layer][None, :]
        g2, b2 = ln2_g[layer][None, :], ln2_b[layer][None, :]
        if layer < N_A_LAYERS:
            qkv = project(hb, fox_in, layer=layer, n=3 * hd, scaled_cols=hd, scale=scale)
            gate_logits = gate_project(hb, pad_cols(fox_in[layer][:, 3 * hd:]))
            cum = _forget_cumsum(gate_logits, fox_b_f[layer])
            cum_rows = cum[:, :N_HEADS].T.reshape(N_HEADS, s // fox_t, 1, fox_t)
            attn = _fox_attention(qkv, cum_rows, _first_fox_tile(qkv, cum, t=fox_t), t=fox_t)
            h, hb = _oproj(attn, fox_out, layer, h, g1, b1, tm=tm)
        else:
            nsa_layer = layer - N_A_LAYERS
            k_cmp_v_cmp, kv_tok, k_aug, tables, agg, n_top = kv_state
            t_sel, t_win, t_cmp = tables
            q = project(hb, nsa_in, layer=nsa_layer, n=hd, scaled_cols=hd, scale=scale)
            gate_logits = gate_project(hb, pad_cols(nsa_in[nsa_layer][:, hd:]))
            o_cmp, selbias = _cmp_select(q, k_cmp_v_cmp, t_cmp, agg, n_top=n_top)
            o_sel, o_win = _sel_win(q, selbias, k_aug, kv_tok, t_sel, t_win)
            h, hb = _nsa_oproj(o_cmp, o_sel, o_win, gate_logits, nsa_out, nsa_layer, h, g1, b1,
                               tm=min(256, s))
        h, hb = _mlp(hb, w1_all, w2_all, layer, h, g2, b2, tm=tm, tf=1024)
        if layer == N_A_LAYERS - 1:
            gd = N_KV_GROUPS * HEAD_DIM
            kvw = kv_w.astype(BF16)
            kv_raw = project(hb, kvw, n=2 * gd, out_dtype=F32)
            kv_tok = project(hb, kvw, n=4 * gd, first_col=2 * gd)
            pos = jnp.stack([cmp_pos_k.reshape(1, -1), cmp_pos_v.reshape(1, -1)])
            w1 = jnp.stack([cmp_k_w1, cmp_v_w1]).astype(BF16)
            w2 = jnp.stack([cmp_k_w2, cmp_v_w2]).astype(BF16)
            kv_cmp = _compress(kv_raw, pos, w1, w2)
            n_sel = s // SEL_BLOCK
            nselp = _round_up(n_sel, LANES)
            onehot = jnp.broadcast_to(_block_onehot(s), (N_KV_GROUPS, s, LANES))
            k_aug = jnp.concatenate([kv_tok[:N_KV_GROUPS], onehot], axis=-1)
            tables = _bias_tables(_bias_vec(rel_bias))
            agg = _selection_aggregator(s // CMP_STRIDE, nselp)
            kv_state = (kv_cmp, kv_tok, k_aug, tables, agg, min(N_SELECTED, n_sel))
    return h[None]
```

```python
import functools
import math

import numpy as np
import jax
import jax.numpy as jnp
from jax import lax
from jax.experimental import pallas as pl
from jax.experimental.pallas import tpu as pltpu

D_MODEL = 2048
DEPTH = 4
HEAD_DIM = 128
N_HEADS = D_MODEL // HEAD_DIM
N_KV_GROUPS = 4
HEADS_PER_GROUP = N_HEADS // N_KV_GROUPS
D_FF = 4 * D_MODEL
N_A_LAYERS = DEPTH // 2
Q_BLOCK = 128
CMP_BLOCK = 32
CMP_STRIDE = 16
CMP_HIDDEN = 256
SEL_BLOCK = 64
N_SELECTED = 16
SEL_RATIO = SEL_BLOCK // CMP_STRIDE
WINDOW = 512
N_BUCKETS = 32
REL_MAX_DIST = 2048
ALPHA = (2.0 * DEPTH) ** 0.25
LN_EPS = 1e-5
N_FORCED = 3
PICKED = -2.0
NEG_BIG = -1e30
LOG2E = math.log2(math.e)

LANES = 128
MXU_WIDTH = 256
VMEM_LIMIT = 56 * 1024 * 1024
BF16 = jnp.bfloat16
F32 = jnp.float32

_EXACT = N_BUCKETS // 2
_BUCKET_THRESHOLDS = tuple(
    int(math.ceil(_EXACT * (REL_MAX_DIST / _EXACT) ** (k / (N_BUCKETS - _EXACT))))
    for k in range(1, N_BUCKETS - _EXACT))
BIAS_RANGE = 2048
KEY_TILE = 512
TILE_CHUNKS = KEY_TILE // LANES
SEL_BAND_BEHIND = BIAS_RANGE // LANES + TILE_CHUNKS - 2
SEL_BAND_CHUNKS = SEL_BAND_BEHIND + TILE_CHUNKS
WIN_CHUNKS = WINDOW // Q_BLOCK + 1
UNROLL = 4
FAR_UNROLL = 8
CMP_TABLES = 32
SUB_ROWS = 128
FOX_SKIP_MARGIN = 160.0


def _nt_dot(a, b):
    return lax.dot_general(a, b, (((1,), (1,)), ((), ())), preferred_element_type=F32)


def _params(sem):
    return pltpu.CompilerParams(dimension_semantics=sem, vmem_limit_bytes=VMEM_LIMIT)


def _layer_norm(y, g, b):
    mu = jnp.mean(y, axis=-1, keepdims=True)
    yc = y - mu
    var = jnp.mean(yc * yc, axis=-1, keepdims=True)
    return yc * lax.rsqrt(var + LN_EPS) * g + b


def _matmul_kernel(x_ref, w_ref, o_ref, *, tn, scaled_cols, scale, head_major):
    acc = jnp.dot(x_ref[...], w_ref[...], preferred_element_type=F32)
    if scaled_cols:
        j = pl.program_id(1)
        acc = acc * jnp.where(j * tn < scaled_cols, scale, 1.0).astype(F32)
    if head_major:
        for c in range(tn // LANES):
            o_ref[c] = acc[:, c * LANES:(c + 1) * LANES].astype(o_ref.dtype)
    else:
        o_ref[...] = acc.astype(o_ref.dtype)


def _weight_spec(w, layer, block, index_map):
    if w.ndim == 2:
        return pl.BlockSpec(block, index_map)
    return pl.BlockSpec((None,) + block, lambda *idx: (layer,) + index_map(*idx))


def _matmul(x, w, *, n=None, first_col=0, layer=None, out_dtype, head_major, tm, tn,
            scaled_cols=0, scale=1.0):
    m, k = x.shape
    n = w.shape[-1] if n is None else n
    assert m % tm == 0 and n % tn == 0 and scaled_cols % tn == 0 and first_col % tn == 0
    col0 = first_col // tn
    w_spec = _weight_spec(w, layer, (k, tn), lambda i, j: (0, col0 + j))
    if head_major:
        out_shape = jax.ShapeDtypeStruct((n // LANES, m, LANES), out_dtype)
        out_spec = pl.BlockSpec((tn // LANES, tm, LANES), lambda i, j: (j, i, 0))
    else:
        out_shape = jax.ShapeDtypeStruct((m, n), out_dtype)
        out_spec = pl.BlockSpec((tm, tn), lambda i, j: (i, j))
    return pl.pallas_call(
        functools.partial(_matmul_kernel, tn=tn, scaled_cols=scaled_cols, scale=scale,
                          head_major=head_major),
        grid=(m // tm, n // tn),
        in_specs=[pl.BlockSpec((tm, k), lambda i, j: (i, 0)), w_spec],
        out_specs=out_spec,
        out_shape=out_shape,
        compiler_params=_params(("parallel", "arbitrary")),
        name="proj_matmul",
    )(x, w)


def _cum_kernel(gl_ref, b_ref, o_ref, *, nblk):
    r = lax.broadcasted_iota(jnp.int32, (LANES, LANES), 0)
    c = lax.broadcasted_iota(jnp.int32, (LANES, LANES), 1)
    tri = (c <= r).astype(F32)
    bias = b_ref[...]

    def body(i, carry):
        rows = pl.ds(pl.multiple_of(i * LANES, LANES), LANES)
        lf = jax.nn.log_sigmoid(gl_ref[rows, :] + bias)
        cum = jnp.dot(tri, lf, preferred_element_type=F32,
                      precision=lax.Precision.HIGHEST) + carry
        o_ref[rows, :] = cum
        return cum[LANES - 1:LANES, :]

    lax.fori_loop(0, nblk, body, jnp.zeros((1, LANES), F32))


def _forget_cumsum(gate_logits, b_f):
    s = gate_logits.shape[0]
    bias = jnp.zeros((1, LANES), F32).at[0, :N_HEADS].set(b_f)
    return pl.pallas_call(
        functools.partial(_cum_kernel, nblk=s // LANES),
        out_shape=jax.ShapeDtypeStruct((s, LANES), F32),
        compiler_params=pltpu.CompilerParams(vmem_limit_bytes=VMEM_LIMIT),
        name="forget_cumsum",
    )(gate_logits, bias)


def _online_update(s, v, m_ref, l_ref, acc_ref, rows):
    m_prev = m_ref[rows, :]
    m_next = jnp.maximum(m_prev, jnp.max(s, axis=1, keepdims=True))
    alpha = jnp.exp2(m_prev - m_next)
    reps = s.shape[1] // LANES
    p = jnp.exp2(s - jnp.concatenate([m_next] * reps, axis=1))
    part = p[:, :LANES]
    for c in range(1, reps):
        part = part + p[:, c * LANES:(c + 1) * LANES]
    l_ref[rows, :] = alpha * l_ref[rows, :] + part
    acc_ref[rows, :] = alpha * acc_ref[rows, :] + jnp.dot(p.astype(BF16), v,
                                                          preferred_element_type=F32)
    m_ref[rows, :] = m_next


def _reset(m_ref, l_ref, acc_ref):
    m_ref[...] = jnp.full_like(m_ref, NEG_BIG)
    l_ref[...] = jnp.zeros_like(l_ref)
    acc_ref[...] = jnp.zeros_like(acc_ref)


def _normalized(l_ref, acc_ref):
    return acc_ref[...] / jnp.sum(l_ref[...], axis=1, keepdims=True)


def _fox_kernel(first_ref, q_ref, k_ref, v_ref, c_ref, o_ref, sa_ref, sb_ref, m_ref, l_ref, acc_ref, *, t):
    i = pl.program_id(1)
    c0 = jnp.max(c_ref[i], axis=1, keepdims=True)
    _reset(m_ref, l_ref, acc_ref)

    def keys(j):
        return pl.ds(pl.multiple_of(j * t, t), t)

    def logits(j, buf):
        buf[...] = _nt_dot(q_ref[...], k_ref[keys(j), :]) + (c0 - c_ref[j]) * LOG2E

    def attend(j, buf, masked):
        v = v_ref[keys(j), :]
        for r in range(t // SUB_ROWS):
            rows = slice(r * SUB_ROWS, (r + 1) * SUB_ROWS)
            s = buf[rows, :]
            if masked:
                row = r * SUB_ROWS + lax.broadcasted_iota(jnp.int32, (SUB_ROWS, t), 0)
                col = lax.broadcasted_iota(jnp.int32, (SUB_ROWS, t), 1)
                s = jnp.where(col <= row, s, NEG_BIG)
            _online_update(s, v, m_ref, l_ref, acc_ref, rows)

    def step(j, cur, nxt):
        logits(j + 1, nxt)
        attend(j, cur, False)

    first = first_ref[pl.program_id(0), i]
    n = i - first
    logits(first, sa_ref)

    def pair(p, carry):
        step(first + 2 * p, sa_ref, sb_ref)
        step(first + 2 * p + 1, sb_ref, sa_ref)
        return carry

    lax.fori_loop(0, n // 2, pair, 0)

    @pl.when(n % 2 == 0)
    def _():
        attend(i, sa_ref, True)

    @pl.when(n % 2 == 1)
    def _():
        step(i - 1, sa_ref, sb_ref)
        attend(i, sb_ref, True)

    o_ref[...] = _normalized(l_ref, acc_ref).astype(o_ref.dtype)


def _first_fox_tile(qkv, cum, *, t):
    h = N_HEADS
    s = qkv.shape[1]
    norm = lambda a: jnp.sqrt(jnp.max(jnp.sum(jnp.square(a.astype(F32)), axis=-1), axis=-1))
    bound = FOX_SKIP_MARGIN + 2.0 * norm(qkv[:h]) * norm(qkv[h:2 * h])
    ct = cum[:, :h].T
    at_query_start = ct[:, ::t]
    at_key_end = ct[:, t - 1::t]
    gap = (at_key_end[:, None, :] - at_query_start[:, :, None]) * LOG2E
    idx = np.arange(s // t)
    earlier = jnp.asarray(idx[None, :] < idx[:, None])
    skip = earlier[None] & (gap >= bound[:, None, None])
    return jnp.sum(skip, axis=-1).astype(jnp.int32)


def _fox_attention(qkv, cum_rows, first_tile, *, t):
    _, s, _ = qkv.shape
    h = N_HEADS
    grid_spec = pltpu.PrefetchScalarGridSpec(
        num_scalar_prefetch=1,
        grid=(h, s // t),
        in_specs=[pl.BlockSpec((None, t, HEAD_DIM), lambda hh, i, f: (hh, i, 0)),
                  pl.BlockSpec((None, s, HEAD_DIM), lambda hh, i, f: (h + hh, 0, 0)),
                  pl.BlockSpec((None, s, HEAD_DIM), lambda hh, i, f: (2 * h + hh, 0, 0)),
                  pl.BlockSpec((None, s // t, 1, t), lambda hh, i, f: (hh, 0, 0, 0))],
        out_specs=pl.BlockSpec((t, HEAD_DIM), lambda hh, i, f: (i, hh)),
        scratch_shapes=[pltpu.VMEM((t, t), F32), pltpu.VMEM((t, t), F32),
                        pltpu.VMEM((t, LANES), F32), pltpu.VMEM((t, LANES), F32),
                        pltpu.VMEM((t, HEAD_DIM), F32)])
    return pl.pallas_call(
        functools.partial(_fox_kernel, t=t),
        grid_spec=grid_spec,
        out_shape=jax.ShapeDtypeStruct((s, h * HEAD_DIM), BF16),
        compiler_params=_params(("parallel", "arbitrary")),
        name="fox_attention",
    )(first_tile, qkv, qkv, qkv, cum_rows)


def _emit_norm(y, g_ref, b_ref, of_ref, ob_ref):
    hn = _layer_norm(y, g_ref[...], b_ref[...])
    of_ref[...] = hn
    ob_ref[...] = hn.astype(BF16)


def _oproj_kernel(a_ref, w_ref, h_ref, g_ref, b_ref, of_ref, ob_ref):
    mix = jnp.dot(a_ref[...], w_ref[...], preferred_element_type=F32)
    _emit_norm(ALPHA * h_ref[...] + mix, g_ref, b_ref, of_ref, ob_ref)


def _row_spec(tm, n):
    return pl.BlockSpec((tm, n), lambda i: (i, 0))


def _const_spec(shape):
    return pl.BlockSpec(shape, lambda i: (0,) * len(shape))


def _norm_outs(s, tm):
    d = D_MODEL
    return dict(
        out_specs=[_row_spec(tm, d), _row_spec(tm, d)],
        out_shape=[jax.ShapeDtypeStruct((s, d), F32), jax.ShapeDtypeStruct((s, d), BF16)])


def _oproj(a, w, layer, h, g, b, *, tm):
    s, d = h.shape
    return pl.pallas_call(
        _oproj_kernel,
        grid=(s // tm,),
        in_specs=[_row_spec(tm, d), _weight_spec(w, layer, (d, d), lambda i: (0, 0)), _row_spec(tm, d),
                  _const_spec((1, d)), _const_spec((1, d))],
        compiler_params=_params(("parallel",)),
        name="oproj_norm",
        **_norm_outs(s, tm),
    )(a, w, h, g, b)


def _mlp_kernel(x_ref, w1_ref, w2_ref, h_ref, g_ref, b_ref, of_ref, ob_ref, acc_ref):
    f = pl.program_id(1)

    @pl.when(f == 0)
    def _():
        acc_ref[...] = jnp.zeros_like(acc_ref)

    x = x_ref[...]
    chunks = [slice(c * MXU_WIDTH, (c + 1) * MXU_WIDTH) for c in range(w1_ref.shape[1] // MXU_WIDTH)]
    hidden = [jnp.dot(x, w1_ref[:, cols], preferred_element_type=F32) for cols in chunks]
    for cols, u in zip(chunks, hidden):
        u = jnp.maximum(u, 0.0)
        acc_ref[...] += jnp.dot((u * u).astype(BF16), w2_ref[cols, :], preferred_element_type=F32)

    @pl.when(f == pl.num_programs(1) - 1)
    def _():
        _emit_norm(ALPHA * h_ref[...] + acc_ref[...], g_ref, b_ref, of_ref, ob_ref)


def _mlp(xb, w1, w2, layer, h, g, b, *, tm, tf):
    s, d = h.shape
    ff = w1.shape[-1]
    row = lambda n: pl.BlockSpec((tm, n), lambda i, f: (i, 0))
    vec = pl.BlockSpec((1, d), lambda i, f: (0, 0))
    return pl.pallas_call(
        _mlp_kernel,
        grid=(s // tm, ff // tf),
        in_specs=[row(d), _weight_spec(w1, layer, (d, tf), lambda i, f: (0, f)),
                  _weight_spec(w2, layer, (tf, d), lambda i, f: (f, 0)), row(d), vec, vec],
        out_specs=[row(d), row(d)],
        out_shape=[jax.ShapeDtypeStruct((s, d), F32), jax.ShapeDtypeStruct((s, d), BF16)],
        scratch_shapes=[pltpu.VMEM((tm, d), F32)],
        compiler_params=_params(("parallel", "arbitrary")),
        name="mlp_norm",
    )(xb, w1, w2, h, g, b)


def _compress_kernel(x_ref, pos_ref, w1_ref, w2_ref, o_ref):
    half = CMP_STRIDE * HEAD_DIM
    x = x_ref[...]
    first = jnp.dot((x + pos_ref[:, :half]).astype(BF16), w1_ref[:half, :],
                    preferred_element_type=F32)
    second = jnp.dot((x + pos_ref[:, half:]).astype(BF16), w1_ref[half:, :],
                     preferred_element_type=F32)
    n = x.shape[0]
    hidden = first + pltpu.roll(second, n - 1, 0)
    act = jax.nn.gelu(hidden)
    o_ref[...] = jnp.dot(act.astype(BF16), w2_ref[...], preferred_element_type=F32).astype(BF16)


def _compress(kv_raw, pos, w1, w2):
    g = N_KV_GROUPS
    s = kv_raw.shape[1]
    nc = s // CMP_STRIDE
    wide = CMP_STRIDE * HEAD_DIM
    x = kv_raw.reshape(2 * g, nc, wide)
    return pl.pallas_call(
        _compress_kernel,
        grid=(2, g),
        in_specs=[pl.BlockSpec((None, nc, wide), lambda a, gg: (a * g + gg, 0, 0)),
                  pl.BlockSpec((None, 1, 2 * wide), lambda a, gg: (a, 0, 0)),
                  pl.BlockSpec((None, 2 * wide, CMP_HIDDEN), lambda a, gg: (a, 0, 0)),
                  pl.BlockSpec((None, CMP_HIDDEN, HEAD_DIM), lambda a, gg: (a, 0, 0))],
        out_specs=pl.BlockSpec((None, None, nc, HEAD_DIM), lambda a, gg: (a, gg, 0, 0)),
        out_shape=jax.ShapeDtypeStruct((2, g, nc, HEAD_DIM), BF16),
        compiler_params=_params(("parallel", "parallel")),
        name="compress_kv",
    )(x, pos, w1, w2)


def _bias_vec_kernel(t_ref, o_ref):
    d = lax.broadcasted_iota(jnp.int32, (1, BIAS_RANGE), 1)
    large = jnp.full_like(d, _EXACT)
    for thr in _BUCKET_THRESHOLDS:
        large = large + (d >= thr).astype(jnp.int32)
    bucket = jnp.where(d < _EXACT, d, large)
    table = t_ref[...]
    acc = jnp.zeros((N_HEADS, BIAS_RANGE), F32)
    for bkt in range(N_BUCKETS):
        acc = jnp.where(bucket == bkt, table[:, bkt:bkt + 1], acc)
    o_ref[...] = (acc - table[:, N_BUCKETS - 1:N_BUCKETS]) * LOG2E


def _bias_vec(rel_bias):
    return pl.pallas_call(
        _bias_vec_kernel,
        out_shape=jax.ShapeDtypeStruct((N_HEADS, BIAS_RANGE), F32),
        name="rel_bias_by_distance",
    )(rel_bias.T)


def _toeplitz(w, rows, cols):
    h, width = w.shape
    flat = jnp.tile(w, (1, rows))[:, :rows * (width - 1)]
    return flat.reshape(h, rows, width - 1)[:, :, :cols]


def _by_distance(vec, d_hi, d_lo, window=None):
    h = vec.shape[0]
    limit = BIAS_RANGE if window is None else window
    above = jnp.full((h, d_hi - limit + 1), 0.0 if window is None else NEG_BIG, F32)
    below = jnp.full((h, -d_lo), NEG_BIG, F32)
    return jnp.concatenate([above, vec[:, limit - 1::-1], below], axis=1)


def _per_group(t):
    n = t.shape[2] // LANES
    t = t.reshape(N_KV_GROUPS, HEADS_PER_GROUP * Q_BLOCK, n, LANES)
    return t.transpose(0, 2, 1, 3)


def _bias_tables(vec):
    pad = jnp.zeros((N_HEADS, Q_BLOCK), F32)
    behind = SEL_BAND_BEHIND * LANES
    ahead = (SEL_BAND_CHUNKS - SEL_BAND_BEHIND) * LANES
    w = jnp.concatenate([_by_distance(vec, behind, 1 - ahead), pad], axis=1)
    t_sel = _per_group(_toeplitz(w, Q_BLOCK, SEL_BAND_CHUNKS * LANES))
    w = jnp.concatenate([_by_distance(vec, WINDOW, 1 - Q_BLOCK, window=WINDOW), pad + NEG_BIG], axis=1)
    t_win = _toeplitz(w, Q_BLOCK, WINDOW + Q_BLOCK)
    t_win = t_win.reshape(N_KV_GROUPS, HEADS_PER_GROUP * Q_BLOCK, WINDOW + Q_BLOCK)
    lo_rel = -(CMP_TABLES - 1) * (Q_BLOCK // CMP_STRIDE)
    n_rel = LANES - lo_rel
    end_min = CMP_STRIDE * lo_rel + CMP_BLOCK - 1
    w = jnp.concatenate([_by_distance(vec, -end_min, 1 - end_min - CMP_STRIDE * n_rel), pad], axis=1)
    band = _toeplitz(w, Q_BLOCK, CMP_STRIDE * n_rel)[:, :, ::CMP_STRIDE]
    per_tile = Q_BLOCK // CMP_STRIDE
    t_cmp = jnp.concatenate(
        [band[:, :, -lo_rel - per_tile * e:-lo_rel - per_tile * e + LANES] for e in range(CMP_TABLES)], axis=2)
    return t_sel, t_win, _per_group(t_cmp)


def _cmp_kernel(q_ref, kc_ref, vc_ref, t0_ref, t1_ref, a_ref, o_ref, sel_ref, lg_ref, p_ref, imp_ref,
                *, n_top):
    i = pl.program_id(1)
    ncp = kc_ref.shape[0]
    nselp = a_ref.shape[1]
    chunks_per_tile = Q_BLOCK // CMP_STRIDE
    c_hi = (i * chunks_per_tile) // LANES
    lg_ref[...] = _nt_dot(q_ref[...].reshape(HEADS_PER_GROUP * Q_BLOCK, HEAD_DIM), kc_ref[...])
    chunks = [slice(c * LANES, (c + 1) * LANES) for c in range(ncp // LANES)]
    heads = [slice(r * Q_BLOCK, (r + 1) * Q_BLOCK) for r in range(HEADS_PER_GROUP)]
    row_max = []
    for rows in heads:
        b0, b1 = t0_ref[rows, :], t1_ref[rows, :]
        top = None
        for c, cols in enumerate(chunks):
            bias = jnp.where(c == c_hi, b0, jnp.where(c == c_hi - 1, b1,
                                                      jnp.where(c > c_hi, NEG_BIG, 0.0)))
            x = lg_ref[rows, cols] + bias
            lg_ref[rows, cols] = x
            top = x if top is None else jnp.maximum(top, x)
        row_max.append(jnp.max(top, axis=1, keepdims=True))
    scale = []
    for rows, m in zip(heads, row_max):
        part = None
        for cols in chunks:
            e = jnp.exp2(lg_ref[rows, cols] - m)
            lg_ref[rows, cols] = e
            part = e if part is None else part + e
        total = jnp.maximum(jnp.sum(part, axis=1, keepdims=True), 1e-30)
        scale.append(jnp.where(m > 0.5 * NEG_BIG, 1.0 / total, 0.0))
    for cols in chunks:
        imp = None
        for rows, inv in zip(heads, scale):
            p = lg_ref[rows, cols] * inv
            p_ref[rows, cols] = p.astype(BF16)
            imp = p if imp is None else imp + p
        imp_ref[:, cols] = imp
    o = jnp.dot(p_ref[...], vc_ref[...], preferred_element_type=F32)
    for r in range(HEADS_PER_GROUP):
        o_ref[:, r * HEAD_DIM:(r + 1) * HEAD_DIM] = o[r * Q_BLOCK:(r + 1) * Q_BLOCK, :]
    imp = imp_ref[...]

    hi = imp.astype(BF16)
    rest = imp - hi.astype(F32)
    mid = rest.astype(BF16)
    low = (rest - mid.astype(F32)).astype(BF16)
    a = a_ref[...]
    imp_sel = (jnp.dot(hi, a, preferred_element_type=F32) + jnp.dot(mid, a, preferred_element_type=F32)
               + jnp.dot(low, a, preferred_element_type=F32))
    imp_sel = imp_sel.T
    t = i * Q_BLOCK + lax.broadcasted_iota(jnp.int32, (1, Q_BLOCK), 1)
    blk_t = t >> (SEL_BLOCK.bit_length() - 1)
    j = lax.broadcasted_iota(jnp.int32, (nselp, 1), 0)
    forced = (j == 0) | (j == blk_t) | (j == blk_t - 1)
    score = jnp.where(forced, PICKED, jnp.where(j <= blk_t, imp_sel, -1.0))
    jf = jnp.broadcast_to(j.astype(F32), (nselp, Q_BLOCK))

    def pick(_, score):
        top = jnp.max(score, axis=0, keepdims=True)
        first = jnp.min(jnp.where(score == top, jf, float(nselp)), axis=0, keepdims=True)
        return jnp.where(jf == first, PICKED, score)

    score = lax.fori_loop(0, n_top - N_FORCED, pick, score)
    sel_ref[...] = jnp.where(score == PICKED, 0.0, NEG_BIG).T.astype(BF16)


def _cmp_select(q, kv_cmp, t_cmp, agg, *, n_top):
    _, s, _ = q.shape
    g, r4 = N_KV_GROUPS, HEADS_PER_GROUP
    ncp = kv_cmp.shape[2]
    nselp = agg.shape[1]
    rows = r4 * Q_BLOCK
    tiles_per_chunk = LANES * CMP_STRIDE // Q_BLOCK
    return pl.pallas_call(
        functools.partial(_cmp_kernel, n_top=n_top),
        grid=(g, s // Q_BLOCK),
        in_specs=[pl.BlockSpec((r4, Q_BLOCK, HEAD_DIM), lambda gg, i: (gg, i, 0)),
                  pl.BlockSpec((None, None, ncp, HEAD_DIM), lambda gg, i: (0, gg, 0, 0)),
                  pl.BlockSpec((None, None, ncp, HEAD_DIM), lambda gg, i: (1, gg, 0, 0)),
                  pl.BlockSpec((None, None, rows, LANES),
                               lambda gg, i: (gg, i % tiles_per_chunk, 0, 0)),
                  pl.BlockSpec((None, None, rows, LANES),
                               lambda gg, i: (gg, i % tiles_per_chunk + tiles_per_chunk, 0, 0)),
                  pl.BlockSpec((ncp, nselp), lambda gg, i: (0, 0))],
        out_specs=[pl.BlockSpec((Q_BLOCK, rows), lambda gg, i: (i, gg)),
                   pl.BlockSpec((None, Q_BLOCK, nselp), lambda gg, i: (gg, i, 0))],
        out_shape=[jax.ShapeDtypeStruct((s, N_HEADS * HEAD_DIM), F32),
                   jax.ShapeDtypeStruct((g, s, nselp), BF16)],
        scratch_shapes=[pltpu.VMEM((rows, ncp), F32), pltpu.VMEM((rows, ncp), BF16),
                        pltpu.VMEM((Q_BLOCK, ncp), F32)],
        compiler_params=_params(("parallel", "arbitrary")),
        name="nsa_compressed_select",
    )(q, kv_cmp, kv_cmp, t_cmp, t_cmp, agg)


def _sel_win_kernel(*refs):
    (q_ref, sb_ref, ka_ref, vs_ref, ts_ref, tw_ref, oc_ref, gl_ref) = refs[:8]
    kw_refs = refs[8:8 + WIN_CHUNKS]
    vw_refs = refs[8 + WIN_CHUNKS:8 + 2 * WIN_CHUNKS]
    (x_ref, qa_ref, sa_ref, sb2_ref, sw_ref, ow_ref, m_ref, l_ref, acc_ref) = refs[8 + 2 * WIN_CHUNKS:]
    i = pl.program_id(1)
    head_rows = [slice(r * Q_BLOCK, (r + 1) * Q_BLOCK) for r in range(HEADS_PER_GROUP)]
    tiles_per_chunk = LANES * SEL_BLOCK // KEY_TILE
    for c in range(qa_ref.shape[0]):
        sb = sb_ref[:, c * LANES:(c + 1) * LANES]
        for r, rows in enumerate(head_rows):
            qa_ref[c, rows, :HEAD_DIM] = q_ref[r]
            qa_ref[c, rows, HEAD_DIM:] = sb

    _reset(m_ref, l_ref, acc_ref)
    last = i // TILE_CHUNKS

    def keys(j):
        return pl.ds(pl.multiple_of(j * KEY_TILE, KEY_TILE), KEY_TILE)

    def logits(j, buf):
        buf[...] = _nt_dot(qa_ref[j // tiles_per_chunk], ka_ref[keys(j), :])

    def attend(j, buf, biased=True):
        v = vs_ref[keys(j), :]
        first = SEL_BAND_BEHIND - (i - TILE_CHUNKS * j)
        for rows in head_rows:
            s = buf[rows, :]
            if biased:
                s = s + jnp.concatenate(
                    [ts_ref[jnp.maximum(first + c, 0), rows, :] for c in range(TILE_CHUNKS)], axis=1)
            _online_update(s, v, m_ref, l_ref, acc_ref, rows)

    def step(j, cur, nxt, biased=True):
        logits(j + 1, nxt)
        attend(j, cur, biased)

    kw = jnp.concatenate([kw_refs[b][...] for b in reversed(range(WIN_CHUNKS))], axis=0)
    vw = jnp.concatenate([vw_refs[b][...] for b in reversed(range(WIN_CHUNKS))], axis=0)
    lane_chunk = lax.broadcasted_iota(jnp.int32, (1, kw.shape[0]), 1) >> (LANES.bit_length() - 1)
    before_start = jnp.where(WIN_CHUNKS - 1 - lane_chunk > i, NEG_BIG, 0.0)
    sw_ref[...] = _nt_dot(q_ref[...].reshape(HEADS_PER_GROUP * Q_BLOCK, HEAD_DIM), kw)
    logits(0, sa_ref)
    for r, rows in enumerate(head_rows):
        s = sw_ref[rows, :] + (tw_ref[rows, :] + before_start)
        p = jnp.exp2(s - jnp.max(s, axis=1, keepdims=True))
        o = jnp.dot(p.astype(BF16), vw, preferred_element_type=F32)
        ow_ref[rows, :] = o / jnp.sum(p, axis=1, keepdims=True)

    far_tiles = jnp.maximum(i - SEL_BAND_BEHIND - 1 + TILE_CHUNKS, 0) // TILE_CHUNKS

    def steps(first, count, biased=True):
        for n in range(0, count, 2):
            step(first + n, sa_ref, sb2_ref, biased)
            step(first + n + 1, sb2_ref, sa_ref, biased)

    def run(first, count, unroll, biased):
        def body(p, carry):
            steps(first + unroll * p, unroll, biased)
            return carry
        lax.fori_loop(0, count, body, 0)

    far_runs = far_tiles // FAR_UNROLL
    run(0, far_runs, FAR_UNROLL, False)
    start = FAR_UNROLL * far_runs
    left = last - start + 1
    runs = (left - 1) // UNROLL
    run(start, runs, UNROLL, True)
    start = start + UNROLL * runs
    left = left - UNROLL * runs

    @pl.when(left > 2)
    def _():
        steps(start, 2)

    @pl.when(left % 2 == 1)
    def _():
        attend(last, sa_ref)

    @pl.when(left % 2 == 0)
    def _():
        step(last - 1, sa_ref, sb2_ref)
        attend(last, sb2_ref)

    gates = jax.nn.sigmoid(gl_ref[...])
    o_sel = _normalized(l_ref, acc_ref)
    for r, rows in enumerate(head_rows):
        cols = slice(r * HEAD_DIM, (r + 1) * HEAD_DIM)
        mix = (gates[:, 3 * r:3 * r + 1] * oc_ref[:, cols] + gates[:, 3 * r + 1:3 * r + 2] * o_sel[rows, :]
               + gates[:, 3 * r + 2:3 * r + 3] * ow_ref[rows, :])
        x_ref[:, cols] = mix.astype(BF16)


def _sel_win(q, selbias, k_aug, kv, t_sel, t_win, o_cmp, gate_logits):
    _, s, _ = q.shape
    g, r4 = N_KV_GROUPS, HEADS_PER_GROUP
    rows = r4 * Q_BLOCK
    nselp = selbias.shape[2]
    whole = lambda slot: pl.BlockSpec((None, s, HEAD_DIM), lambda gg, i: (slot * g + gg, 0, 0))

    def win(slot, behind):
        return pl.BlockSpec((None, Q_BLOCK, HEAD_DIM),
                            lambda gg, i: (slot * g + gg, jnp.maximum(i - behind, 0), 0))

    in_specs = [pl.BlockSpec((r4, Q_BLOCK, HEAD_DIM), lambda gg, i: (gg, i, 0)),
                pl.BlockSpec((None, Q_BLOCK, nselp), lambda gg, i: (gg, i, 0)),
                pl.BlockSpec((None, s, 2 * HEAD_DIM), lambda gg, i: (gg, 0, 0)),
                whole(1),
                pl.BlockSpec((None, SEL_BAND_CHUNKS, rows, LANES), lambda gg, i: (gg, 0, 0, 0)),
                pl.BlockSpec((None, rows, WINDOW + Q_BLOCK), lambda gg, i: (gg, 0, 0)),
                pl.BlockSpec((Q_BLOCK, rows), lambda gg, i: (i, gg)),
                pl.BlockSpec((None, Q_BLOCK, LANES), lambda gg, i: (gg, i, 0))]
    in_specs += [win(2, b) for b in range(WIN_CHUNKS)] + [win(3, b) for b in range(WIN_CHUNKS)]
    per_group = 3 * r4
    group_gates = jnp.stack([jnp.pad(gate_logits[:, per_group * gg:per_group * (gg + 1)],
                                     ((0, 0), (0, LANES - per_group))) for gg in range(g)])
    return pl.pallas_call(
        _sel_win_kernel,
        grid=(g, s // Q_BLOCK),
        in_specs=in_specs,
        out_specs=pl.BlockSpec((Q_BLOCK, rows), lambda gg, i: (i, gg)),
        out_shape=jax.ShapeDtypeStruct((s, N_HEADS * HEAD_DIM), BF16),
        scratch_shapes=[pltpu.VMEM((nselp // LANES, rows, 2 * HEAD_DIM), BF16),
                        pltpu.VMEM((rows, KEY_TILE), F32), pltpu.VMEM((rows, KEY_TILE), F32),
                        pltpu.VMEM((rows, WINDOW + Q_BLOCK), F32), pltpu.VMEM((rows, HEAD_DIM), F32),
                        pltpu.VMEM((rows, LANES), F32), pltpu.VMEM((rows, LANES), F32),
                        pltpu.VMEM((rows, HEAD_DIM), F32)],
        compiler_params=_params(("parallel", "arbitrary")),
        name="nsa_selected_window",
    )(q, selbias, k_aug, kv, t_sel, t_win, o_cmp, group_gates, *([kv] * (2 * WIN_CHUNKS)))


def _selection_aggregator(ncp, nselp):
    n = np.arange(ncp)[:, None]
    j = np.arange(nselp)[None, :]
    hit = (n >= SEL_RATIO * j - 1) & (n <= SEL_RATIO * j + SEL_RATIO - 1) & (n < ncp - 1)
    return jnp.asarray(hit.astype(np.float32), dtype=BF16)


def _block_onehot(s):
    blk = (np.arange(s) // SEL_BLOCK) % LANES
    return jnp.asarray((blk[:, None] == np.arange(LANES)[None, :]).astype(np.float32), dtype=BF16)


def _round_up(x, m):
    return (x + m - 1) // m * m


def kernel(x, fox_w_in, fox_b_f, fox_w_o, nsa_w_in, nsa_w_o, kv_w, cmp_pos_k, cmp_pos_v, cmp_k_w1, cmp_k_w2, cmp_v_w1, cmp_v_w2, rel_bias, mlp_w1, mlp_w2, ln1_g, ln1_b, ln2_g, ln2_b):
    b, s, d = x.shape
    assert b == 1 and d == D_MODEL and s % (CMP_STRIDE * LANES) == 0
    hd = N_HEADS * HEAD_DIM
    scale = HEAD_DIM ** -0.5 * LOG2E
    tm = min(512, s)
    tp = min(1024, s)
    fox_t = min(512, s)
    pad_cols = lambda w: jnp.pad(w, ((0, 0), (0, LANES - w.shape[1])))

    fox_in, fox_out = fox_w_in.astype(BF16), fox_w_o.astype(BF16)
    nsa_in, nsa_out = nsa_w_in.astype(BF16), nsa_w_o.astype(BF16)
    w1_all, w2_all = mlp_w1.astype(BF16), mlp_w2.astype(BF16)
    project = functools.partial(_matmul, tm=tp, tn=512, out_dtype=BF16, head_major=True)
    gate_project = functools.partial(_matmul, tm=tp, tn=LANES, out_dtype=F32, head_major=False)

    h = x[0]
    hb = h.astype(BF16)
    kv_state = None
    for layer in range(DEPTH):
        g1, b1 = ln1_g[layer][None, :], ln1_b[layer][None, :]
        g2, b2 = ln2_g[layer][None, :], ln2_b[layer][None, :]
        if layer < N_A_LAYERS:
            qkv = project(hb, fox_in, layer=layer, n=3 * hd, scaled_cols=hd, scale=scale)
            gate_logits = gate_project(hb, pad_cols(fox_in[layer][:, 3 * hd:]))
            cum = _forget_cumsum(gate_logits, fox_b_f[layer])
            cum_rows = cum[:, :N_HEADS].T.reshape(N_HEADS, s // fox_t, 1, fox_t)
            attn = _fox_attention(qkv, cum_rows, _first_fox_tile(qkv, cum, t=fox_t), t=fox_t)
            h, hb = _oproj(attn, fox_out, layer, h, g1, b1, tm=tm)
        else:
            nsa_layer = layer - N_A_LAYERS
            k_cmp_v_cmp, kv_tok, k_aug, tables, agg, n_top = kv_state
            t_sel, t_win, t_cmp = tables
            q = project(hb, nsa_in, layer=nsa_layer, n=hd, scaled_cols=hd, scale=scale)
            gate_logits = gate_project(hb, pad_cols(nsa_in[nsa_layer][:, hd:]))
            o_cmp, selbias = _cmp_select(q, k_cmp_v_cmp, t_cmp, agg, n_top=n_top)
            mixed = _sel_win(q, selbias, k_aug, kv_tok, t_sel, t_win, o_cmp, gate_logits)
            h, hb = _oproj(mixed, nsa_out, nsa_layer, h, g1, b1, tm=tm)
        h, hb = _mlp(hb, w1_all, w2_all, layer, h, g2, b2, tm=tm, tf=1024)
        if layer == N_A_LAYERS - 1:
            gd = N_KV_GROUPS * HEAD_DIM
            kvw = kv_w.astype(BF16)
            kv_raw = project(hb, kvw, n=2 * gd, out_dtype=F32)
            kv_tok = project(hb, kvw, n=4 * gd, first_col=2 * gd)
            pos = jnp.stack([cmp_pos_k.reshape(1, -1), cmp_pos_v.reshape(1, -1)])
            w1 = jnp.stack([cmp_k_w1, cmp_v_w1]).astype(BF16)
            w2 = jnp.stack([cmp_k_w2, cmp_v_w2]).astype(BF16)
            kv_cmp = _compress(kv_raw, pos, w1, w2)
            n_sel = s // SEL_BLOCK
            nselp = _round_up(n_sel, LANES)
            onehot = jnp.broadcast_to(_block_onehot(s), (N_KV_GROUPS, s, LANES))
            k_aug = jnp.concatenate([kv_tok[:N_KV_GROUPS], onehot], axis=-1)
            tables = _bias_tables(_bias_vec(rel_bias))
            agg = _selection_aggregator(s // CMP_STRIDE, nselp)
            kv_state = (kv_cmp, kv_tok, k_aug, tables, agg, min(N_SELECTED, n_sel))
    return h[None]
```

```python
import functools
import math

import numpy as np
import jax
import jax.numpy as jnp
from jax import lax
from jax.experimental import pallas as pl
from jax.experimental.pallas import tpu as pltpu

D_MODEL = 2048
DEPTH = 4
HEAD_DIM = 128
N_HEADS = D_MODEL // HEAD_DIM
N_KV_GROUPS = 4
HEADS_PER_GROUP = N_HEADS // N_KV_GROUPS
D_FF = 4 * D_MODEL
N_A_LAYERS = DEPTH // 2
Q_BLOCK = 128
CMP_BLOCK = 32
CMP_STRIDE = 16
CMP_HIDDEN = 256
SEL_BLOCK = 64
N_SELECTED = 16
SEL_RATIO = SEL_BLOCK // CMP_STRIDE
WINDOW = 512
N_BUCKETS = 32
REL_MAX_DIST = 2048
ALPHA = (2.0 * DEPTH) ** 0.25
LN_EPS = 1e-5
N_FORCED = 3
PICKED = -2.0
NEG_BIG = -1e30
LOG2E = math.log2(math.e)

LANES = 128
MXU_WIDTH = 256
VMEM_LIMIT = 56 * 1024 * 1024
BF16 = jnp.bfloat16
F32 = jnp.float32

_EXACT = N_BUCKETS // 2
_BUCKET_THRESHOLDS = tuple(
    int(math.ceil(_EXACT * (REL_MAX_DIST / _EXACT) ** (k / (N_BUCKETS - _EXACT))))
    for k in range(1, N_BUCKETS - _EXACT))
BIAS_RANGE = 2048
KEY_TILE = 512
TILE_CHUNKS = KEY_TILE // LANES
SEL_BAND_BEHIND = BIAS_RANGE // LANES + TILE_CHUNKS - 2
SEL_BAND_CHUNKS = SEL_BAND_BEHIND + TILE_CHUNKS
WIN_CHUNKS = WINDOW // Q_BLOCK + 1
UNROLL = 4
FAR_UNROLL = 8
CMP_TABLES = 32
CMP_TILES = 4
SUB_ROWS = 128
FOX_SKIP_MARGIN = 160.0


def _nt_dot(a, b):
    return lax.dot_general(a, b, (((1,), (1,)), ((), ())), preferred_element_type=F32)


def _params(sem):
    return pltpu.CompilerParams(dimension_semantics=sem, vmem_limit_bytes=VMEM_LIMIT)


def _layer_norm(y, g, b):
    mu = jnp.mean(y, axis=-1, keepdims=True)
    yc = y - mu
    var = jnp.mean(yc * yc, axis=-1, keepdims=True)
    return yc * lax.rsqrt(var + LN_EPS) * g + b


def _matmul_kernel(x_ref, w_ref, o_ref, *, tn, scaled_cols, scale, head_major):
    acc = jnp.dot(x_ref[...], w_ref[...], preferred_element_type=F32)
    if scaled_cols:
        j = pl.program_id(1)
        acc = acc * jnp.where(j * tn < scaled_cols, scale, 1.0).astype(F32)
    if head_major:
        for c in range(tn // LANES):
            o_ref[c] = acc[:, c * LANES:(c + 1) * LANES].astype(o_ref.dtype)
    else:
        o_ref[...] = acc.astype(o_ref.dtype)


def _weight_spec(w, layer, block, index_map):
    if w.ndim == 2:
        return pl.BlockSpec(block, index_map)
    return pl.BlockSpec((None,) + block, lambda *idx: (layer,) + index_map(*idx))


def _matmul(x, w, *, n=None, first_col=0, layer=None, out_dtype, head_major, tm, tn,
            scaled_cols=0, scale=1.0):
    m, k = x.shape
    n = w.shape[-1] if n is None else n
    assert m % tm == 0 and n % tn == 0 and scaled_cols % tn == 0 and first_col % tn == 0
    col0 = first_col // tn
    w_spec = _weight_spec(w, layer, (k, tn), lambda i, j: (0, col0 + j))
    if head_major:
        out_shape = jax.ShapeDtypeStruct((n // LANES, m, LANES), out_dtype)
        out_spec = pl.BlockSpec((tn // LANES, tm, LANES), lambda i, j: (j, i, 0))
    else:
        out_shape = jax.ShapeDtypeStruct((m, n), out_dtype)
        out_spec = pl.BlockSpec((tm, tn), lambda i, j: (i, j))
    return pl.pallas_call(
        functools.partial(_matmul_kernel, tn=tn, scaled_cols=scaled_cols, scale=scale,
                          head_major=head_major),
        grid=(m // tm, n // tn),
        in_specs=[pl.BlockSpec((tm, k), lambda i, j: (i, 0)), w_spec],
        out_specs=out_spec,
        out_shape=out_shape,
        compiler_params=_params(("parallel", "arbitrary")),
        name="proj_matmul",
    )(x, w)


def _cum_kernel(gl_ref, b_ref, o_ref, *, nblk):
    r = lax.broadcasted_iota(jnp.int32, (LANES, LANES), 0)
    c = lax.broadcasted_iota(jnp.int32, (LANES, LANES), 1)
    tri = (c <= r).astype(F32)
    bias = b_ref[...]

    def body(i, carry):
        rows = pl.ds(pl.multiple_of(i * LANES, LANES), LANES)
        lf = jax.nn.log_sigmoid(gl_ref[rows, :] + bias)
        cum = jnp.dot(tri, lf, preferred_element_type=F32,
                      precision=lax.Precision.HIGHEST) + carry
        o_ref[rows, :] = cum
        return cum[LANES - 1:LANES, :]

    lax.fori_loop(0, nblk, body, jnp.zeros((1, LANES), F32))


def _forget_cumsum(gate_logits, b_f):
    s = gate_logits.shape[0]
    bias = jnp.zeros((1, LANES), F32).at[0, :N_HEADS].set(b_f)
    return pl.pallas_call(
        functools.partial(_cum_kernel, nblk=s // LANES),
        out_shape=jax.ShapeDtypeStruct((s, LANES), F32),
        compiler_params=pltpu.CompilerParams(vmem_limit_bytes=VMEM_LIMIT),
        name="forget_cumsum",
    )(gate_logits, bias)


def _online_update(s, v, m_ref, l_ref, acc_ref, rows):
    m_prev = m_ref[rows, :]
    m_next = jnp.maximum(m_prev, jnp.max(s, axis=1, keepdims=True))
    alpha = jnp.exp2(m_prev - m_next)
    reps = s.shape[1] // LANES
    p = jnp.exp2(s - jnp.concatenate([m_next] * reps, axis=1))
    part = p[:, :LANES]
    for c in range(1, reps):
        part = part + p[:, c * LANES:(c + 1) * LANES]
    l_ref[rows, :] = alpha * l_ref[rows, :] + part
    acc_ref[rows, :] = alpha * acc_ref[rows, :] + jnp.dot(p.astype(BF16), v,
                                                          preferred_element_type=F32)
    m_ref[rows, :] = m_next


def _reset(m_ref, l_ref, acc_ref):
    m_ref[...] = jnp.full_like(m_ref, NEG_BIG)
    l_ref[...] = jnp.zeros_like(l_ref)
    acc_ref[...] = jnp.zeros_like(acc_ref)


def _normalized(l_ref, acc_ref):
    return acc_ref[...] / jnp.sum(l_ref[...], axis=1, keepdims=True)


def _fox_kernel(first_ref, q_ref, k_ref, v_ref, c_ref, o_ref, sa_ref, sb_ref, m_ref, l_ref, acc_ref, *, t):
    i = pl.program_id(1)
    c0 = jnp.max(c_ref[i], axis=1, keepdims=True)
    _reset(m_ref, l_ref, acc_ref)

    def keys(j):
        return pl.ds(pl.multiple_of(j * t, t), t)

    def logits(j, buf):
        buf[...] = _nt_dot(q_ref[...], k_ref[keys(j), :]) + (c0 - c_ref[j]) * LOG2E

    def attend(j, buf, masked):
        v = v_ref[keys(j), :]
        for r in range(t // SUB_ROWS):
            rows = slice(r * SUB_ROWS, (r + 1) * SUB_ROWS)
            if masked:
                width = (r + 1) * SUB_ROWS
                row = r * SUB_ROWS + lax.broadcasted_iota(jnp.int32, (SUB_ROWS, width), 0)
                col = lax.broadcasted_iota(jnp.int32, (SUB_ROWS, width), 1)
                s = jnp.where(col <= row, buf[rows, :width], NEG_BIG)
                _online_update(s, v[:width, :], m_ref, l_ref, acc_ref, rows)
            else:
                _online_update(buf[rows, :], v, m_ref, l_ref, acc_ref, rows)

    def step(j, cur, nxt):
        logits(j + 1, nxt)
        attend(j, cur, False)

    first = first_ref[pl.program_id(0), i]
    n = i - first
    logits(first, sa_ref)

    def pair(p, carry):
        step(first + 2 * p, sa_ref, sb_ref)
        step(first + 2 * p + 1, sb_ref, sa_ref)
        return carry

    lax.fori_loop(0, n // 2, pair, 0)

    @pl.when(n % 2 == 0)
    def _():
        attend(i, sa_ref, True)

    @pl.when(n % 2 == 1)
    def _():
        step(i - 1, sa_ref, sb_ref)
        attend(i, sb_ref, True)

    o_ref[...] = _normalized(l_ref, acc_ref).astype(o_ref.dtype)


def _first_fox_tile(qkv, cum, *, t):
    h = N_HEADS
    s = qkv.shape[1]
    norm = lambda a: jnp.sqrt(jnp.max(jnp.sum(jnp.square(a.astype(F32)), axis=-1), axis=-1))
    bound = FOX_SKIP_MARGIN + 2.0 * norm(qkv[:h]) * norm(qkv[h:2 * h])
    ct = cum[:, :h].T
    at_query_start = ct[:, ::t]
    at_key_end = ct[:, t - 1::t]
    gap = (at_key_end[:, None, :] - at_query_start[:, :, None]) * LOG2E
    idx = np.arange(s // t)
    earlier = jnp.asarray(idx[None, :] < idx[:, None])
    skip = earlier[None] & (gap >= bound[:, None, None])
    return jnp.sum(skip, axis=-1).astype(jnp.int32)


def _fox_attention(qkv, cum_rows, first_tile, *, t):
    _, s, _ = qkv.shape
    h = N_HEADS
    grid_spec = pltpu.PrefetchScalarGridSpec(
        num_scalar_prefetch=1,
        grid=(h, s // t),
        in_specs=[pl.BlockSpec((None, t, HEAD_DIM), lambda hh, i, f: (hh, i, 0)),
                  pl.BlockSpec((None, s, HEAD_DIM), lambda hh, i, f: (h + hh, 0, 0)),
                  pl.BlockSpec((None, s, HEAD_DIM), lambda hh, i, f: (2 * h + hh, 0, 0)),
                  pl.BlockSpec((None, s // t, 1, t), lambda hh, i, f: (hh, 0, 0, 0))],
        out_specs=pl.BlockSpec((t, HEAD_DIM), lambda hh, i, f: (i, hh)),
        scratch_shapes=[pltpu.VMEM((t, t), F32), pltpu.VMEM((t, t), F32),
                        pltpu.VMEM((t, LANES), F32), pltpu.VMEM((t, LANES), F32),
                        pltpu.VMEM((t, HEAD_DIM), F32)])
    return pl.pallas_call(
        functools.partial(_fox_kernel, t=t),
        grid_spec=grid_spec,
        out_shape=jax.ShapeDtypeStruct((s, h * HEAD_DIM), BF16),
        compiler_params=_params(("parallel", "arbitrary")),
        name="fox_attention",
    )(first_tile, qkv, qkv, qkv, cum_rows)


def _emit_norm(y, g_ref, b_ref, of_ref, ob_ref):
    hn = _layer_norm(y, g_ref[...], b_ref[...])
    of_ref[...] = hn
    ob_ref[...] = hn.astype(BF16)


def _oproj_kernel(a_ref, w_ref, h_ref, g_ref, b_ref, of_ref, ob_ref):
    mix = jnp.dot(a_ref[...], w_ref[...], preferred_element_type=F32)
    _emit_norm(ALPHA * h_ref[...] + mix, g_ref, b_ref, of_ref, ob_ref)


def _row_spec(tm, n):
    return pl.BlockSpec((tm, n), lambda i: (i, 0))


def _const_spec(shape):
    return pl.BlockSpec(shape, lambda i: (0,) * len(shape))


def _norm_outs(s, tm):
    d = D_MODEL
    return dict(
        out_specs=[_row_spec(tm, d), _row_spec(tm, d)],
        out_shape=[jax.ShapeDtypeStruct((s, d), F32), jax.ShapeDtypeStruct((s, d), BF16)])


def _oproj(a, w, layer, h, g, b, *, tm):
    s, d = h.shape
    return pl.pallas_call(
        _oproj_kernel,
        grid=(s // tm,),
        in_specs=[_row_spec(tm, d), _weight_spec(w, layer, (d, d), lambda i: (0, 0)), _row_spec(tm, d),
                  _const_spec((1, d)), _const_spec((1, d))],
        compiler_params=_params(("parallel",)),
        name="oproj_norm",
        **_norm_outs(s, tm),
    )(a, w, h, g, b)


def _mlp_kernel(x_ref, w1_ref, w2_ref, h_ref, g_ref, b_ref, of_ref, ob_ref, acc_ref):
    f = pl.program_id(1)

    @pl.when(f == 0)
    def _():
        acc_ref[...] = jnp.zeros_like(acc_ref)

    x = x_ref[...]
    chunks = [slice(c * MXU_WIDTH, (c + 1) * MXU_WIDTH) for c in range(w1_ref.shape[1] // MXU_WIDTH)]
    hidden = [jnp.dot(x, w1_ref[:, cols], preferred_element_type=F32) for cols in chunks]
    for cols, u in zip(chunks, hidden):
        u = jnp.maximum(u, 0.0)
        acc_ref[...] += jnp.dot((u * u).astype(BF16), w2_ref[cols, :], preferred_element_type=F32)

    @pl.when(f == pl.num_programs(1) - 1)
    def _():
        _emit_norm(ALPHA * h_ref[...] + acc_ref[...], g_ref, b_ref, of_ref, ob_ref)


def _mlp(xb, w1, w2, layer, h, g, b, *, tm, tf):
    s, d = h.shape
    ff = w1.shape[-1]
    row = lambda n: pl.BlockSpec((tm, n), lambda i, f: (i, 0))
    vec = pl.BlockSpec((1, d), lambda i, f: (0, 0))
    return pl.pallas_call(
        _mlp_kernel,
        grid=(s // tm, ff // tf),
        in_specs=[row(d), _weight_spec(w1, layer, (d, tf), lambda i, f: (0, f)),
                  _weight_spec(w2, layer, (tf, d), lambda i, f: (f, 0)), row(d), vec, vec],
        out_specs=[row(d), row(d)],
        out_shape=[jax.ShapeDtypeStruct((s, d), F32), jax.ShapeDtypeStruct((s, d), BF16)],
        scratch_shapes=[pltpu.VMEM((tm, d), F32)],
        compiler_params=_params(("parallel", "arbitrary")),
        name="mlp_norm",
    )(xb, w1, w2, h, g, b)


def _compress_kernel(x_ref, pos_ref, w1_ref, w2_ref, o_ref):
    half = CMP_STRIDE * HEAD_DIM
    x = x_ref[...]
    first = jnp.dot((x + pos_ref[:, :half]).astype(BF16), w1_ref[:half, :],
                    preferred_element_type=F32)
    second = jnp.dot((x + pos_ref[:, half:]).astype(BF16), w1_ref[half:, :],
                     preferred_element_type=F32)
    n = x.shape[0]
    hidden = first + pltpu.roll(second, n - 1, 0)
    act = jax.nn.gelu(hidden)
    o_ref[...] = jnp.dot(act.astype(BF16), w2_ref[...], preferred_element_type=F32).astype(BF16)


def _compress(kv_raw, pos, w1, w2):
    g = N_KV_GROUPS
    s = kv_raw.shape[1]
    nc = s // CMP_STRIDE
    wide = CMP_STRIDE * HEAD_DIM
    x = kv_raw.reshape(2 * g, nc, wide)
    return pl.pallas_call(
        _compress_kernel,
        grid=(2, g),
        in_specs=[pl.BlockSpec((None, nc, wide), lambda a, gg: (a * g + gg, 0, 0)),
                  pl.BlockSpec((None, 1, 2 * wide), lambda a, gg: (a, 0, 0)),
                  pl.BlockSpec((None, 2 * wide, CMP_HIDDEN), lambda a, gg: (a, 0, 0)),
                  pl.BlockSpec((None, CMP_HIDDEN, HEAD_DIM), lambda a, gg: (a, 0, 0))],
        out_specs=pl.BlockSpec((None, None, nc, HEAD_DIM), lambda a, gg: (a, gg, 0, 0)),
        out_shape=jax.ShapeDtypeStruct((2, g, nc, HEAD_DIM), BF16),
        compiler_params=_params(("parallel", "parallel")),
        name="compress_kv",
    )(x, pos, w1, w2)


def _bias_vec_kernel(t_ref, o_ref):
    d = lax.broadcasted_iota(jnp.int32, (1, BIAS_RANGE), 1)
    large = jnp.full_like(d, _EXACT)
    for thr in _BUCKET_THRESHOLDS:
        large = large + (d >= thr).astype(jnp.int32)
    bucket = jnp.where(d < _EXACT, d, large)
    table = t_ref[...]
    acc = jnp.zeros((N_HEADS, BIAS_RANGE), F32)
    for bkt in range(N_BUCKETS):
        acc = jnp.where(bucket == bkt, table[:, bkt:bkt + 1], acc)
    o_ref[...] = (acc - table[:, N_BUCKETS - 1:N_BUCKETS]) * LOG2E


def _bias_vec(rel_bias):
    return pl.pallas_call(
        _bias_vec_kernel,
        out_shape=jax.ShapeDtypeStruct((N_HEADS, BIAS_RANGE), F32),
        name="rel_bias_by_distance",
    )(rel_bias.T)


def _toeplitz(w, rows, cols):
    h, width = w.shape
    flat = jnp.tile(w, (1, rows))[:, :rows * (width - 1)]
    return flat.reshape(h, rows, width - 1)[:, :, :cols]


def _by_distance(vec, d_hi, d_lo, window=None):
    h = vec.shape[0]
    limit = BIAS_RANGE if window is None else window
    above = jnp.full((h, d_hi - limit + 1), 0.0 if window is None else NEG_BIG, F32)
    below = jnp.full((h, -d_lo), NEG_BIG, F32)
    return jnp.concatenate([above, vec[:, limit - 1::-1], below], axis=1)


def _per_group(t):
    n = t.shape[2] // LANES
    t = t.reshape(N_KV_GROUPS, HEADS_PER_GROUP * Q_BLOCK, n, LANES)
    return t.transpose(0, 2, 1, 3)


def _bias_tables(vec):
    pad = jnp.zeros((N_HEADS, Q_BLOCK), F32)
    behind = SEL_BAND_BEHIND * LANES
    ahead = (SEL_BAND_CHUNKS - SEL_BAND_BEHIND) * LANES
    w = jnp.concatenate([_by_distance(vec, behind, 1 - ahead), pad], axis=1)
    t_sel = _per_group(_toeplitz(w, Q_BLOCK, SEL_BAND_CHUNKS * LANES))
    w = jnp.concatenate([_by_distance(vec, WINDOW, 1 - Q_BLOCK, window=WINDOW), pad + NEG_BIG], axis=1)
    t_win = _toeplitz(w, Q_BLOCK, WINDOW + Q_BLOCK)
    t_win = t_win.reshape(N_KV_GROUPS, HEADS_PER_GROUP * Q_BLOCK, WINDOW + Q_BLOCK)
    lo_rel = -(CMP_TABLES - 1) * (Q_BLOCK // CMP_STRIDE)
    n_rel = LANES - lo_rel
    end_min = CMP_STRIDE * lo_rel + CMP_BLOCK - 1
    w = jnp.concatenate([_by_distance(vec, -end_min, 1 - end_min - CMP_STRIDE * n_rel), pad], axis=1)
    band = _toeplitz(w, Q_BLOCK, CMP_STRIDE * n_rel)[:, :, ::CMP_STRIDE]
    per_tile = Q_BLOCK // CMP_STRIDE
    t_cmp = jnp.concatenate(
        [band[:, :, -lo_rel - per_tile * e:-lo_rel - per_tile * e + LANES] for e in range(CMP_TABLES)], axis=2)
    return t_sel, t_win, _per_group(t_cmp)


def _cmp_kernel(q_ref, kc_ref, vc_ref, t0_ref, t1_ref, a_ref, o_ref, sel_ref, lg_ref, p_ref, imp_ref,
                *, n_top):
    i = pl.program_id(1)
    ncp = kc_ref.shape[0]
    nselp = a_ref.shape[1]
    qb = q_ref.shape[1]
    c_hi = (i * (qb // CMP_STRIDE)) // LANES
    lg_ref[...] = _nt_dot(q_ref[...].reshape(HEADS_PER_GROUP * qb, HEAD_DIM), kc_ref[...])
    chunks = [slice(c * LANES, (c + 1) * LANES) for c in range(ncp // LANES)]
    heads = [slice(r * qb, (r + 1) * qb) for r in range(HEADS_PER_GROUP)]
    row_max = []
    for r, rows in enumerate(heads):
        own = slice(r * Q_BLOCK, (r + 1) * Q_BLOCK)
        b0 = jnp.concatenate([t0_ref[u, own, :] for u in range(qb // Q_BLOCK)], axis=0)
        b1 = jnp.concatenate([t1_ref[u, own, :] for u in range(qb // Q_BLOCK)], axis=0)
        top = None
        for c, cols in enumerate(chunks):
            bias = jnp.where(c == c_hi, b0, jnp.where(c == c_hi - 1, b1,
                                                      jnp.where(c > c_hi, NEG_BIG, 0.0)))
            x = lg_ref[rows, cols] + bias
            lg_ref[rows, cols] = x
            top = x if top is None else jnp.maximum(top, x)
        row_max.append(jnp.max(top, axis=1, keepdims=True))
    scale = []
    for rows, m in zip(heads, row_max):
        part = None
        for cols in chunks:
            e = jnp.exp2(lg_ref[rows, cols] - m)
            lg_ref[rows, cols] = e
            part = e if part is None else part + e
        total = jnp.maximum(jnp.sum(part, axis=1, keepdims=True), 1e-30)
        scale.append(jnp.where(m > 0.5 * NEG_BIG, 1.0 / total, 0.0))
    for cols in chunks:
        imp = None
        for rows, inv in zip(heads, scale):
            p = lg_ref[rows, cols] * inv
            p_ref[rows, cols] = p.astype(BF16)
            imp = p if imp is None else imp + p
        imp_ref[:, cols] = imp
    o = jnp.dot(p_ref[...], vc_ref[...], preferred_element_type=F32)
    for r in range(HEADS_PER_GROUP):
        o_ref[:, r * HEAD_DIM:(r + 1) * HEAD_DIM] = o[heads[r], :]
    imp = imp_ref[...]

    hi = imp.astype(BF16)
    rest = imp - hi.astype(F32)
    mid = rest.astype(BF16)
    low = (rest - mid.astype(F32)).astype(BF16)
    a = a_ref[...]
    imp_sel = (jnp.dot(hi, a, preferred_element_type=F32) + jnp.dot(mid, a, preferred_element_type=F32)
               + jnp.dot(low, a, preferred_element_type=F32))
    imp_sel = imp_sel.T
    t = i * qb + lax.broadcasted_iota(jnp.int32, (1, qb), 1)
    blk_t = t >> (SEL_BLOCK.bit_length() - 1)
    j = lax.broadcasted_iota(jnp.int32, (nselp, 1), 0)
    forced = (j == 0) | (j == blk_t) | (j == blk_t - 1)
    score = jnp.where(forced, PICKED, jnp.where(j <= blk_t, imp_sel, -1.0))
    jf = jnp.broadcast_to(j.astype(F32), (nselp, qb))

    def pick(_, score):
        top = jnp.max(score, axis=0, keepdims=True)
        first = jnp.min(jnp.where(score == top, jf, float(nselp)), axis=0, keepdims=True)
        return jnp.where(jf == first, PICKED, score)

    score = lax.fori_loop(0, n_top - N_FORCED, pick, score)
    sel_ref[...] = jnp.where(score == PICKED, 0.0, NEG_BIG).T.astype(BF16)


def _cmp_select(q, kv_cmp, t_cmp, agg, *, n_top):
    _, s, _ = q.shape
    g, r4 = N_KV_GROUPS, HEADS_PER_GROUP
    ncp = kv_cmp.shape[2]
    nselp = agg.shape[1]
    qb = CMP_TILES * Q_BLOCK
    rows = r4 * qb
    table_rows = r4 * Q_BLOCK
    steps_per_chunk = LANES * CMP_STRIDE // qb
    table = lambda shift: pl.BlockSpec(
        (None, CMP_TILES, table_rows, LANES),
        lambda gg, i: (gg, i % steps_per_chunk + shift * steps_per_chunk, 0, 0))
    return pl.pallas_call(
        functools.partial(_cmp_kernel, n_top=n_top),
        grid=(g, s // qb),
        in_specs=[pl.BlockSpec((r4, qb, HEAD_DIM), lambda gg, i: (gg, i, 0)),
                  pl.BlockSpec((None, None, ncp, HEAD_DIM), lambda gg, i: (0, gg, 0, 0)),
                  pl.BlockSpec((None, None, ncp, HEAD_DIM), lambda gg, i: (1, gg, 0, 0)),
                  table(0), table(1),
                  pl.BlockSpec((ncp, nselp), lambda gg, i: (0, 0))],
        out_specs=[pl.BlockSpec((qb, r4 * HEAD_DIM), lambda gg, i: (i, gg)),
                   pl.BlockSpec((None, qb, nselp), lambda gg, i: (gg, i, 0))],
        out_shape=[jax.ShapeDtypeStruct((s, N_HEADS * HEAD_DIM), F32),
                   jax.ShapeDtypeStruct((g, s, nselp), BF16)],
        scratch_shapes=[pltpu.VMEM((rows, ncp), F32), pltpu.VMEM((rows, ncp), BF16),
                        pltpu.VMEM((qb, ncp), F32)],
        compiler_params=_params(("parallel", "arbitrary")),
        name="nsa_compressed_select",
    )(q, kv_cmp, kv_cmp, t_cmp, t_cmp, agg)


def _sel_win_kernel(*refs):
    (q_ref, sb_ref, ka_ref, vs_ref, ts_ref, tw_ref, oc_ref, gl_ref) = refs[:8]
    kw_refs = refs[8:8 + WIN_CHUNKS]
    vw_refs = refs[8 + WIN_CHUNKS:8 + 2 * WIN_CHUNKS]
    (x_ref, qa_ref, sa_ref, sb2_ref, sw_ref, ow_ref, m_ref, l_ref, acc_ref) = refs[8 + 2 * WIN_CHUNKS:]
    i = pl.program_id(1)
    head_rows = [slice(r * Q_BLOCK, (r + 1) * Q_BLOCK) for r in range(HEADS_PER_GROUP)]
    tiles_per_chunk = LANES * SEL_BLOCK // KEY_TILE
    for c in range(qa_ref.shape[0]):
        sb = sb_ref[:, c * LANES:(c + 1) * LANES]
        for r, rows in enumerate(head_rows):
            qa_ref[c, rows, :HEAD_DIM] = q_ref[r]
            qa_ref[c, rows, HEAD_DIM:] = sb

    _reset(m_ref, l_ref, acc_ref)
    last = i // TILE_CHUNKS

    def keys(j):
        return pl.ds(pl.multiple_of(j * KEY_TILE, KEY_TILE), KEY_TILE)

    def logits(j, buf):
        buf[...] = _nt_dot(qa_ref[j // tiles_per_chunk], ka_ref[keys(j), :])

    def attend(j, buf, biased=True):
        v = vs_ref[keys(j), :]
        first = SEL_BAND_BEHIND - (i - TILE_CHUNKS * j)
        for rows in head_rows:
            s = buf[rows, :]
            if biased:
                s = s + jnp.concatenate(
                    [ts_ref[jnp.maximum(first + c, 0), rows, :] for c in range(TILE_CHUNKS)], axis=1)
            _online_update(s, v, m_ref, l_ref, acc_ref, rows)

    def step(j, cur, nxt, biased=True):
        logits(j + 1, nxt)
        attend(j, cur, biased)

    kw = jnp.concatenate([kw_refs[b][...] for b in reversed(range(WIN_CHUNKS))], axis=0)
    vw = jnp.concatenate([vw_refs[b][...] for b in reversed(range(WIN_CHUNKS))], axis=0)
    lane_chunk = lax.broadcasted_iota(jnp.int32, (1, kw.shape[0]), 1) >> (LANES.bit_length() - 1)
    before_start = jnp.where(WIN_CHUNKS - 1 - lane_chunk > i, NEG_BIG, 0.0)
    sw_ref[...] = _nt_dot(q_ref[...].reshape(HEADS_PER_GROUP * Q_BLOCK, HEAD_DIM), kw)
    logits(0, sa_ref)
    for r, rows in enumerate(head_rows):
        s = sw_ref[rows, :] + (tw_ref[rows, :] + before_start)
        p = jnp.exp2(s - jnp.max(s, axis=1, keepdims=True))
        o = jnp.dot(p.astype(BF16), vw, preferred_element_type=F32)
        ow_ref[rows, :] = o / jnp.sum(p, axis=1, keepdims=True)

    far_tiles = jnp.maximum(i - SEL_BAND_BEHIND - 1 + TILE_CHUNKS, 0) // TILE_CHUNKS

    def steps(first, count, biased=True):
        for n in range(0, count, 2):
            step(first + n, sa_ref, sb2_ref, biased)
            step(first + n + 1, sb2_ref, sa_ref, biased)

    def run(first, count, unroll, biased):
        def body(p, carry):
            steps(first + unroll * p, unroll, biased)
            return carry
        lax.fori_loop(0, count, body, 0)

    far_runs = far_tiles // FAR_UNROLL
    run(0, far_runs, FAR_UNROLL, False)
    start = FAR_UNROLL * far_runs
    left = last - start + 1
    runs = (left - 1) // UNROLL
    run(start, runs, UNROLL, True)
    start = start + UNROLL * runs
    left = left - UNROLL * runs

    @pl.when(left > 2)
    def _():
        steps(start, 2)

    @pl.when(left % 2 == 1)
    def _():
        attend(last, sa_ref)

    @pl.when(left % 2 == 0)
    def _():
        step(last - 1, sa_ref, sb2_ref)
        attend(last, sb2_ref)

    gates = jax.nn.sigmoid(gl_ref[...])
    o_sel = _normalized(l_ref, acc_ref)
    for r, rows in enumerate(head_rows):
        cols = slice(r * HEAD_DIM, (r + 1) * HEAD_DIM)
        mix = (gates[:, 3 * r:3 * r + 1] * oc_ref[:, cols] + gates[:, 3 * r + 1:3 * r + 2] * o_sel[rows, :]
               + gates[:, 3 * r + 2:3 * r + 3] * ow_ref[rows, :])
        x_ref[:, cols] = mix.astype(BF16)


def _sel_win(q, selbias, k_aug, kv, t_sel, t_win, o_cmp, gate_logits):
    _, s, _ = q.shape
    g, r4 = N_KV_GROUPS, HEADS_PER_GROUP
    rows = r4 * Q_BLOCK
    nselp = selbias.shape[2]
    whole = lambda slot: pl.BlockSpec((None, s, HEAD_DIM), lambda gg, i: (slot * g + gg, 0, 0))

    def win(slot, behind):
        return pl.BlockSpec((None, Q_BLOCK, HEAD_DIM),
                            lambda gg, i: (slot * g + gg, jnp.maximum(i - behind, 0), 0))

    in_specs = [pl.BlockSpec((r4, Q_BLOCK, HEAD_DIM), lambda gg, i: (gg, i, 0)),
                pl.BlockSpec((None, Q_BLOCK, nselp), lambda gg, i: (gg, i, 0)),
                pl.BlockSpec((None, s, 2 * HEAD_DIM), lambda gg, i: (gg, 0, 0)),
                whole(1),
                pl.BlockSpec((None, SEL_BAND_CHUNKS, rows, LANES), lambda gg, i: (gg, 0, 0, 0)),
                pl.BlockSpec((None, rows, WINDOW + Q_BLOCK), lambda gg, i: (gg, 0, 0)),
                pl.BlockSpec((Q_BLOCK, rows), lambda gg, i: (i, gg)),
                pl.BlockSpec((None, Q_BLOCK, LANES), lambda gg, i: (gg, i, 0))]
    in_specs += [win(2, b) for b in range(WIN_CHUNKS)] + [win(3, b) for b in range(WIN_CHUNKS)]
    per_group = 3 * r4
    group_gates = jnp.stack([jnp.pad(gate_logits[:, per_group * gg:per_group * (gg + 1)],
                                     ((0, 0), (0, LANES - per_group))) for gg in range(g)])
    return pl.pallas_call(
        _sel_win_kernel,
        grid=(g, s // Q_BLOCK),
        in_specs=in_specs,
        out_specs=pl.BlockSpec((Q_BLOCK, rows), lambda gg, i: (i, gg)),
        out_shape=jax.ShapeDtypeStruct((s, N_HEADS * HEAD_DIM), BF16),
        scratch_shapes=[pltpu.VMEM((nselp // LANES, rows, 2 * HEAD_DIM), BF16),
                        pltpu.VMEM((rows, KEY_TILE), F32), pltpu.VMEM((rows, KEY_TILE), F32),
                        pltpu.VMEM((rows, WINDOW + Q_BLOCK), F32), pltpu.VMEM((rows, HEAD_DIM), F32),
                        pltpu.VMEM((rows, LANES), F32), pltpu.VMEM((rows, LANES), F32),
                        pltpu.VMEM((rows, HEAD_DIM), F32)],
        compiler_params=_params(("parallel", "arbitrary")),
        name="nsa_selected_window",
    )(q, selbias, k_aug, kv, t_sel, t_win, o_cmp, group_gates, *([kv] * (2 * WIN_CHUNKS)))


def _selection_aggregator(ncp, nselp):
    n = np.arange(ncp)[:, None]
    j = np.arange(nselp)[None, :]
    hit = (n >= SEL_RATIO * j - 1) & (n <= SEL_RATIO * j + SEL_RATIO - 1) & (n < ncp - 1)
    return jnp.asarray(hit.astype(np.float32), dtype=BF16)


def _block_onehot(s):
    blk = (np.arange(s) // SEL_BLOCK) % LANES
    return jnp.asarray((blk[:, None] == np.arange(LANES)[None, :]).astype(np.float32), dtype=BF16)


def _round_up(x, m):
    return (x + m - 1) // m * m


def kernel(x, fox_w_in, fox_b_f, fox_w_o, nsa_w_in, nsa_w_o, kv_w, cmp_pos_k, cmp_pos_v, cmp_k_w1, cmp_k_w2, cmp_v_w1, cmp_v_w2, rel_bias, mlp_w1, mlp_w2, ln1_g, ln1_b, ln2_g, ln2_b):
    b, s, d = x.shape
    assert b == 1 and d == D_MODEL and s % (CMP_STRIDE * LANES) == 0
    hd = N_HEADS * HEAD_DIM
    scale = HEAD_DIM ** -0.5 * LOG2E
    tm = min(512, s)
    tp = min(1024, s)
    fox_t = min(512, s)
    pad_cols = lambda w: jnp.pad(w, ((0, 0), (0, LANES - w.shape[1])))

    fox_in, fox_out = fox_w_in.astype(BF16), fox_w_o.astype(BF16)
    nsa_in, nsa_out = nsa_w_in.astype(BF16), nsa_w_o.astype(BF16)
    w1_all, w2_all = mlp_w1.astype(BF16), mlp_w2.astype(BF16)
    project = functools.partial(_matmul, tm=tp, tn=512, out_dtype=BF16, head_major=True)
    gate_project = functools.partial(_matmul, tm=tp, tn=LANES, out_dtype=F32, head_major=False)

    h = x[0]
    hb = h.astype(BF16)
    kv_state = None
    for layer in range(DEPTH):
        g1, b1 = ln1_g[layer][None, :], ln1_b[layer][None, :]
        g2, b2 = ln2_g[layer][None, :], ln2_b[layer][None, :]
        if layer < N_A_LAYERS:
            qkv = project(hb, fox_in, layer=layer, n=3 * hd, scaled_cols=hd, scale=scale)
            gate_logits = gate_project(hb, pad_cols(fox_in[layer][:, 3 * hd:]))
            cum = _forget_cumsum(gate_logits, fox_b_f[layer])
            cum_rows = cum[:, :N_HEADS].T.reshape(N_HEADS, s // fox_t, 1, fox_t)
            attn = _fox_attention(qkv, cum_rows, _first_fox_tile(qkv, cum, t=fox_t), t=fox_t)
            h, hb = _oproj(attn, fox_out, layer, h, g1, b1, tm=tm)
        else:
            nsa_layer = layer - N_A_LAYERS
            k_cmp_v_cmp, kv_tok, k_aug, tables, agg, n_top = kv_state
            t_sel, t_win, t_cmp = tables
            q = project(hb, nsa_in, layer=nsa_layer, n=hd, scaled_cols=hd, scale=scale)
            gate_logits = gate_project(hb, pad_cols(nsa_in[nsa_layer][:, hd:]))
            o_cmp, selbias = _cmp_select(q, k_cmp_v_cmp, t_cmp, agg, n_top=n_top)
            mixed = _sel_win(q, selbias, k_aug, kv_tok, t_sel, t_win, o_cmp, gate_logits)
            h, hb = _oproj(mixed, nsa_out, nsa_layer, h, g1, b1, tm=tm)
        h, hb = _mlp(hb, w1_all, w2_all, layer, h, g2, b2, tm=tm, tf=1024)
        if layer == N_A_LAYERS - 1:
            gd = N_KV_GROUPS * HEAD_DIM
            kvw = kv_w.astype(BF16)
            kv_raw = project(hb, kvw, n=2 * gd, out_dtype=F32)
            kv_tok = project(hb, kvw, n=4 * gd, first_col=2 * gd)
            pos = jnp.stack([cmp_pos_k.reshape(1, -1), cmp_pos_v.reshape(1, -1)])
            w1 = jnp.stack([cmp_k_w1, cmp_v_w1]).astype(BF16)
            w2 = jnp.stack([cmp_k_w2, cmp_v_w2]).astype(BF16)
            kv_cmp = _compress(kv_raw, pos, w1, w2)
            n_sel = s // SEL_BLOCK
            nselp = _round_up(n_sel, LANES)
            onehot = jnp.broadcast_to(_block_onehot(s), (N_KV_GROUPS, s, LANES))
            k_aug = jnp.concatenate([kv_tok[:N_KV_GROUPS], onehot], axis=-1)
            tables = _bias_tables(_bias_vec(rel_bias))
            agg = _selection_aggregator(s // CMP_STRIDE, nselp)
            kv_state = (kv_cmp, kv_tok, k_aug, tables, agg, min(N_SELECTED, n_sel))
    return h[None]
```

```python
import functools
import math

import numpy as np
import jax
import jax.numpy as jnp
from jax import lax
from jax.experimental import pallas as pl
from jax.experimental.pallas import tpu as pltpu

D_MODEL = 2048
DEPTH = 4
HEAD_DIM = 128
N_HEADS = D_MODEL // HEAD_DIM
N_KV_GROUPS = 4
HEADS_PER_GROUP = N_HEADS // N_KV_GROUPS
D_FF = 4 * D_MODEL
N_A_LAYERS = DEPTH // 2
Q_BLOCK = 128
CMP_BLOCK = 32
CMP_STRIDE = 16
CMP_HIDDEN = 256
SEL_BLOCK = 64
N_SELECTED = 16
SEL_RATIO = SEL_BLOCK // CMP_STRIDE
WINDOW = 512
N_BUCKETS = 32
REL_MAX_DIST = 2048
ALPHA = (2.0 * DEPTH) ** 0.25
LN_EPS = 1e-5
N_FORCED = 3
PICKED = -2.0
NEG_BIG = -1e30
LOG2E = math.log2(math.e)

LANES = 128
MXU_WIDTH = 256
VMEM_LIMIT = 56 * 1024 * 1024
BF16 = jnp.bfloat16
F32 = jnp.float32

_EXACT = N_BUCKETS // 2
_BUCKET_THRESHOLDS = tuple(
    int(math.ceil(_EXACT * (REL_MAX_DIST / _EXACT) ** (k / (N_BUCKETS - _EXACT))))
    for k in range(1, N_BUCKETS - _EXACT))
BIAS_RANGE = 2048
KEY_TILE = 512
TILE_CHUNKS = KEY_TILE // LANES
SEL_BAND_BEHIND = BIAS_RANGE // LANES + TILE_CHUNKS - 2
SEL_BAND_CHUNKS = SEL_BAND_BEHIND + TILE_CHUNKS
WIN_CHUNKS = WINDOW // Q_BLOCK + 1
SEL_TILES = 2
WIN_BLOCKS = WIN_CHUNKS - 1 + SEL_TILES
UNROLL = 4
FAR_UNROLL = 8
CMP_TABLES = 32
CMP_TILES = 4
SUB_ROWS = 128
FOX_SKIP_MARGIN = 160.0


def _nt_dot(a, b):
    return lax.dot_general(a, b, (((1,), (1,)), ((), ())), preferred_element_type=F32)


def _params(sem):
    return pltpu.CompilerParams(dimension_semantics=sem, vmem_limit_bytes=VMEM_LIMIT)


def _layer_norm(y, g, b):
    mu = jnp.mean(y, axis=-1, keepdims=True)
    yc = y - mu
    var = jnp.mean(yc * yc, axis=-1, keepdims=True)
    return yc * lax.rsqrt(var + LN_EPS) * g + b


def _matmul_kernel(x_ref, w_ref, o_ref, *, tn, scaled_cols, scale, head_major):
    acc = jnp.dot(x_ref[...], w_ref[...], preferred_element_type=F32)
    if scaled_cols:
        j = pl.program_id(1)
        acc = acc * jnp.where(j * tn < scaled_cols, scale, 1.0).astype(F32)
    if head_major:
        for c in range(tn // LANES):
            o_ref[c] = acc[:, c * LANES:(c + 1) * LANES].astype(o_ref.dtype)
    else:
        o_ref[...] = acc.astype(o_ref.dtype)


def _weight_spec(w, layer, block, index_map):
    if w.ndim == 2:
        return pl.BlockSpec(block, index_map)
    return pl.BlockSpec((None,) + block, lambda *idx: (layer,) + index_map(*idx))


def _matmul(x, w, *, n=None, first_col=0, layer=None, out_dtype, head_major, tm, tn,
            scaled_cols=0, scale=1.0):
    m, k = x.shape
    n = w.shape[-1] if n is None else n
    assert m % tm == 0 and n % tn == 0 and scaled_cols % tn == 0 and first_col % tn == 0
    col0 = first_col // tn
    w_spec = _weight_spec(w, layer, (k, tn), lambda i, j: (0, col0 + j))
    if head_major:
        out_shape = jax.ShapeDtypeStruct((n // LANES, m, LANES), out_dtype)
        out_spec = pl.BlockSpec((tn // LANES, tm, LANES), lambda i, j: (j, i, 0))
    else:
        out_shape = jax.ShapeDtypeStruct((m, n), out_dtype)
        out_spec = pl.BlockSpec((tm, tn), lambda i, j: (i, j))
    return pl.pallas_call(
        functools.partial(_matmul_kernel, tn=tn, scaled_cols=scaled_cols, scale=scale,
                          head_major=head_major),
        grid=(m // tm, n // tn),
        in_specs=[pl.BlockSpec((tm, k), lambda i, j: (i, 0)), w_spec],
        out_specs=out_spec,
        out_shape=out_shape,
        compiler_params=_params(("parallel", "arbitrary")),
        name="proj_matmul",
    )(x, w)


def _cum_kernel(gl_ref, b_ref, o_ref, *, nblk):
    r = lax.broadcasted_iota(jnp.int32, (LANES, LANES), 0)
    c = lax.broadcasted_iota(jnp.int32, (LANES, LANES), 1)
    tri = (c <= r).astype(F32)
    bias = b_ref[...]

    def body(i, carry):
        rows = pl.ds(pl.multiple_of(i * LANES, LANES), LANES)
        lf = jax.nn.log_sigmoid(gl_ref[rows, :] + bias)
        cum = jnp.dot(tri, lf, preferred_element_type=F32,
                      precision=lax.Precision.HIGHEST) + carry
        o_ref[rows, :] = cum
        return cum[LANES - 1:LANES, :]

    lax.fori_loop(0, nblk, body, jnp.zeros((1, LANES), F32))


def _forget_cumsum(gate_logits, b_f):
    s = gate_logits.shape[0]
    bias = jnp.zeros((1, LANES), F32).at[0, :N_HEADS].set(b_f)
    return pl.pallas_call(
        functools.partial(_cum_kernel, nblk=s // LANES),
        out_shape=jax.ShapeDtypeStruct((s, LANES), F32),
        compiler_params=pltpu.CompilerParams(vmem_limit_bytes=VMEM_LIMIT),
        name="forget_cumsum",
    )(gate_logits, bias)


def _online_update(s, v, m_ref, l_ref, acc_ref, rows):
    m_prev = m_ref[rows, :]
    m_next = jnp.maximum(m_prev, jnp.max(s, axis=1, keepdims=True))
    alpha = jnp.exp2(m_prev - m_next)
    reps = s.shape[1] // LANES
    p = jnp.exp2(s - jnp.concatenate([m_next] * reps, axis=1))
    part = p[:, :LANES]
    for c in range(1, reps):
        part = part + p[:, c * LANES:(c + 1) * LANES]
    l_ref[rows, :] = alpha * l_ref[rows, :] + part
    acc_ref[rows, :] = alpha * acc_ref[rows, :] + jnp.dot(p.astype(BF16), v,
                                                          preferred_element_type=F32)
    m_ref[rows, :] = m_next


def _reset(m_ref, l_ref, acc_ref):
    m_ref[...] = jnp.full_like(m_ref, NEG_BIG)
    l_ref[...] = jnp.zeros_like(l_ref)
    acc_ref[...] = jnp.zeros_like(acc_ref)


def _normalized(l_ref, acc_ref):
    return acc_ref[...] / jnp.sum(l_ref[...], axis=1, keepdims=True)


def _fox_kernel(first_ref, q_ref, k_ref, v_ref, c_ref, o_ref, sa_ref, sb_ref, m_ref, l_ref, acc_ref, *, t):
    i = pl.program_id(1)
    c0 = jnp.max(c_ref[i], axis=1, keepdims=True)
    _reset(m_ref, l_ref, acc_ref)

    def keys(j):
        return pl.ds(pl.multiple_of(j * t, t), t)

    def logits(j, buf):
        buf[...] = _nt_dot(q_ref[...], k_ref[keys(j), :]) + (c0 - c_ref[j]) * LOG2E

    def attend(j, buf, masked):
        v = v_ref[keys(j), :]
        for r in range(t // SUB_ROWS):
            rows = slice(r * SUB_ROWS, (r + 1) * SUB_ROWS)
            if masked:
                width = (r + 1) * SUB_ROWS
                row = r * SUB_ROWS + lax.broadcasted_iota(jnp.int32, (SUB_ROWS, width), 0)
                col = lax.broadcasted_iota(jnp.int32, (SUB_ROWS, width), 1)
                s = jnp.where(col <= row, buf[rows, :width], NEG_BIG)
                _online_update(s, v[:width, :], m_ref, l_ref, acc_ref, rows)
            else:
                _online_update(buf[rows, :], v, m_ref, l_ref, acc_ref, rows)

    def step(j, cur, nxt):
        logits(j + 1, nxt)
        attend(j, cur, False)

    first = first_ref[pl.program_id(0), i]
    n = i - first
    logits(first, sa_ref)

    def pair(p, carry):
        step(first + 2 * p, sa_ref, sb_ref)
        step(first + 2 * p + 1, sb_ref, sa_ref)
        return carry

    lax.fori_loop(0, n // 2, pair, 0)

    @pl.when(n % 2 == 0)
    def _():
        attend(i, sa_ref, True)

    @pl.when(n % 2 == 1)
    def _():
        step(i - 1, sa_ref, sb_ref)
        attend(i, sb_ref, True)

    o_ref[...] = _normalized(l_ref, acc_ref).astype(o_ref.dtype)


def _first_fox_tile(qkv, cum, *, t):
    h = N_HEADS
    s = qkv.shape[1]
    norm = lambda a: jnp.sqrt(jnp.max(jnp.sum(jnp.square(a.astype(F32)), axis=-1), axis=-1))
    bound = FOX_SKIP_MARGIN + 2.0 * norm(qkv[:h]) * norm(qkv[h:2 * h])
    ct = cum[:, :h].T
    at_query_start = ct[:, ::t]
    at_key_end = ct[:, t - 1::t]
    gap = (at_key_end[:, None, :] - at_query_start[:, :, None]) * LOG2E
    idx = np.arange(s // t)
    earlier = jnp.asarray(idx[None, :] < idx[:, None])
    skip = earlier[None] & (gap >= bound[:, None, None])
    return jnp.sum(skip, axis=-1).astype(jnp.int32)


def _fox_attention(qkv, cum_rows, first_tile, *, t):
    _, s, _ = qkv.shape
    h = N_HEADS
    grid_spec = pltpu.PrefetchScalarGridSpec(
        num_scalar_prefetch=1,
        grid=(h, s // t),
        in_specs=[pl.BlockSpec((None, t, HEAD_DIM), lambda hh, i, f: (hh, i, 0)),
                  pl.BlockSpec((None, s, HEAD_DIM), lambda hh, i, f: (h + hh, 0, 0)),
                  pl.BlockSpec((None, s, HEAD_DIM), lambda hh, i, f: (2 * h + hh, 0, 0)),
                  pl.BlockSpec((None, s // t, 1, t), lambda hh, i, f: (hh, 0, 0, 0))],
        out_specs=pl.BlockSpec((t, HEAD_DIM), lambda hh, i, f: (i, hh)),
        scratch_shapes=[pltpu.VMEM((t, t), F32), pltpu.VMEM((t, t), F32),
                        pltpu.VMEM((t, LANES), F32), pltpu.VMEM((t, LANES), F32),
                        pltpu.VMEM((t, HEAD_DIM), F32)])
    return pl.pallas_call(
        functools.partial(_fox_kernel, t=t),
        grid_spec=grid_spec,
        out_shape=jax.ShapeDtypeStruct((s, h * HEAD_DIM), BF16),
        compiler_params=_params(("parallel", "arbitrary")),
        name="fox_attention",
    )(first_tile, qkv, qkv, qkv, cum_rows)


def _emit_norm(y, g_ref, b_ref, of_ref, ob_ref):
    hn = _layer_norm(y, g_ref[...], b_ref[...])
    of_ref[...] = hn
    ob_ref[...] = hn.astype(BF16)


def _oproj_kernel(a_ref, w_ref, h_ref, g_ref, b_ref, of_ref, ob_ref):
    mix = jnp.dot(a_ref[...], w_ref[...], preferred_element_type=F32)
    _emit_norm(ALPHA * h_ref[...] + mix, g_ref, b_ref, of_ref, ob_ref)


def _row_spec(tm, n):
    return pl.BlockSpec((tm, n), lambda i: (i, 0))


def _const_spec(shape):
    return pl.BlockSpec(shape, lambda i: (0,) * len(shape))


def _norm_outs(s, tm):
    d = D_MODEL
    return dict(
        out_specs=[_row_spec(tm, d), _row_spec(tm, d)],
        out_shape=[jax.ShapeDtypeStruct((s, d), F32), jax.ShapeDtypeStruct((s, d), BF16)])


def _oproj(a, w, layer, h, g, b, *, tm):
    s, d = h.shape
    return pl.pallas_call(
        _oproj_kernel,
        grid=(s // tm,),
        in_specs=[_row_spec(tm, d), _weight_spec(w, layer, (d, d), lambda i: (0, 0)), _row_spec(tm, d),
                  _const_spec((1, d)), _const_spec((1, d))],
        compiler_params=_params(("parallel",)),
        name="oproj_norm",
        **_norm_outs(s, tm),
    )(a, w, h, g, b)


def _mlp_kernel(x_ref, w1_ref, w2_ref, h_ref, g_ref, b_ref, of_ref, ob_ref, acc_ref):
    f = pl.program_id(1)

    @pl.when(f == 0)
    def _():
        acc_ref[...] = jnp.zeros_like(acc_ref)

    x = x_ref[...]
    chunks = [slice(c * MXU_WIDTH, (c + 1) * MXU_WIDTH) for c in range(w1_ref.shape[1] // MXU_WIDTH)]
    hidden = [jnp.dot(x, w1_ref[:, cols], preferred_element_type=F32) for cols in chunks]
    for cols, u in zip(chunks, hidden):
        u = jnp.maximum(u, 0.0)
        acc_ref[...] += jnp.dot((u * u).astype(BF16), w2_ref[cols, :], preferred_element_type=F32)

    @pl.when(f == pl.num_programs(1) - 1)
    def _():
        _emit_norm(ALPHA * h_ref[...] + acc_ref[...], g_ref, b_ref, of_ref, ob_ref)


def _mlp(xb, w1, w2, layer, h, g, b, *, tm, tf):
    s, d = h.shape
    ff = w1.shape[-1]
    row = lambda n: pl.BlockSpec((tm, n), lambda i, f: (i, 0))
    vec = pl.BlockSpec((1, d), lambda i, f: (0, 0))
    return pl.pallas_call(
        _mlp_kernel,
        grid=(s // tm, ff // tf),
        in_specs=[row(d), _weight_spec(w1, layer, (d, tf), lambda i, f: (0, f)),
                  _weight_spec(w2, layer, (tf, d), lambda i, f: (f, 0)), row(d), vec, vec],
        out_specs=[row(d), row(d)],
        out_shape=[jax.ShapeDtypeStruct((s, d), F32), jax.ShapeDtypeStruct((s, d), BF16)],
        scratch_shapes=[pltpu.VMEM((tm, d), F32)],
        compiler_params=_params(("parallel", "arbitrary")),
        name="mlp_norm",
    )(xb, w1, w2, h, g, b)


def _compress_kernel(x_ref, pos_ref, w1_ref, w2_ref, o_ref):
    half = CMP_STRIDE * HEAD_DIM
    x = x_ref[...]
    first = jnp.dot((x + pos_ref[:, :half]).astype(BF16), w1_ref[:half, :],
                    preferred_element_type=F32)
    second = jnp.dot((x + pos_ref[:, half:]).astype(BF16), w1_ref[half:, :],
                     preferred_element_type=F32)
    n = x.shape[0]
    hidden = first + pltpu.roll(second, n - 1, 0)
    act = jax.nn.gelu(hidden)
    o_ref[...] = jnp.dot(act.astype(BF16), w2_ref[...], preferred_element_type=F32).astype(BF16)


def _compress(kv_raw, pos, w1, w2):
    g = N_KV_GROUPS
    s = kv_raw.shape[1]
    nc = s // CMP_STRIDE
    wide = CMP_STRIDE * HEAD_DIM
    x = kv_raw.reshape(2 * g, nc, wide)
    return pl.pallas_call(
        _compress_kernel,
        grid=(2, g),
        in_specs=[pl.BlockSpec((None, nc, wide), lambda a, gg: (a * g + gg, 0, 0)),
                  pl.BlockSpec((None, 1, 2 * wide), lambda a, gg: (a, 0, 0)),
                  pl.BlockSpec((None, 2 * wide, CMP_HIDDEN), lambda a, gg: (a, 0, 0)),
                  pl.BlockSpec((None, CMP_HIDDEN, HEAD_DIM), lambda a, gg: (a, 0, 0))],
        out_specs=pl.BlockSpec((None, None, nc, HEAD_DIM), lambda a, gg: (a, gg, 0, 0)),
        out_shape=jax.ShapeDtypeStruct((2, g, nc, HEAD_DIM), BF16),
        compiler_params=_params(("parallel", "parallel")),
        name="compress_kv",
    )(x, pos, w1, w2)


def _bias_vec_kernel(t_ref, o_ref):
    d = lax.broadcasted_iota(jnp.int32, (1, BIAS_RANGE), 1)
    large = jnp.full_like(d, _EXACT)
    for thr in _BUCKET_THRESHOLDS:
        large = large + (d >= thr).astype(jnp.int32)
    bucket = jnp.where(d < _EXACT, d, large)
    table = t_ref[...]
    acc = jnp.zeros((N_HEADS, BIAS_RANGE), F32)
    for bkt in range(N_BUCKETS):
        acc = jnp.where(bucket == bkt, table[:, bkt:bkt + 1], acc)
    o_ref[...] = (acc - table[:, N_BUCKETS - 1:N_BUCKETS]) * LOG2E


def _bias_vec(rel_bias):
    return pl.pallas_call(
        _bias_vec_kernel,
        out_shape=jax.ShapeDtypeStruct((N_HEADS, BIAS_RANGE), F32),
        name="rel_bias_by_distance",
    )(rel_bias.T)


def _toeplitz(w, rows, cols):
    h, width = w.shape
    flat = jnp.tile(w, (1, rows))[:, :rows * (width - 1)]
    return flat.reshape(h, rows, width - 1)[:, :, :cols]


def _by_distance(vec, d_hi, d_lo, window=None):
    h = vec.shape[0]
    limit = BIAS_RANGE if window is None else window
    above = jnp.full((h, d_hi - limit + 1), 0.0 if window is None else NEG_BIG, F32)
    below = jnp.full((h, -d_lo), NEG_BIG, F32)
    return jnp.concatenate([above, vec[:, limit - 1::-1], below], axis=1)


def _per_group(t):
    n = t.shape[2] // LANES
    t = t.reshape(N_KV_GROUPS, HEADS_PER_GROUP * Q_BLOCK, n, LANES)
    return t.transpose(0, 2, 1, 3)


def _bias_tables(vec):
    pad = jnp.zeros((N_HEADS, Q_BLOCK), F32)
    behind = SEL_BAND_BEHIND * LANES
    ahead = (SEL_BAND_CHUNKS - SEL_BAND_BEHIND) * LANES
    w = jnp.concatenate([_by_distance(vec, behind, 1 - ahead), pad], axis=1)
    t_sel = _per_group(_toeplitz(w, Q_BLOCK, SEL_BAND_CHUNKS * LANES))
    w = jnp.concatenate([_by_distance(vec, WINDOW, 1 - Q_BLOCK, window=WINDOW), pad + NEG_BIG], axis=1)
    t_win = _toeplitz(w, Q_BLOCK, WINDOW + Q_BLOCK)
    t_win = t_win.reshape(N_KV_GROUPS, HEADS_PER_GROUP * Q_BLOCK, WINDOW + Q_BLOCK)
    lo_rel = -(CMP_TABLES - 1) * (Q_BLOCK // CMP_STRIDE)
    n_rel = LANES - lo_rel
    end_min = CMP_STRIDE * lo_rel + CMP_BLOCK - 1
    w = jnp.concatenate([_by_distance(vec, -end_min, 1 - end_min - CMP_STRIDE * n_rel), pad], axis=1)
    band = _toeplitz(w, Q_BLOCK, CMP_STRIDE * n_rel)[:, :, ::CMP_STRIDE]
    per_tile = Q_BLOCK // CMP_STRIDE
    t_cmp = jnp.concatenate(
        [band[:, :, -lo_rel - per_tile * e:-lo_rel - per_tile * e + LANES] for e in range(CMP_TABLES)], axis=2)
    return t_sel, t_win, _per_group(t_cmp)


def _cmp_kernel(q_ref, kc_ref, vc_ref, t0_ref, t1_ref, a_ref, o_ref, sel_ref, lg_ref, p_ref, imp_ref,
                *, n_top):
    i = pl.program_id(1)
    ncp = kc_ref.shape[0]
    nselp = a_ref.shape[1]
    qb = q_ref.shape[1]
    c_hi = (i * (qb // CMP_STRIDE)) // LANES
    lg_ref[...] = _nt_dot(q_ref[...].reshape(HEADS_PER_GROUP * qb, HEAD_DIM), kc_ref[...])
    chunks = [slice(c * LANES, (c + 1) * LANES) for c in range(ncp // LANES)]
    heads = [slice(r * qb, (r + 1) * qb) for r in range(HEADS_PER_GROUP)]
    row_max = []
    for r, rows in enumerate(heads):
        own = slice(r * Q_BLOCK, (r + 1) * Q_BLOCK)
        b0 = jnp.concatenate([t0_ref[u, own, :] for u in range(qb // Q_BLOCK)], axis=0)
        b1 = jnp.concatenate([t1_ref[u, own, :] for u in range(qb // Q_BLOCK)], axis=0)
        top = None
        for c, cols in enumerate(chunks):
            bias = jnp.where(c == c_hi, b0, jnp.where(c == c_hi - 1, b1,
                                                      jnp.where(c > c_hi, NEG_BIG, 0.0)))
            x = lg_ref[rows, cols] + bias
            lg_ref[rows, cols] = x
            top = x if top is None else jnp.maximum(top, x)
        row_max.append(jnp.max(top, axis=1, keepdims=True))
    scale = []
    for rows, m in zip(heads, row_max):
        part = None
        for cols in chunks:
            e = jnp.exp2(lg_ref[rows, cols] - m)
            lg_ref[rows, cols] = e
            part = e if part is None else part + e
        total = jnp.maximum(jnp.sum(part, axis=1, keepdims=True), 1e-30)
        scale.append(jnp.where(m > 0.5 * NEG_BIG, 1.0 / total, 0.0))
    for cols in chunks:
        imp = None
        for rows, inv in zip(heads, scale):
            p = lg_ref[rows, cols] * inv
            p_ref[rows, cols] = p.astype(BF16)
            imp = p if imp is None else imp + p
        imp_ref[:, cols] = imp
    o = jnp.dot(p_ref[...], vc_ref[...], preferred_element_type=F32)
    for r in range(HEADS_PER_GROUP):
        o_ref[:, r * HEAD_DIM:(r + 1) * HEAD_DIM] = o[heads[r], :]
    imp = imp_ref[...]

    hi = imp.astype(BF16)
    rest = imp - hi.astype(F32)
    mid = rest.astype(BF16)
    low = (rest - mid.astype(F32)).astype(BF16)
    a = a_ref[...]
    imp_sel = (jnp.dot(hi, a, preferred_element_type=F32) + jnp.dot(mid, a, preferred_element_type=F32)
               + jnp.dot(low, a, preferred_element_type=F32))
    imp_sel = imp_sel.T
    t = i * qb + lax.broadcasted_iota(jnp.int32, (1, qb), 1)
    blk_t = t >> (SEL_BLOCK.bit_length() - 1)
    j = lax.broadcasted_iota(jnp.int32, (nselp, 1), 0)
    forced = (j == 0) | (j == blk_t) | (j == blk_t - 1)
    score = jnp.where(forced, PICKED, jnp.where(j <= blk_t, imp_sel, -1.0))
    jf = jnp.broadcast_to(j.astype(F32), (nselp, qb))

    def pick(_, score):
        top = jnp.max(score, axis=0, keepdims=True)
        first = jnp.min(jnp.where(score == top, jf, float(nselp)), axis=0, keepdims=True)
        return jnp.where(jf == first, PICKED, score)

    score = lax.fori_loop(0, n_top - N_FORCED, pick, score)
    sel_ref[...] = jnp.where(score == PICKED, 0.0, NEG_BIG).T.astype(BF16)


def _cmp_select(q, kv_cmp, t_cmp, agg, *, n_top):
    _, s, _ = q.shape
    g, r4 = N_KV_GROUPS, HEADS_PER_GROUP
    ncp = kv_cmp.shape[2]
    nselp = agg.shape[1]
    qb = CMP_TILES * Q_BLOCK
    rows = r4 * qb
    table_rows = r4 * Q_BLOCK
    steps_per_chunk = LANES * CMP_STRIDE // qb
    table = lambda shift: pl.BlockSpec(
        (None, CMP_TILES, table_rows, LANES),
        lambda gg, i: (gg, i % steps_per_chunk + shift * steps_per_chunk, 0, 0))
    return pl.pallas_call(
        functools.partial(_cmp_kernel, n_top=n_top),
        grid=(g, s // qb),
        in_specs=[pl.BlockSpec((r4, qb, HEAD_DIM), lambda gg, i: (gg, i, 0)),
                  pl.BlockSpec((None, None, ncp, HEAD_DIM), lambda gg, i: (0, gg, 0, 0)),
                  pl.BlockSpec((None, None, ncp, HEAD_DIM), lambda gg, i: (1, gg, 0, 0)),
                  table(0), table(1),
                  pl.BlockSpec((ncp, nselp), lambda gg, i: (0, 0))],
        out_specs=[pl.BlockSpec((qb, r4 * HEAD_DIM), lambda gg, i: (i, gg)),
                   pl.BlockSpec((None, qb, nselp), lambda gg, i: (gg, i, 0))],
        out_shape=[jax.ShapeDtypeStruct((s, N_HEADS * HEAD_DIM), F32),
                   jax.ShapeDtypeStruct((g, s, nselp), BF16)],
        scratch_shapes=[pltpu.VMEM((rows, ncp), F32), pltpu.VMEM((rows, ncp), BF16),
                        pltpu.VMEM((qb, ncp), F32)],
        compiler_params=_params(("parallel", "arbitrary")),
        name="nsa_compressed_select",
    )(q, kv_cmp, kv_cmp, t_cmp, t_cmp, agg)


def _sel_win_kernel(*refs):
    (q_ref, sb_ref, ka_ref, vs_ref, ts_ref, tw_ref, oc_ref, gl_ref) = refs[:8]
    kw_refs = refs[8:8 + WIN_BLOCKS]
    vw_refs = refs[8 + WIN_BLOCKS:8 + 2 * WIN_BLOCKS]
    (x_ref, qa_ref, sa_ref, sb2_ref, sw_ref, ow_ref, m_ref, l_ref, acc_ref) = refs[8 + 2 * WIN_BLOCKS:]
    i = pl.program_id(1)
    qb = SEL_TILES * Q_BLOCK
    tile0 = i * SEL_TILES
    head_rows = [slice(r * qb, (r + 1) * qb) for r in range(HEADS_PER_GROUP)]
    chains = [(u, slice(r * qb + u * Q_BLOCK, r * qb + (u + 1) * Q_BLOCK),
               slice(r * Q_BLOCK, (r + 1) * Q_BLOCK))
              for r in range(HEADS_PER_GROUP) for u in range(SEL_TILES)]
    tiles_per_chunk = LANES * SEL_BLOCK // KEY_TILE
    for c in range(qa_ref.shape[0]):
        sb = sb_ref[:, c * LANES:(c + 1) * LANES]
        for r, rows in enumerate(head_rows):
            qa_ref[c, rows, :HEAD_DIM] = q_ref[r]
            qa_ref[c, rows, HEAD_DIM:] = sb

    _reset(m_ref, l_ref, acc_ref)
    last = (tile0 + SEL_TILES - 1) // TILE_CHUNKS

    def keys(j):
        return pl.ds(pl.multiple_of(j * KEY_TILE, KEY_TILE), KEY_TILE)

    def logits(j, buf):
        buf[...] = _nt_dot(qa_ref[j // tiles_per_chunk], ka_ref[keys(j), :])

    def attend(j, buf, biased=True):
        v = vs_ref[keys(j), :]
        first = SEL_BAND_BEHIND - (tile0 - TILE_CHUNKS * j)
        for u, rows, own in chains:
            s = buf[rows, :]
            if biased:
                s = s + jnp.concatenate(
                    [ts_ref[jnp.maximum(first - u + c, 0), own, :] for c in range(TILE_CHUNKS)], axis=1)
            _online_update(s, v, m_ref, l_ref, acc_ref, rows)

    def step(j, cur, nxt, biased=True):
        logits(j + 1, nxt)
        attend(j, cur, biased)

    kw = jnp.concatenate([kw_refs[b][...] for b in reversed(range(WIN_BLOCKS))], axis=0)
    vw = jnp.concatenate([vw_refs[b][...] for b in reversed(range(WIN_BLOCKS))], axis=0)
    span = WINDOW + Q_BLOCK
    lane_chunk = lax.broadcasted_iota(jnp.int32, (1, span), 1) >> (LANES.bit_length() - 1)
    sw_ref[...] = _nt_dot(q_ref[...].reshape(HEADS_PER_GROUP * qb, HEAD_DIM), kw)
    logits(0, sa_ref)
    for u, rows, own in chains:
        before_start = jnp.where(WIN_CHUNKS - 1 - lane_chunk > tile0 + u, NEG_BIG, 0.0)
        s = sw_ref[rows, u * Q_BLOCK:u * Q_BLOCK + span] + (tw_ref[own, :] + before_start)
        p = jnp.exp2(s - jnp.max(s, axis=1, keepdims=True))
        o = jnp.dot(p.astype(BF16), vw[u * Q_BLOCK:u * Q_BLOCK + span, :], preferred_element_type=F32)
        ow_ref[rows, :] = o / jnp.sum(p, axis=1, keepdims=True)

    far_tiles = jnp.maximum(tile0 - SEL_BAND_BEHIND - 1 + TILE_CHUNKS, 0) // TILE_CHUNKS

    def steps(first, count, biased=True):
        for n in range(0, count, 2):
            step(first + n, sa_ref, sb2_ref, biased)
            step(first + n + 1, sb2_ref, sa_ref, biased)

    def run(first, count, unroll, biased):
        def body(p, carry):
            steps(first + unroll * p, unroll, biased)
            return carry
        lax.fori_loop(0, count, body, 0)

    far_runs = far_tiles // FAR_UNROLL
    run(0, far_runs, FAR_UNROLL, False)
    start = FAR_UNROLL * far_runs
    left = last - start + 1
    runs = (left - 1) // UNROLL
    run(start, runs, UNROLL, True)
    start = start + UNROLL * runs
    left = left - UNROLL * runs

    @pl.when(left > 2)
    def _():
        steps(start, 2)

    @pl.when(left % 2 == 1)
    def _():
        attend(last, sa_ref)

    @pl.when(left % 2 == 0)
    def _():
        step(last - 1, sa_ref, sb2_ref)
        attend(last, sb2_ref)

    gates = jax.nn.sigmoid(gl_ref[...])
    o_sel = _normalized(l_ref, acc_ref)
    for r, rows in enumerate(head_rows):
        cols = slice(r * HEAD_DIM, (r + 1) * HEAD_DIM)
        mix = (gates[:, 3 * r:3 * r + 1] * oc_ref[:, cols] + gates[:, 3 * r + 1:3 * r + 2] * o_sel[rows, :]
               + gates[:, 3 * r + 2:3 * r + 3] * ow_ref[rows, :])
        x_ref[:, cols] = mix.astype(BF16)


def _sel_win(q, selbias, k_aug, kv, t_sel, t_win, o_cmp, gate_logits):
    _, s, _ = q.shape
    g, r4 = N_KV_GROUPS, HEADS_PER_GROUP
    qb = SEL_TILES * Q_BLOCK
    rows = r4 * qb
    table_rows = r4 * Q_BLOCK
    width = r4 * HEAD_DIM
    nselp = selbias.shape[2]
    once = pl.Buffered(1)
    whole = lambda slot: pl.BlockSpec((None, s, HEAD_DIM), lambda gg, i: (slot * g + gg, 0, 0),
                                      pipeline_mode=once)

    def win(slot, behind):
        return pl.BlockSpec((None, Q_BLOCK, HEAD_DIM),
                            lambda gg, i: (slot * g + gg,
                                           jnp.maximum(i * SEL_TILES + SEL_TILES - 1 - behind, 0), 0))

    in_specs = [pl.BlockSpec((r4, qb, HEAD_DIM), lambda gg, i: (gg, i, 0)),
                pl.BlockSpec((None, qb, nselp), lambda gg, i: (gg, i, 0)),
                pl.BlockSpec((None, s, 2 * HEAD_DIM), lambda gg, i: (gg, 0, 0), pipeline_mode=once),
                whole(1),
                pl.BlockSpec((None, SEL_BAND_CHUNKS, table_rows, LANES), lambda gg, i: (gg, 0, 0, 0),
                             pipeline_mode=once),
                pl.BlockSpec((None, table_rows, WINDOW + Q_BLOCK), lambda gg, i: (gg, 0, 0),
                             pipeline_mode=once),
                pl.BlockSpec((qb, width), lambda gg, i: (i, gg)),
                pl.BlockSpec((None, qb, LANES), lambda gg, i: (gg, i, 0))]
    in_specs += [win(2, b) for b in range(WIN_BLOCKS)] + [win(3, b) for b in range(WIN_BLOCKS)]
    per_group = 3 * r4
    group_gates = jnp.stack([jnp.pad(gate_logits[:, per_group * gg:per_group * (gg + 1)],
                                     ((0, 0), (0, LANES - per_group))) for gg in range(g)])
    return pl.pallas_call(
        _sel_win_kernel,
        grid=(g, s // qb),
        in_specs=in_specs,
        out_specs=pl.BlockSpec((qb, width), lambda gg, i: (i, gg)),
        out_shape=jax.ShapeDtypeStruct((s, N_HEADS * HEAD_DIM), BF16),
        scratch_shapes=[pltpu.VMEM((nselp // LANES, rows, 2 * HEAD_DIM), BF16),
                        pltpu.VMEM((rows, KEY_TILE), F32), pltpu.VMEM((rows, KEY_TILE), F32),
                        pltpu.VMEM((rows, WINDOW + qb), F32), pltpu.VMEM((rows, HEAD_DIM), F32),
                        pltpu.VMEM((rows, LANES), F32), pltpu.VMEM((rows, LANES), F32),
                        pltpu.VMEM((rows, HEAD_DIM), F32)],
        compiler_params=_params(("parallel", "arbitrary")),
        name="nsa_selected_window",
    )(q, selbias, k_aug, kv, t_sel, t_win, o_cmp, group_gates, *([kv] * (2 * WIN_BLOCKS)))


def _selection_aggregator(ncp, nselp):
    n = np.arange(ncp)[:, None]
    j = np.arange(nselp)[None, :]
    hit = (n >= SEL_RATIO * j - 1) & (n <= SEL_RATIO * j + SEL_RATIO - 1) & (n < ncp - 1)
    return jnp.asarray(hit.astype(np.float32), dtype=BF16)


def _block_onehot(s):
    blk = (np.arange(s) // SEL_BLOCK) % LANES
    return jnp.asarray((blk[:, None] == np.arange(LANES)[None, :]).astype(np.float32), dtype=BF16)


def _round_up(x, m):
    return (x + m - 1) // m * m


def kernel(x, fox_w_in, fox_b_f, fox_w_o, nsa_w_in, nsa_w_o, kv_w, cmp_pos_k, cmp_pos_v, cmp_k_w1, cmp_k_w2, cmp_v_w1, cmp_v_w2, rel_bias, mlp_w1, mlp_w2, ln1_g, ln1_b, ln2_g, ln2_b):
    b, s, d = x.shape
    assert b == 1 and d == D_MODEL and s % (CMP_STRIDE * LANES) == 0
    hd = N_HEADS * HEAD_DIM
    scale = HEAD_DIM ** -0.5 * LOG2E
    tm = min(512, s)
    tp = min(1024, s)
    fox_t = min(512, s)
    pad_cols = lambda w: jnp.pad(w, ((0, 0), (0, LANES - w.shape[1])))

    fox_in, fox_out = fox_w_in.astype(BF16), fox_w_o.astype(BF16)
    nsa_in, nsa_out = nsa_w_in.astype(BF16), nsa_w_o.astype(BF16)
    w1_all, w2_all = mlp_w1.astype(BF16), mlp_w2.astype(BF16)
    project = functools.partial(_matmul, tm=tp, tn=512, out_dtype=BF16, head_major=True)
    gate_project = functools.partial(_matmul, tm=tp, tn=LANES, out_dtype=F32, head_major=False)

    h = x[0]
    hb = h.astype(BF16)
    kv_state = None
    for layer in range(DEPTH):
        g1, b1 = ln1_g[layer][None, :], ln1_b[layer][None, :]
        g2, b2 = ln2_g[layer][None, :], ln2_b[layer][None, :]
        if layer < N_A_LAYERS:
            qkv = project(hb, fox_in, layer=layer, n=3 * hd, scaled_cols=hd, scale=scale)
            gate_logits = gate_project(hb, pad_cols(fox_in[layer][:, 3 * hd:]))
            cum = _forget_cumsum(gate_logits, fox_b_f[layer])
            cum_rows = cum[:, :N_HEADS].T.reshape(N_HEADS, s // fox_t, 1, fox_t)
            attn = _fox_attention(qkv, cum_rows, _first_fox_tile(qkv, cum, t=fox_t), t=fox_t)
            h, hb = _oproj(attn, fox_out, layer, h, g1, b1, tm=tm)
        else:
            nsa_layer = layer - N_A_LAYERS
            k_cmp_v_cmp, kv_tok, k_aug, tables, agg, n_top = kv_state
            t_sel, t_win, t_cmp = tables
            q = project(hb, nsa_in, layer=nsa_layer, n=hd, scaled_cols=hd, scale=scale)
            gate_logits = gate_project(hb, pad_cols(nsa_in[nsa_layer][:, hd:]))
            o_cmp, selbias = _cmp_select(q, k_cmp_v_cmp, t_cmp, agg, n_top=n_top)
            mixed = _sel_win(q, selbias, k_aug, kv_tok, t_sel, t_win, o_cmp, gate_logits)
            h, hb = _oproj(mixed, nsa_out, nsa_layer, h, g1, b1, tm=tm)
        h, hb = _mlp(hb, w1_all, w2_all, layer, h, g2, b2, tm=tm, tf=1024)
        if layer == N_A_LAYERS - 1:
            gd = N_KV_GROUPS * HEAD_DIM
            kvw = kv_w.astype(BF16)
            kv_raw = project(hb, kvw, n=2 * gd, out_dtype=F32)
            kv_tok = project(hb, kvw, n=4 * gd, first_col=2 * gd)
            pos = jnp.stack([cmp_pos_k.reshape(1, -1), cmp_pos_v.reshape(1, -1)])
            w1 = jnp.stack([cmp_k_w1, cmp_v_w1]).astype(BF16)
            w2 = jnp.stack([cmp_k_w2, cmp_v_w2]).astype(BF16)
            kv_cmp = _compress(kv_raw, pos, w1, w2)
            n_sel = s // SEL_BLOCK
            nselp = _round_up(n_sel, LANES)
            onehot = jnp.broadcast_to(_block_onehot(s), (N_KV_GROUPS, s, LANES))
            k_aug = jnp.concatenate([kv_tok[:N_KV_GROUPS], onehot], axis=-1)
            tables = _bias_tables(_bias_vec(rel_bias))
            agg = _selection_aggregator(s // CMP_STRIDE, nselp)
            kv_state = (kv_cmp, kv_tok, k_aug, tables, agg, min(N_SELECTED, n_sel))
    return h[None]
```

```python
import functools
import math

import numpy as np
import jax
import jax.numpy as jnp
from jax import lax
from jax.experimental import pallas as pl
from jax.experimental.pallas import tpu as pltpu

D_MODEL = 2048
DEPTH = 4
HEAD_DIM = 128
N_HEADS = D_MODEL // HEAD_DIM
N_KV_GROUPS = 4
HEADS_PER_GROUP = N_HEADS // N_KV_GROUPS
D_FF = 4 * D_MODEL
N_A_LAYERS = DEPTH // 2
Q_BLOCK = 128
CMP_BLOCK = 32
CMP_STRIDE = 16
CMP_HIDDEN = 256
SEL_BLOCK = 64
N_SELECTED = 16
SEL_RATIO = SEL_BLOCK // CMP_STRIDE
WINDOW = 512
N_BUCKETS = 32
REL_MAX_DIST = 2048
ALPHA = (2.0 * DEPTH) ** 0.25
LN_EPS = 1e-5
N_FORCED = 3
PICKED = -2.0
NEG_BIG = -1e30
LOG2E = math.log2(math.e)

LANES = 128
MXU_WIDTH = 256
VMEM_LIMIT = 56 * 1024 * 1024
BF16 = jnp.bfloat16
F32 = jnp.float32

_EXACT = N_BUCKETS // 2
_BUCKET_THRESHOLDS = tuple(
    int(math.ceil(_EXACT * (REL_MAX_DIST / _EXACT) ** (k / (N_BUCKETS - _EXACT))))
    for k in range(1, N_BUCKETS - _EXACT))
BIAS_RANGE = 2048
KEY_TILE = 512
TILE_CHUNKS = KEY_TILE // LANES
SEL_BAND_BEHIND = BIAS_RANGE // LANES + TILE_CHUNKS - 2
SEL_BAND_CHUNKS = SEL_BAND_BEHIND + TILE_CHUNKS
WIN_CHUNKS = WINDOW // Q_BLOCK + 1
SEL_TILES = 2
WIN_BLOCKS = WIN_CHUNKS - 1 + SEL_TILES
UNROLL = 4
FAR_UNROLL = 8
CMP_TABLES = 32
CMP_TILES = 4
SUB_ROWS = 128
FOX_TILES = 2
FOX_SKIP_MARGIN = 160.0


def _nt_dot(a, b):
    return lax.dot_general(a, b, (((1,), (1,)), ((), ())), preferred_element_type=F32)


def _params(sem):
    return pltpu.CompilerParams(dimension_semantics=sem, vmem_limit_bytes=VMEM_LIMIT)


def _layer_norm(y, g, b):
    mu = jnp.mean(y, axis=-1, keepdims=True)
    yc = y - mu
    var = jnp.mean(yc * yc, axis=-1, keepdims=True)
    return yc * lax.rsqrt(var + LN_EPS) * g + b


def _matmul_kernel(x_ref, w_ref, o_ref, *, tn, scaled_cols, scale, head_major):
    acc = jnp.dot(x_ref[...], w_ref[...], preferred_element_type=F32)
    if scaled_cols:
        j = pl.program_id(1)
        acc = acc * jnp.where(j * tn < scaled_cols, scale, 1.0).astype(F32)
    if head_major:
        for c in range(tn // LANES):
            o_ref[c] = acc[:, c * LANES:(c + 1) * LANES].astype(o_ref.dtype)
    else:
        o_ref[...] = acc.astype(o_ref.dtype)


def _weight_spec(w, layer, block, index_map):
    if w.ndim == 2:
        return pl.BlockSpec(block, index_map)
    return pl.BlockSpec((None,) + block, lambda *idx: (layer,) + index_map(*idx))


def _matmul(x, w, *, n=None, first_col=0, layer=None, out_dtype, head_major, tm, tn,
            scaled_cols=0, scale=1.0):
    m, k = x.shape
    n = w.shape[-1] if n is None else n
    assert m % tm == 0 and n % tn == 0 and scaled_cols % tn == 0 and first_col % tn == 0
    col0 = first_col // tn
    w_spec = _weight_spec(w, layer, (k, tn), lambda i, j: (0, col0 + j))
    if head_major:
        out_shape = jax.ShapeDtypeStruct((n // LANES, m, LANES), out_dtype)
        out_spec = pl.BlockSpec((tn // LANES, tm, LANES), lambda i, j: (j, i, 0))
    else:
        out_shape = jax.ShapeDtypeStruct((m, n), out_dtype)
        out_spec = pl.BlockSpec((tm, tn), lambda i, j: (i, j))
    return pl.pallas_call(
        functools.partial(_matmul_kernel, tn=tn, scaled_cols=scaled_cols, scale=scale,
                          head_major=head_major),
        grid=(m // tm, n // tn),
        in_specs=[pl.BlockSpec((tm, k), lambda i, j: (i, 0)), w_spec],
        out_specs=out_spec,
        out_shape=out_shape,
        compiler_params=_params(("parallel", "arbitrary")),
        name="proj_matmul",
    )(x, w)


def _cum_kernel(gl_ref, b_ref, o_ref, *, nblk):
    r = lax.broadcasted_iota(jnp.int32, (LANES, LANES), 0)
    c = lax.broadcasted_iota(jnp.int32, (LANES, LANES), 1)
    tri = (c <= r).astype(F32)
    bias = b_ref[...]

    def body(i, carry):
        rows = pl.ds(pl.multiple_of(i * LANES, LANES), LANES)
        lf = jax.nn.log_sigmoid(gl_ref[rows, :] + bias)
        cum = jnp.dot(tri, lf, preferred_element_type=F32,
                      precision=lax.Precision.HIGHEST) + carry
        o_ref[rows, :] = cum
        return cum[LANES - 1:LANES, :]

    lax.fori_loop(0, nblk, body, jnp.zeros((1, LANES), F32))


def _forget_cumsum(gate_logits, b_f):
    s = gate_logits.shape[0]
    bias = jnp.zeros((1, LANES), F32).at[0, :N_HEADS].set(b_f)
    return pl.pallas_call(
        functools.partial(_cum_kernel, nblk=s // LANES),
        out_shape=jax.ShapeDtypeStruct((s, LANES), F32),
        compiler_params=pltpu.CompilerParams(vmem_limit_bytes=VMEM_LIMIT),
        name="forget_cumsum",
    )(gate_logits, bias)


def _online_update(s, v, m_ref, l_ref, acc_ref, rows):
    m_prev = m_ref[rows, :]
    m_next = jnp.maximum(m_prev, jnp.max(s, axis=1, keepdims=True))
    alpha = jnp.exp2(m_prev - m_next)
    reps = s.shape[1] // LANES
    p = jnp.exp2(s - jnp.concatenate([m_next] * reps, axis=1))
    part = p[:, :LANES]
    for c in range(1, reps):
        part = part + p[:, c * LANES:(c + 1) * LANES]
    l_ref[rows, :] = alpha * l_ref[rows, :] + part
    acc_ref[rows, :] = alpha * acc_ref[rows, :] + jnp.dot(p.astype(BF16), v,
                                                          preferred_element_type=F32)
    m_ref[rows, :] = m_next


def _reset(m_ref, l_ref, acc_ref):
    m_ref[...] = jnp.full_like(m_ref, NEG_BIG)
    l_ref[...] = jnp.zeros_like(l_ref)
    acc_ref[...] = jnp.zeros_like(acc_ref)


def _normalized(l_ref, acc_ref):
    return acc_ref[...] / jnp.sum(l_ref[...], axis=1, keepdims=True)


def _fox_kernel(first_ref, q_ref, k_ref, v_ref, c_ref, o_ref, sa_ref, sb_ref, m_ref, l_ref, acc_ref, *, t):
    i = pl.program_id(1)
    diag0 = i * FOX_TILES
    c0 = jnp.max(c_ref[diag0], axis=1, keepdims=True)
    _reset(m_ref, l_ref, acc_ref)
    per_tile = t // SUB_ROWS

    def keys(j):
        return pl.ds(pl.multiple_of(j * t, t), t)

    def logits(j, buf):
        buf[...] = _nt_dot(q_ref[...], k_ref[keys(j), :]) + (c0 - c_ref[j]) * LOG2E

    def attend(j, buf, diagonal=None):
        v = v_ref[keys(j), :]
        for u in range(FOX_TILES):
            if diagonal is not None and u < diagonal:
                continue
            for r in range(per_tile):
                rows = slice(u * t + r * SUB_ROWS, u * t + (r + 1) * SUB_ROWS)
                if diagonal == u:
                    width = (r + 1) * SUB_ROWS
                    row = r * SUB_ROWS + lax.broadcasted_iota(jnp.int32, (SUB_ROWS, width), 0)
                    col = lax.broadcasted_iota(jnp.int32, (SUB_ROWS, width), 1)
                    s = jnp.where(col <= row, buf[rows, :width], NEG_BIG)
                    _online_update(s, v[:width, :], m_ref, l_ref, acc_ref, rows)
                else:
                    _online_update(buf[rows, :], v, m_ref, l_ref, acc_ref, rows)

    def step(j, cur, nxt, diagonal=None):
        logits(j + 1, nxt)
        attend(j, cur, diagonal)

    first = first_ref[pl.program_id(0), i]
    n = diag0 - first
    logits(first, sa_ref)

    def pair(p, carry):
        step(first + 2 * p, sa_ref, sb_ref)
        step(first + 2 * p + 1, sb_ref, sa_ref)
        return carry

    lax.fori_loop(0, n // 2, pair, 0)

    def finish(cur, nxt):
        for d in range(FOX_TILES):
            if d + 1 < FOX_TILES:
                step(diag0 + d, cur, nxt, diagonal=d)
            else:
                attend(diag0 + d, cur, diagonal=d)
            cur, nxt = nxt, cur

    @pl.when(n % 2 == 0)
    def _():
        finish(sa_ref, sb_ref)

    @pl.when(n % 2 == 1)
    def _():
        step(diag0 - 1, sa_ref, sb_ref)
        finish(sb_ref, sa_ref)

    o_ref[...] = _normalized(l_ref, acc_ref).astype(o_ref.dtype)


def _first_fox_tile(qkv, cum, *, t):
    h = N_HEADS
    s = qkv.shape[1]
    norm = lambda a: jnp.sqrt(jnp.max(jnp.sum(jnp.square(a.astype(F32)), axis=-1), axis=-1))
    bound = FOX_SKIP_MARGIN + 2.0 * norm(qkv[:h]) * norm(qkv[h:2 * h])
    ct = cum[:, :h].T
    at_query_start = ct[:, ::t]
    at_key_end = ct[:, t - 1::t]
    gap = (at_key_end[:, None, :] - at_query_start[:, :, None]) * LOG2E
    idx = np.arange(s // t)
    earlier = jnp.asarray(idx[None, :] < idx[:, None])
    skip = earlier[None] & (gap >= bound[:, None, None])
    return jnp.sum(skip, axis=-1).astype(jnp.int32)


def _fox_attention(qkv, cum_rows, first_tile, *, t):
    _, s, _ = qkv.shape
    h = N_HEADS
    rows = FOX_TILES * t
    first_tile = jnp.min(first_tile.reshape(h, s // rows, FOX_TILES), axis=-1)
    grid_spec = pltpu.PrefetchScalarGridSpec(
        num_scalar_prefetch=1,
        grid=(h, s // rows),
        in_specs=[pl.BlockSpec((None, rows, HEAD_DIM), lambda hh, i, f: (hh, i, 0)),
                  pl.BlockSpec((None, s, HEAD_DIM), lambda hh, i, f: (h + hh, 0, 0)),
                  pl.BlockSpec((None, s, HEAD_DIM), lambda hh, i, f: (2 * h + hh, 0, 0)),
                  pl.BlockSpec((None, s // t, 1, t), lambda hh, i, f: (hh, 0, 0, 0))],
        out_specs=pl.BlockSpec((rows, HEAD_DIM), lambda hh, i, f: (i, hh)),
        scratch_shapes=[pltpu.VMEM((rows, t), F32), pltpu.VMEM((rows, t), F32),
                        pltpu.VMEM((rows, LANES), F32), pltpu.VMEM((rows, LANES), F32),
                        pltpu.VMEM((rows, HEAD_DIM), F32)])
    return pl.pallas_call(
        functools.partial(_fox_kernel, t=t),
        grid_spec=grid_spec,
        out_shape=jax.ShapeDtypeStruct((s, h * HEAD_DIM), BF16),
        compiler_params=_params(("parallel", "arbitrary")),
        name="fox_attention",
    )(first_tile, qkv, qkv, qkv, cum_rows)


def _emit_norm(y, g_ref, b_ref, of_ref, ob_ref):
    hn = _layer_norm(y, g_ref[...], b_ref[...])
    of_ref[...] = hn
    ob_ref[...] = hn.astype(BF16)


def _oproj_kernel(a_ref, w_ref, h_ref, g_ref, b_ref, of_ref, ob_ref):
    mix = jnp.dot(a_ref[...], w_ref[...], preferred_element_type=F32)
    _emit_norm(ALPHA * h_ref[...] + mix, g_ref, b_ref, of_ref, ob_ref)


def _row_spec(tm, n):
    return pl.BlockSpec((tm, n), lambda i: (i, 0))


def _const_spec(shape):
    return pl.BlockSpec(shape, lambda i: (0,) * len(shape))


def _norm_outs(s, tm):
    d = D_MODEL
    return dict(
        out_specs=[_row_spec(tm, d), _row_spec(tm, d)],
        out_shape=[jax.ShapeDtypeStruct((s, d), F32), jax.ShapeDtypeStruct((s, d), BF16)])


def _oproj(a, w, layer, h, g, b, *, tm):
    s, d = h.shape
    return pl.pallas_call(
        _oproj_kernel,
        grid=(s // tm,),
        in_specs=[_row_spec(tm, d), _weight_spec(w, layer, (d, d), lambda i: (0, 0)), _row_spec(tm, d),
                  _const_spec((1, d)), _const_spec((1, d))],
        compiler_params=_params(("parallel",)),
        name="oproj_norm",
        **_norm_outs(s, tm),
    )(a, w, h, g, b)


def _mlp_kernel(x_ref, w1_ref, w2_ref, h_ref, g_ref, b_ref, of_ref, ob_ref, acc_ref):
    f = pl.program_id(1)

    @pl.when(f == 0)
    def _():
        acc_ref[...] = jnp.zeros_like(acc_ref)

    x = x_ref[...]
    chunks = [slice(c * MXU_WIDTH, (c + 1) * MXU_WIDTH) for c in range(w1_ref.shape[1] // MXU_WIDTH)]
    hidden = [jnp.dot(x, w1_ref[:, cols], preferred_element_type=F32) for cols in chunks]
    for cols, u in zip(chunks, hidden):
        u = jnp.maximum(u, 0.0)
        acc_ref[...] += jnp.dot((u * u).astype(BF16), w2_ref[cols, :], preferred_element_type=F32)

    @pl.when(f == pl.num_programs(1) - 1)
    def _():
        _emit_norm(ALPHA * h_ref[...] + acc_ref[...], g_ref, b_ref, of_ref, ob_ref)


def _mlp(xb, w1, w2, layer, h, g, b, *, tm, tf):
    s, d = h.shape
    ff = w1.shape[-1]
    row = lambda n: pl.BlockSpec((tm, n), lambda i, f: (i, 0))
    vec = pl.BlockSpec((1, d), lambda i, f: (0, 0))
    return pl.pallas_call(
        _mlp_kernel,
        grid=(s // tm, ff // tf),
        in_specs=[row(d), _weight_spec(w1, layer, (d, tf), lambda i, f: (0, f)),
                  _weight_spec(w2, layer, (tf, d), lambda i, f: (f, 0)), row(d), vec, vec],
        out_specs=[row(d), row(d)],
        out_shape=[jax.ShapeDtypeStruct((s, d), F32), jax.ShapeDtypeStruct((s, d), BF16)],
        scratch_shapes=[pltpu.VMEM((tm, d), F32)],
        compiler_params=_params(("parallel", "arbitrary")),
        name="mlp_norm",
    )(xb, w1, w2, h, g, b)


def _compress_kernel(x_ref, pos_ref, w1_ref, w2_ref, o_ref):
    half = CMP_STRIDE * HEAD_DIM
    x = x_ref[...]
    first = jnp.dot((x + pos_ref[:, :half]).astype(BF16), w1_ref[:half, :],
                    preferred_element_type=F32)
    second = jnp.dot((x + pos_ref[:, half:]).astype(BF16), w1_ref[half:, :],
                     preferred_element_type=F32)
    n = x.shape[0]
    hidden = first + pltpu.roll(second, n - 1, 0)
    act = jax.nn.gelu(hidden)
    o_ref[...] = jnp.dot(act.astype(BF16), w2_ref[...], preferred_element_type=F32).astype(BF16)


def _compress(kv_raw, pos, w1, w2):
    g = N_KV_GROUPS
    s = kv_raw.shape[1]
    nc = s // CMP_STRIDE
    wide = CMP_STRIDE * HEAD_DIM
    x = kv_raw.reshape(2 * g, nc, wide)
    return pl.pallas_call(
        _compress_kernel,
        grid=(2, g),
        in_specs=[pl.BlockSpec((None, nc, wide), lambda a, gg: (a * g + gg, 0, 0)),
                  pl.BlockSpec((None, 1, 2 * wide), lambda a, gg: (a, 0, 0)),
                  pl.BlockSpec((None, 2 * wide, CMP_HIDDEN), lambda a, gg: (a, 0, 0)),
                  pl.BlockSpec((None, CMP_HIDDEN, HEAD_DIM), lambda a, gg: (a, 0, 0))],
        out_specs=pl.BlockSpec((None, None, nc, HEAD_DIM), lambda a, gg: (a, gg, 0, 0)),
        out_shape=jax.ShapeDtypeStruct((2, g, nc, HEAD_DIM), BF16),
        compiler_params=_params(("parallel", "parallel")),
        name="compress_kv",
    )(x, pos, w1, w2)


def _bias_vec_kernel(t_ref, o_ref):
    d = lax.broadcasted_iota(jnp.int32, (1, BIAS_RANGE), 1)
    large = jnp.full_like(d, _EXACT)
    for thr in _BUCKET_THRESHOLDS:
        large = large + (d >= thr).astype(jnp.int32)
    bucket = jnp.where(d < _EXACT, d, large)
    table = t_ref[...]
    acc = jnp.zeros((N_HEADS, BIAS_RANGE), F32)
    for bkt in range(N_BUCKETS):
        acc = jnp.where(bucket == bkt, table[:, bkt:bkt + 1], acc)
    o_ref[...] = (acc - table[:, N_BUCKETS - 1:N_BUCKETS]) * LOG2E


def _bias_vec(rel_bias):
    return pl.pallas_call(
        _bias_vec_kernel,
        out_shape=jax.ShapeDtypeStruct((N_HEADS, BIAS_RANGE), F32),
        name="rel_bias_by_distance",
    )(rel_bias.T)


def _toeplitz(w, rows, cols):
    h, width = w.shape
    flat = jnp.tile(w, (1, rows))[:, :rows * (width - 1)]
    return flat.reshape(h, rows, width - 1)[:, :, :cols]


def _by_distance(vec, d_hi, d_lo, window=None):
    h = vec.shape[0]
    limit = BIAS_RANGE if window is None else window
    above = jnp.full((h, d_hi - limit + 1), 0.0 if window is None else NEG_BIG, F32)
    below = jnp.full((h, -d_lo), NEG_BIG, F32)
    return jnp.concatenate([above, vec[:, limit - 1::-1], below], axis=1)


def _per_group(t):
    n = t.shape[2] // LANES
    t = t.reshape(N_KV_GROUPS, HEADS_PER_GROUP * Q_BLOCK, n, LANES)
    return t.transpose(0, 2, 1, 3)


def _bias_tables(vec):
    pad = jnp.zeros((N_HEADS, Q_BLOCK), F32)
    behind = SEL_BAND_BEHIND * LANES
    ahead = (SEL_BAND_CHUNKS - SEL_BAND_BEHIND) * LANES
    w = jnp.concatenate([_by_distance(vec, behind, 1 - ahead), pad], axis=1)
    t_sel = _per_group(_toeplitz(w, Q_BLOCK, SEL_BAND_CHUNKS * LANES))
    w = jnp.concatenate([_by_distance(vec, WINDOW, 1 - Q_BLOCK, window=WINDOW), pad + NEG_BIG], axis=1)
    t_win = _toeplitz(w, Q_BLOCK, WINDOW + Q_BLOCK)
    t_win = t_win.reshape(N_KV_GROUPS, HEADS_PER_GROUP * Q_BLOCK, WINDOW + Q_BLOCK)
    lo_rel = -(CMP_TABLES - 1) * (Q_BLOCK // CMP_STRIDE)
    n_rel = LANES - lo_rel
    end_min = CMP_STRIDE * lo_rel + CMP_BLOCK - 1
    w = jnp.concatenate([_by_distance(vec, -end_min, 1 - end_min - CMP_STRIDE * n_rel), pad], axis=1)
    band = _toeplitz(w, Q_BLOCK, CMP_STRIDE * n_rel)[:, :, ::CMP_STRIDE]
    per_tile = Q_BLOCK // CMP_STRIDE
    t_cmp = jnp.concatenate(
        [band[:, :, -lo_rel - per_tile * e:-lo_rel - per_tile * e + LANES] for e in range(CMP_TABLES)], axis=2)
    return t_sel, t_win, _per_group(t_cmp)


def _cmp_kernel(q_ref, kc_ref, vc_ref, t0_ref, t1_ref, a_ref, o_ref, sel_ref, lg_ref, p_ref, imp_ref,
                *, n_top):
    i = pl.program_id(1)
    ncp = kc_ref.shape[0]
    nselp = a_ref.shape[1]
    qb = q_ref.shape[1]
    c_hi = (i * (qb // CMP_STRIDE)) // LANES
    lg_ref[...] = _nt_dot(q_ref[...].reshape(HEADS_PER_GROUP * qb, HEAD_DIM), kc_ref[...])
    chunks = [slice(c * LANES, (c + 1) * LANES) for c in range(ncp // LANES)]
    heads = [slice(r * qb, (r + 1) * qb) for r in range(HEADS_PER_GROUP)]
    row_max = []
    for r, rows in enumerate(heads):
        own = slice(r * Q_BLOCK, (r + 1) * Q_BLOCK)
        b0 = jnp.concatenate([t0_ref[u, own, :] for u in range(qb // Q_BLOCK)], axis=0)
        b1 = jnp.concatenate([t1_ref[u, own, :] for u in range(qb // Q_BLOCK)], axis=0)
        top = None
        for c, cols in enumerate(chunks):
            bias = jnp.where(c == c_hi, b0, jnp.where(c == c_hi - 1, b1,
                                                      jnp.where(c > c_hi, NEG_BIG, 0.0)))
            x = lg_ref[rows, cols] + bias
            lg_ref[rows, cols] = x
            top = x if top is None else jnp.maximum(top, x)
        row_max.append(jnp.max(top, axis=1, keepdims=True))
    scale = []
    for rows, m in zip(heads, row_max):
        part = None
        for cols in chunks:
            e = jnp.exp2(lg_ref[rows, cols] - m)
            lg_ref[rows, cols] = e
            part = e if part is None else part + e
        total = jnp.maximum(jnp.sum(part, axis=1, keepdims=True), 1e-30)
        scale.append(jnp.where(m > 0.5 * NEG_BIG, 1.0 / total, 0.0))
    for cols in chunks:
        imp = None
        for rows, inv in zip(heads, scale):
            p = lg_ref[rows, cols] * inv
            p_ref[rows, cols] = p.astype(BF16)
            imp = p if imp is None else imp + p
        imp_ref[:, cols] = imp
    o = jnp.dot(p_ref[...], vc_ref[...], preferred_element_type=F32)
    for r in range(HEADS_PER_GROUP):
        o_ref[:, r * HEAD_DIM:(r + 1) * HEAD_DIM] = o[heads[r], :]
    imp = imp_ref[...]

    hi = imp.astype(BF16)
    rest = imp - hi.astype(F32)
    mid = rest.astype(BF16)
    low = (rest - mid.astype(F32)).astype(BF16)
    a = a_ref[...]
    imp_sel = (jnp.dot(hi, a, preferred_element_type=F32) + jnp.dot(mid, a, preferred_element_type=F32)
               + jnp.dot(low, a, preferred_element_type=F32))
    imp_sel = imp_sel.T
    t = i * qb + lax.broadcasted_iota(jnp.int32, (1, qb), 1)
    blk_t = t >> (SEL_BLOCK.bit_length() - 1)
    j = lax.broadcasted_iota(jnp.int32, (nselp, 1), 0)
    forced = (j == 0) | (j == blk_t) | (j == blk_t - 1)
    score = jnp.where(forced, PICKED, jnp.where(j <= blk_t, imp_sel, -1.0))
    jf = jnp.broadcast_to(j.astype(F32), (nselp, qb))

    def pick(_, score):
        top = jnp.max(score, axis=0, keepdims=True)
        first = jnp.min(jnp.where(score == top, jf, float(nselp)), axis=0, keepdims=True)
        return jnp.where(jf == first, PICKED, score)

    score = lax.fori_loop(0, n_top - N_FORCED, pick, score)
    sel_ref[...] = jnp.where(score == PICKED, 0.0, NEG_BIG).T.astype(BF16)


def _cmp_select(q, kv_cmp, t_cmp, agg, *, n_top):
    _, s, _ = q.shape
    g, r4 = N_KV_GROUPS, HEADS_PER_GROUP
    ncp = kv_cmp.shape[2]
    nselp = agg.shape[1]
    qb = CMP_TILES * Q_BLOCK
    rows = r4 * qb
    table_rows = r4 * Q_BLOCK
    steps_per_chunk = LANES * CMP_STRIDE // qb
    table = lambda shift: pl.BlockSpec(
        (None, CMP_TILES, table_rows, LANES),
        lambda gg, i: (gg, i % steps_per_chunk + shift * steps_per_chunk, 0, 0))
    return pl.pallas_call(
        functools.partial(_cmp_kernel, n_top=n_top),
        grid=(g, s // qb),
        in_specs=[pl.BlockSpec((r4, qb, HEAD_DIM), lambda gg, i: (gg, i, 0)),
                  pl.BlockSpec((None, None, ncp, HEAD_DIM), lambda gg, i: (0, gg, 0, 0)),
                  pl.BlockSpec((None, None, ncp, HEAD_DIM), lambda gg, i: (1, gg, 0, 0)),
                  table(0), table(1),
                  pl.BlockSpec((ncp, nselp), lambda gg, i: (0, 0))],
        out_specs=[pl.BlockSpec((qb, r4 * HEAD_DIM), lambda gg, i: (i, gg)),
                   pl.BlockSpec((None, qb, nselp), lambda gg, i: (gg, i, 0))],
        out_shape=[jax.ShapeDtypeStruct((s, N_HEADS * HEAD_DIM), F32),
                   jax.ShapeDtypeStruct((g, s, nselp), BF16)],
        scratch_shapes=[pltpu.VMEM((rows, ncp), F32), pltpu.VMEM((rows, ncp), BF16),
                        pltpu.VMEM((qb, ncp), F32)],
        compiler_params=_params(("parallel", "arbitrary")),
        name="nsa_compressed_select",
    )(q, kv_cmp, kv_cmp, t_cmp, t_cmp, agg)


def _sel_win_kernel(*refs):
    (q_ref, sb_ref, ka_ref, vs_ref, ts_ref, tw_ref, oc_ref, gl_ref) = refs[:8]
    kw_refs = refs[8:8 + WIN_BLOCKS]
    vw_refs = refs[8 + WIN_BLOCKS:8 + 2 * WIN_BLOCKS]
    (x_ref, qa_ref, sa_ref, sb2_ref, sw_ref, ow_ref, m_ref, l_ref, acc_ref) = refs[8 + 2 * WIN_BLOCKS:]
    i = pl.program_id(1)
    qb = SEL_TILES * Q_BLOCK
    tile0 = i * SEL_TILES
    head_rows = [slice(r * qb, (r + 1) * qb) for r in range(HEADS_PER_GROUP)]
    chains = [(u, slice(r * qb + u * Q_BLOCK, r * qb + (u + 1) * Q_BLOCK),
               slice(r * Q_BLOCK, (r + 1) * Q_BLOCK))
              for r in range(HEADS_PER_GROUP) for u in range(SEL_TILES)]
    tiles_per_chunk = LANES * SEL_BLOCK // KEY_TILE
    for c in range(qa_ref.shape[0]):
        sb = sb_ref[:, c * LANES:(c + 1) * LANES]
        for r, rows in enumerate(head_rows):
            qa_ref[c, rows, :HEAD_DIM] = q_ref[r]
            qa_ref[c, rows, HEAD_DIM:] = sb

    _reset(m_ref, l_ref, acc_ref)
    last = (tile0 + SEL_TILES - 1) // TILE_CHUNKS

    def keys(j):
        return pl.ds(pl.multiple_of(j * KEY_TILE, KEY_TILE), KEY_TILE)

    def logits(j, buf):
        buf[...] = _nt_dot(qa_ref[j // tiles_per_chunk], ka_ref[keys(j), :])

    def attend(j, buf, biased=True):
        v = vs_ref[keys(j), :]
        first = SEL_BAND_BEHIND - (tile0 - TILE_CHUNKS * j)
        for u, rows, own in chains:
            s = buf[rows, :]
            if biased:
                s = s + jnp.concatenate(
                    [ts_ref[jnp.maximum(first - u + c, 0), own, :] for c in range(TILE_CHUNKS)], axis=1)
            _online_update(s, v, m_ref, l_ref, acc_ref, rows)

    def step(j, cur, nxt, biased=True):
        logits(j + 1, nxt)
        attend(j, cur, biased)

    kw = jnp.concatenate([kw_refs[b][...] for b in reversed(range(WIN_BLOCKS))], axis=0)
    vw = jnp.concatenate([vw_refs[b][...] for b in reversed(range(WIN_BLOCKS))], axis=0)
    span = WINDOW + Q_BLOCK
    lane_chunk = lax.broadcasted_iota(jnp.int32, (1, span), 1) >> (LANES.bit_length() - 1)
    sw_ref[...] = _nt_dot(q_ref[...].reshape(HEADS_PER_GROUP * qb, HEAD_DIM), kw)
    logits(0, sa_ref)
    for u, rows, own in chains:
        before_start = jnp.where(WIN_CHUNKS - 1 - lane_chunk > tile0 + u, NEG_BIG, 0.0)
        s = sw_ref[rows, u * Q_BLOCK:u * Q_BLOCK + span] + (tw_ref[own, :] + before_start)
        p = jnp.exp2(s - jnp.max(s, axis=1, keepdims=True))
        o = jnp.dot(p.astype(BF16), vw[u * Q_BLOCK:u * Q_BLOCK + span, :], preferred_element_type=F32)
        ow_ref[rows, :] = o / jnp.sum(p, axis=1, keepdims=True)

    far_tiles = jnp.maximum(tile0 - SEL_BAND_BEHIND - 1 + TILE_CHUNKS, 0) // TILE_CHUNKS

    def steps(first, count, biased=True):
        for n in range(0, count, 2):
            step(first + n, sa_ref, sb2_ref, biased)
            step(first + n + 1, sb2_ref, sa_ref, biased)

    def run(first, count, unroll, biased):
        def body(p, carry):
            steps(first + unroll * p, unroll, biased)
            return carry
        lax.fori_loop(0, count, body, 0)

    far_runs = far_tiles // FAR_UNROLL
    run(0, far_runs, FAR_UNROLL, False)
    start = FAR_UNROLL * far_runs
    left = last - start + 1
    runs = (left - 1) // UNROLL
    run(start, runs, UNROLL, True)
    start = start + UNROLL * runs
    left = left - UNROLL * runs

    @pl.when(left > 2)
    def _():
        steps(start, 2)

    @pl.when(left % 2 == 1)
    def _():
        attend(last, sa_ref)

    @pl.when(left % 2 == 0)
    def _():
        step(last - 1, sa_ref, sb2_ref)
        attend(last, sb2_ref)

    gates = jax.nn.sigmoid(gl_ref[...])
    o_sel = _normalized(l_ref, acc_ref)
    for r, rows in enumerate(head_rows):
        cols = slice(r * HEAD_DIM, (r + 1) * HEAD_DIM)
        mix = (gates[:, 3 * r:3 * r + 1] * oc_ref[:, cols] + gates[:, 3 * r + 1:3 * r + 2] * o_sel[rows, :]
               + gates[:, 3 * r + 2:3 * r + 3] * ow_ref[rows, :])
        x_ref[:, cols] = mix.astype(BF16)


def _sel_win(q, selbias, k_aug, kv, t_sel, t_win, o_cmp, gate_logits):
    _, s, _ = q.shape
    g, r4 = N_KV_GROUPS, HEADS_PER_GROUP
    qb = SEL_TILES * Q_BLOCK
    rows = r4 * qb
    table_rows = r4 * Q_BLOCK
    width = r4 * HEAD_DIM
    nselp = selbias.shape[2]
    once = pl.Buffered(1)
    whole = lambda slot: pl.BlockSpec((None, s, HEAD_DIM), lambda gg, i: (slot * g + gg, 0, 0),
                                      pipeline_mode=once)

    def win(slot, behind):
        return pl.BlockSpec((None, Q_BLOCK, HEAD_DIM),
                            lambda gg, i: (slot * g + gg,
                                           jnp.maximum(i * SEL_TILES + SEL_TILES - 1 - behind, 0), 0))

    in_specs = [pl.BlockSpec((r4, qb, HEAD_DIM), lambda gg, i: (gg, i, 0)),
                pl.BlockSpec((None, qb, nselp), lambda gg, i: (gg, i, 0)),
                pl.BlockSpec((None, s, 2 * HEAD_DIM), lambda gg, i: (gg, 0, 0), pipeline_mode=once),
                whole(1),
                pl.BlockSpec((None, SEL_BAND_CHUNKS, table_rows, LANES), lambda gg, i: (gg, 0, 0, 0),
                             pipeline_mode=once),
                pl.BlockSpec((None, table_rows, WINDOW + Q_BLOCK), lambda gg, i: (gg, 0, 0),
                             pipeline_mode=once),
                pl.BlockSpec((qb, width), lambda gg, i: (i, gg)),
                pl.BlockSpec((None, qb, LANES), lambda gg, i: (gg, i, 0))]
    in_specs += [win(2, b) for b in range(WIN_BLOCKS)] + [win(3, b) for b in range(WIN_BLOCKS)]
    per_group = 3 * r4
    group_gates = jnp.stack([jnp.pad(gate_logits[:, per_group * gg:per_group * (gg + 1)],
                                     ((0, 0), (0, LANES - per_group))) for gg in range(g)])
    return pl.pallas_call(
        _sel_win_kernel,
        grid=(g, s // qb),
        in_specs=in_specs,
        out_specs=pl.BlockSpec((qb, width), lambda gg, i: (i, gg)),
        out_shape=jax.ShapeDtypeStruct((s, N_HEADS * HEAD_DIM), BF16),
        scratch_shapes=[pltpu.VMEM((nselp // LANES, rows, 2 * HEAD_DIM), BF16),
                        pltpu.VMEM((rows, KEY_TILE), F32), pltpu.VMEM((rows, KEY_TILE), F32),
                        pltpu.VMEM((rows, WINDOW + qb), F32), pltpu.VMEM((rows, HEAD_DIM), F32),
                        pltpu.VMEM((rows, LANES), F32), pltpu.VMEM((rows, LANES), F32),
                        pltpu.VMEM((rows, HEAD_DIM), F32)],
        compiler_params=_params(("parallel", "arbitrary")),
        name="nsa_selected_window",
    )(q, selbias, k_aug, kv, t_sel, t_win, o_cmp, group_gates, *([kv] * (2 * WIN_BLOCKS)))


def _selection_aggregator(ncp, nselp):
    n = np.arange(ncp)[:, None]
    j = np.arange(nselp)[None, :]
    hit = (n >= SEL_RATIO * j - 1) & (n <= SEL_RATIO * j + SEL_RATIO - 1) & (n < ncp - 1)
    return jnp.asarray(hit.astype(np.float32), dtype=BF16)


def _block_onehot(s):
    blk = (np.arange(s) // SEL_BLOCK) % LANES
    return jnp.asarray((blk[:, None] == np.arange(LANES)[None, :]).astype(np.float32), dtype=BF16)


def _round_up(x, m):
    return (x + m - 1) // m * m


def kernel(x, fox_w_in, fox_b_f, fox_w_o, nsa_w_in, nsa_w_o, kv_w, cmp_pos_k, cmp_pos_v, cmp_k_w1, cmp_k_w2, cmp_v_w1, cmp_v_w2, rel_bias, mlp_w1, mlp_w2, ln1_g, ln1_b, ln2_g, ln2_b):
    b, s, d = x.shape
    assert b == 1 and d == D_MODEL and s % (CMP_STRIDE * LANES) == 0
    hd = N_HEADS * HEAD_DIM
    scale = HEAD_DIM ** -0.5 * LOG2E
    tm = min(512, s)
    tp = min(1024, s)
    fox_t = min(512, s)
    pad_cols = lambda w: jnp.pad(w, ((0, 0), (0, LANES - w.shape[1])))

    fox_in, fox_out = fox_w_in.astype(BF16), fox_w_o.astype(BF16)
    nsa_in, nsa_out = nsa_w_in.astype(BF16), nsa_w_o.astype(BF16)
    w1_all, w2_all = mlp_w1.astype(BF16), mlp_w2.astype(BF16)
    project = functools.partial(_matmul, tm=tp, tn=512, out_dtype=BF16, head_major=True)
    gate_project = functools.partial(_matmul, tm=tp, tn=LANES, out_dtype=F32, head_major=False)

    h = x[0]
    hb = h.astype(BF16)
    kv_state = None
    for layer in range(DEPTH):
        g1, b1 = ln1_g[layer][None, :], ln1_b[layer][None, :]
        g2, b2 = ln2_g[layer][None, :], ln2_b[layer][None, :]
        if layer < N_A_LAYERS:
            qkv = project(hb, fox_in, layer=layer, n=3 * hd, scaled_cols=hd, scale=scale)
            gate_logits = gate_project(hb, pad_cols(fox_in[layer][:, 3 * hd:]))
            cum = _forget_cumsum(gate_logits, fox_b_f[layer])
            cum_rows = cum[:, :N_HEADS].T.reshape(N_HEADS, s // fox_t, 1, fox_t)
            attn = _fox_attention(qkv, cum_rows, _first_fox_tile(qkv, cum, t=fox_t), t=fox_t)
            h, hb = _oproj(attn, fox_out, layer, h, g1, b1, tm=tm)
        else:
            nsa_layer = layer - N_A_LAYERS
            k_cmp_v_cmp, kv_tok, k_aug, tables, agg, n_top = kv_state
            t_sel, t_win, t_cmp = tables
            q = project(hb, nsa_in, layer=nsa_layer, n=hd, scaled_cols=hd, scale=scale)
            gate_logits = gate_project(hb, pad_cols(nsa_in[nsa_layer][:, hd:]))
            o_cmp, selbias = _cmp_select(q, k_cmp_v_cmp, t_cmp, agg, n_top=n_top)
            mixed = _sel_win(q, selbias, k_aug, kv_tok, t_sel, t_win, o_cmp, gate_logits)
            h, hb = _oproj(mixed, nsa_out, nsa_layer, h, g1, b1, tm=tm)
        h, hb = _mlp(hb, w1_all, w2_all, layer, h, g2, b2, tm=tm, tf=1024)
        if layer == N_A_LAYERS - 1:
            gd = N_KV_GROUPS * HEAD_DIM
            kvw = kv_w.astype(BF16)
            kv_raw = project(hb, kvw, n=2 * gd, out_dtype=F32)
            kv_tok = project(hb, kvw, n=4 * gd, first_col=2 * gd)
            pos = jnp.stack([cmp_pos_k.reshape(1, -1), cmp_pos_v.reshape(1, -1)])
            w1 = jnp.stack([cmp_k_w1, cmp_v_w1]).astype(BF16)
            w2 = jnp.stack([cmp_k_w2, cmp_v_w2]).astype(BF16)
            kv_cmp = _compress(kv_raw, pos, w1, w2)
            n_sel = s // SEL_BLOCK
            nselp = _round_up(n_sel, LANES)
            onehot = jnp.broadcast_to(_block_onehot(s), (N_KV_GROUPS, s, LANES))
            k_aug = jnp.concatenate([kv_tok[:N_KV_GROUPS], onehot], axis=-1)
            tables = _bias_tables(_bias_vec(rel_bias))
            agg = _selection_aggregator(s // CMP_STRIDE, nselp)
            kv_state = (kv_cmp, kv_tok, k_aug, tables, agg, min(N_SELECTED, n_sel))
    return h[None]
```

```python
import functools
import math

import numpy as np
import jax
import jax.numpy as jnp
from jax import lax
from jax.experimental import pallas as pl
from jax.experimental.pallas import tpu as pltpu

D_MODEL = 2048
DEPTH = 4
HEAD_DIM = 128
N_HEADS = D_MODEL // HEAD_DIM
N_KV_GROUPS = 4
HEADS_PER_GROUP = N_HEADS // N_KV_GROUPS
D_FF = 4 * D_MODEL
N_A_LAYERS = DEPTH // 2
Q_BLOCK = 128
CMP_BLOCK = 32
CMP_STRIDE = 16
CMP_HIDDEN = 256
SEL_BLOCK = 64
N_SELECTED = 16
SEL_RATIO = SEL_BLOCK // CMP_STRIDE
WINDOW = 512
N_BUCKETS = 32
REL_MAX_DIST = 2048
ALPHA = (2.0 * DEPTH) ** 0.25
LN_EPS = 1e-5
N_FORCED = 3
PICKED = -2.0
NEG_BIG = -1e30
LOG2E = math.log2(math.e)

LANES = 128
MXU_WIDTH = 256
VMEM_LIMIT = 56 * 1024 * 1024
BF16 = jnp.bfloat16
F32 = jnp.float32

_EXACT = N_BUCKETS // 2
_BUCKET_THRESHOLDS = tuple(
    int(math.ceil(_EXACT * (REL_MAX_DIST / _EXACT) ** (k / (N_BUCKETS - _EXACT))))
    for k in range(1, N_BUCKETS - _EXACT))
BIAS_RANGE = 2048
KEY_TILE = 512
TILE_CHUNKS = KEY_TILE // LANES
SEL_BAND_BEHIND = BIAS_RANGE // LANES + TILE_CHUNKS - 2
SEL_BAND_CHUNKS = SEL_BAND_BEHIND + TILE_CHUNKS
WIN_CHUNKS = WINDOW // Q_BLOCK + 1
SEL_TILES = 2
WIN_BLOCKS = WIN_CHUNKS - 1 + SEL_TILES
UNROLL = 4
FAR_UNROLL = 8
CMP_TABLES = 32
CMP_TILES = 8
SUB_ROWS = 128
FOX_TILES = 2
FOX_SKIP_MARGIN = 160.0


def _nt_dot(a, b):
    return lax.dot_general(a, b, (((1,), (1,)), ((), ())), preferred_element_type=F32)


def _params(sem):
    return pltpu.CompilerParams(dimension_semantics=sem, vmem_limit_bytes=VMEM_LIMIT)


def _layer_norm(y, g, b):
    mu = jnp.mean(y, axis=-1, keepdims=True)
    yc = y - mu
    var = jnp.mean(yc * yc, axis=-1, keepdims=True)
    return yc * lax.rsqrt(var + LN_EPS) * g + b


def _matmul_kernel(x_ref, w_ref, o_ref, *, tn, scaled_cols, scale, head_major):
    acc = jnp.dot(x_ref[...], w_ref[...], preferred_element_type=F32)
    if scaled_cols:
        j = pl.program_id(1)
        acc = acc * jnp.where(j * tn < scaled_cols, scale, 1.0).astype(F32)
    if head_major:
        for c in range(tn // LANES):
            o_ref[c] = acc[:, c * LANES:(c + 1) * LANES].astype(o_ref.dtype)
    else:
        o_ref[...] = acc.astype(o_ref.dtype)


def _weight_spec(w, layer, block, index_map):
    if w.ndim == 2:
        return pl.BlockSpec(block, index_map)
    return pl.BlockSpec((None,) + block, lambda *idx: (layer,) + index_map(*idx))


def _matmul(x, w, *, n=None, first_col=0, layer=None, out_dtype, head_major, tm, tn,
            scaled_cols=0, scale=1.0):
    m, k = x.shape
    n = w.shape[-1] if n is None else n
    assert m % tm == 0 and n % tn == 0 and scaled_cols % tn == 0 and first_col % tn == 0
    col0 = first_col // tn
    w_spec = _weight_spec(w, layer, (k, tn), lambda i, j: (0, col0 + j))
    if head_major:
        out_shape = jax.ShapeDtypeStruct((n // LANES, m, LANES), out_dtype)
        out_spec = pl.BlockSpec((tn // LANES, tm, LANES), lambda i, j: (j, i, 0))
    else:
        out_shape = jax.ShapeDtypeStruct((m, n), out_dtype)
        out_spec = pl.BlockSpec((tm, tn), lambda i, j: (i, j))
    return pl.pallas_call(
        functools.partial(_matmul_kernel, tn=tn, scaled_cols=scaled_cols, scale=scale,
                          head_major=head_major),
        grid=(m // tm, n // tn),
        in_specs=[pl.BlockSpec((tm, k), lambda i, j: (i, 0)), w_spec],
        out_specs=out_spec,
        out_shape=out_shape,
        compiler_params=_params(("parallel", "arbitrary")),
        name="proj_matmul",
    )(x, w)


def _cum_kernel(gl_ref, b_ref, o_ref, *, nblk):
    r = lax.broadcasted_iota(jnp.int32, (LANES, LANES), 0)
    c = lax.broadcasted_iota(jnp.int32, (LANES, LANES), 1)
    tri = (c <= r).astype(F32)
    bias = b_ref[...]

    def body(i, carry):
        rows = pl.ds(pl.multiple_of(i * LANES, LANES), LANES)
        lf = jax.nn.log_sigmoid(gl_ref[rows, :] + bias)
        cum = jnp.dot(tri, lf, preferred_element_type=F32,
                      precision=lax.Precision.HIGHEST) + carry
        o_ref[rows, :] = cum
        return cum[LANES - 1:LANES, :]

    lax.fori_loop(0, nblk, body, jnp.zeros((1, LANES), F32))


def _forget_cumsum(gate_logits, b_f):
    s = gate_logits.shape[0]
    bias = jnp.zeros((1, LANES), F32).at[0, :N_HEADS].set(b_f)
    return pl.pallas_call(
        functools.partial(_cum_kernel, nblk=s // LANES),
        out_shape=jax.ShapeDtypeStruct((s, LANES), F32),
        compiler_params=pltpu.CompilerParams(vmem_limit_bytes=VMEM_LIMIT),
        name="forget_cumsum",
    )(gate_logits, bias)


def _online_update(s, v, m_ref, l_ref, acc_ref, rows):
    m_prev = m_ref[rows, :]
    m_next = jnp.maximum(m_prev, jnp.max(s, axis=1, keepdims=True))
    alpha = jnp.exp2(m_prev - m_next)
    reps = s.shape[1] // LANES
    p = jnp.exp2(s - jnp.concatenate([m_next] * reps, axis=1))
    part = p[:, :LANES]
    for c in range(1, reps):
        part = part + p[:, c * LANES:(c + 1) * LANES]
    l_ref[rows, :] = alpha * l_ref[rows, :] + part
    acc_ref[rows, :] = alpha * acc_ref[rows, :] + jnp.dot(p.astype(BF16), v,
                                                          preferred_element_type=F32)
    m_ref[rows, :] = m_next


def _reset(m_ref, l_ref, acc_ref):
    m_ref[...] = jnp.full_like(m_ref, NEG_BIG)
    l_ref[...] = jnp.zeros_like(l_ref)
    acc_ref[...] = jnp.zeros_like(acc_ref)


def _normalized(l_ref, acc_ref):
    return acc_ref[...] / jnp.sum(l_ref[...], axis=1, keepdims=True)


def _fox_kernel(first_ref, q_ref, k_ref, v_ref, c_ref, o_ref, sa_ref, sb_ref, m_ref, l_ref, acc_ref, *, t):
    i = pl.program_id(1)
    diag0 = i * FOX_TILES
    c0 = jnp.max(c_ref[diag0], axis=1, keepdims=True)
    _reset(m_ref, l_ref, acc_ref)
    per_tile = t // SUB_ROWS

    def keys(j):
        return pl.ds(pl.multiple_of(j * t, t), t)

    def logits(j, buf):
        buf[...] = _nt_dot(q_ref[...], k_ref[keys(j), :]) + (c0 - c_ref[j]) * LOG2E

    def attend(j, buf, diagonal=None):
        v = v_ref[keys(j), :]
        for u in range(FOX_TILES):
            if diagonal is not None and u < diagonal:
                continue
            for r in range(per_tile):
                rows = slice(u * t + r * SUB_ROWS, u * t + (r + 1) * SUB_ROWS)
                if diagonal == u:
                    width = (r + 1) * SUB_ROWS
                    row = r * SUB_ROWS + lax.broadcasted_iota(jnp.int32, (SUB_ROWS, width), 0)
                    col = lax.broadcasted_iota(jnp.int32, (SUB_ROWS, width), 1)
                    s = jnp.where(col <= row, buf[rows, :width], NEG_BIG)
                    _online_update(s, v[:width, :], m_ref, l_ref, acc_ref, rows)
                else:
                    _online_update(buf[rows, :], v, m_ref, l_ref, acc_ref, rows)

    def step(j, cur, nxt, diagonal=None):
        logits(j + 1, nxt)
        attend(j, cur, diagonal)

    first = first_ref[pl.program_id(0), i]
    n = diag0 - first
    logits(first, sa_ref)

    def pair(p, carry):
        step(first + 2 * p, sa_ref, sb_ref)
        step(first + 2 * p + 1, sb_ref, sa_ref)
        return carry

    lax.fori_loop(0, n // 2, pair, 0)

    def finish(cur, nxt):
        for d in range(FOX_TILES):
            if d + 1 < FOX_TILES:
                step(diag0 + d, cur, nxt, diagonal=d)
            else:
                attend(diag0 + d, cur, diagonal=d)
            cur, nxt = nxt, cur

    @pl.when(n % 2 == 0)
    def _():
        finish(sa_ref, sb_ref)

    @pl.when(n % 2 == 1)
    def _():
        step(diag0 - 1, sa_ref, sb_ref)
        finish(sb_ref, sa_ref)

    o_ref[...] = _normalized(l_ref, acc_ref).astype(o_ref.dtype)


def _first_fox_tile(qkv, cum, *, t):
    h = N_HEADS
    s = qkv.shape[1]
    norm = lambda a: jnp.sqrt(jnp.max(jnp.sum(jnp.square(a.astype(F32)), axis=-1), axis=-1))
    bound = FOX_SKIP_MARGIN + 2.0 * norm(qkv[:h]) * norm(qkv[h:2 * h])
    ct = cum[:, :h].T
    at_query_start = ct[:, ::t]
    at_key_end = ct[:, t - 1::t]
    gap = (at_key_end[:, None, :] - at_query_start[:, :, None]) * LOG2E
    idx = np.arange(s // t)
    earlier = jnp.asarray(idx[None, :] < idx[:, None])
    skip = earlier[None] & (gap >= bound[:, None, None])
    return jnp.sum(skip, axis=-1).astype(jnp.int32)


def _fox_attention(qkv, cum_rows, first_tile, *, t):
    _, s, _ = qkv.shape
    h = N_HEADS
    rows = FOX_TILES * t
    first_tile = jnp.min(first_tile.reshape(h, s // rows, FOX_TILES), axis=-1)
    grid_spec = pltpu.PrefetchScalarGridSpec(
        num_scalar_prefetch=1,
        grid=(h, s // rows),
        in_specs=[pl.BlockSpec((None, rows, HEAD_DIM), lambda hh, i, f: (hh, i, 0)),
                  pl.BlockSpec((None, s, HEAD_DIM), lambda hh, i, f: (h + hh, 0, 0)),
                  pl.BlockSpec((None, s, HEAD_DIM), lambda hh, i, f: (2 * h + hh, 0, 0)),
                  pl.BlockSpec((None, s // t, 1, t), lambda hh, i, f: (hh, 0, 0, 0))],
        out_specs=pl.BlockSpec((rows, HEAD_DIM), lambda hh, i, f: (i, hh)),
        scratch_shapes=[pltpu.VMEM((rows, t), F32), pltpu.VMEM((rows, t), F32),
                        pltpu.VMEM((rows, LANES), F32), pltpu.VMEM((rows, LANES), F32),
                        pltpu.VMEM((rows, HEAD_DIM), F32)])
    return pl.pallas_call(
        functools.partial(_fox_kernel, t=t),
        grid_spec=grid_spec,
        out_shape=jax.ShapeDtypeStruct((s, h * HEAD_DIM), BF16),
        compiler_params=_params(("parallel", "arbitrary")),
        name="fox_attention",
    )(first_tile, qkv, qkv, qkv, cum_rows)


def _emit_norm(y, g_ref, b_ref, of_ref, ob_ref):
    hn = _layer_norm(y, g_ref[...], b_ref[...])
    of_ref[...] = hn
    ob_ref[...] = hn.astype(BF16)


def _oproj_kernel(a_ref, w_ref, h_ref, g_ref, b_ref, of_ref, ob_ref):
    mix = jnp.dot(a_ref[...], w_ref[...], preferred_element_type=F32)
    _emit_norm(ALPHA * h_ref[...] + mix, g_ref, b_ref, of_ref, ob_ref)


def _row_spec(tm, n):
    return pl.BlockSpec((tm, n), lambda i: (i, 0))


def _const_spec(shape):
    return pl.BlockSpec(shape, lambda i: (0,) * len(shape))


def _norm_outs(s, tm):
    d = D_MODEL
    return dict(
        out_specs=[_row_spec(tm, d), _row_spec(tm, d)],
        out_shape=[jax.ShapeDtypeStruct((s, d), F32), jax.ShapeDtypeStruct((s, d), BF16)])


def _oproj(a, w, layer, h, g, b, *, tm):
    s, d = h.shape
    return pl.pallas_call(
        _oproj_kernel,
        grid=(s // tm,),
        in_specs=[_row_spec(tm, d), _weight_spec(w, layer, (d, d), lambda i: (0, 0)), _row_spec(tm, d),
                  _const_spec((1, d)), _const_spec((1, d))],
        compiler_params=_params(("parallel",)),
        name="oproj_norm",
        **_norm_outs(s, tm),
    )(a, w, h, g, b)


def _mlp_kernel(x_ref, w1_ref, w2_ref, h_ref, g_ref, b_ref, of_ref, ob_ref, acc_ref):
    f = pl.program_id(1)

    @pl.when(f == 0)
    def _():
        acc_ref[...] = jnp.zeros_like(acc_ref)

    x = x_ref[...]
    chunks = [slice(c * MXU_WIDTH, (c + 1) * MXU_WIDTH) for c in range(w1_ref.shape[1] // MXU_WIDTH)]
    hidden = [jnp.dot(x, w1_ref[:, cols], preferred_element_type=F32) for cols in chunks]
    for cols, u in zip(chunks, hidden):
        u = jnp.maximum(u, 0.0)
        acc_ref[...] += jnp.dot((u * u).astype(BF16), w2_ref[cols, :], preferred_element_type=F32)

    @pl.when(f == pl.num_programs(1) - 1)
    def _():
        _emit_norm(ALPHA * h_ref[...] + acc_ref[...], g_ref, b_ref, of_ref, ob_ref)


def _mlp(xb, w1, w2, layer, h, g, b, *, tm, tf):
    s, d = h.shape
    ff = w1.shape[-1]
    row = lambda n: pl.BlockSpec((tm, n), lambda i, f: (i, 0))
    vec = pl.BlockSpec((1, d), lambda i, f: (0, 0))
    return pl.pallas_call(
        _mlp_kernel,
        grid=(s // tm, ff // tf),
        in_specs=[row(d), _weight_spec(w1, layer, (d, tf), lambda i, f: (0, f)),
                  _weight_spec(w2, layer, (tf, d), lambda i, f: (f, 0)), row(d), vec, vec],
        out_specs=[row(d), row(d)],
        out_shape=[jax.ShapeDtypeStruct((s, d), F32), jax.ShapeDtypeStruct((s, d), BF16)],
        scratch_shapes=[pltpu.VMEM((tm, d), F32)],
        compiler_params=_params(("parallel", "arbitrary")),
        name="mlp_norm",
    )(xb, w1, w2, h, g, b)


def _compress_kernel(x_ref, pos_ref, w1_ref, w2_ref, o_ref):
    half = CMP_STRIDE * HEAD_DIM
    x = x_ref[...]
    first = jnp.dot((x + pos_ref[:, :half]).astype(BF16), w1_ref[:half, :],
                    preferred_element_type=F32)
    second = jnp.dot((x + pos_ref[:, half:]).astype(BF16), w1_ref[half:, :],
                     preferred_element_type=F32)
    n = x.shape[0]
    hidden = first + pltpu.roll(second, n - 1, 0)
    act = jax.nn.gelu(hidden)
    o_ref[...] = jnp.dot(act.astype(BF16), w2_ref[...], preferred_element_type=F32).astype(BF16)


def _compress(kv_raw, pos, w1, w2):
    g = N_KV_GROUPS
    s = kv_raw.shape[1]
    nc = s // CMP_STRIDE
    wide = CMP_STRIDE * HEAD_DIM
    x = kv_raw.reshape(2 * g, nc, wide)
    return pl.pallas_call(
        _compress_kernel,
        grid=(2, g),
        in_specs=[pl.BlockSpec((None, nc, wide), lambda a, gg: (a * g + gg, 0, 0)),
                  pl.BlockSpec((None, 1, 2 * wide), lambda a, gg: (a, 0, 0)),
                  pl.BlockSpec((None, 2 * wide, CMP_HIDDEN), lambda a, gg: (a, 0, 0)),
                  pl.BlockSpec((None, CMP_HIDDEN, HEAD_DIM), lambda a, gg: (a, 0, 0))],
        out_specs=pl.BlockSpec((None, None, nc, HEAD_DIM), lambda a, gg: (a, gg, 0, 0)),
        out_shape=jax.ShapeDtypeStruct((2, g, nc, HEAD_DIM), BF16),
        compiler_params=_params(("parallel", "parallel")),
        name="compress_kv",
    )(x, pos, w1, w2)


def _bias_vec_kernel(t_ref, o_ref):
    d = lax.broadcasted_iota(jnp.int32, (1, BIAS_RANGE), 1)
    large = jnp.full_like(d, _EXACT)
    for thr in _BUCKET_THRESHOLDS:
        large = large + (d >= thr).astype(jnp.int32)
    bucket = jnp.where(d < _EXACT, d, large)
    table = t_ref[...]
    acc = jnp.zeros((N_HEADS, BIAS_RANGE), F32)
    for bkt in range(N_BUCKETS):
        acc = jnp.where(bucket == bkt, table[:, bkt:bkt + 1], acc)
    o_ref[...] = (acc - table[:, N_BUCKETS - 1:N_BUCKETS]) * LOG2E


def _bias_vec(rel_bias):
    return pl.pallas_call(
        _bias_vec_kernel,
        out_shape=jax.ShapeDtypeStruct((N_HEADS, BIAS_RANGE), F32),
        name="rel_bias_by_distance",
    )(rel_bias.T)


def _toeplitz(w, rows, cols):
    h, width = w.shape
    flat = jnp.tile(w, (1, rows))[:, :rows * (width - 1)]
    return flat.reshape(h, rows, width - 1)[:, :, :cols]


def _by_distance(vec, d_hi, d_lo, window=None):
    h = vec.shape[0]
    limit = BIAS_RANGE if window is None else window
    above = jnp.full((h, d_hi - limit + 1), 0.0 if window is None else NEG_BIG, F32)
    below = jnp.full((h, -d_lo), NEG_BIG, F32)
    return jnp.concatenate([above, vec[:, limit - 1::-1], below], axis=1)


def _per_group(t):
    n = t.shape[2] // LANES
    t = t.reshape(N_KV_GROUPS, HEADS_PER_GROUP * Q_BLOCK, n, LANES)
    return t.transpose(0, 2, 1, 3)


def _bias_tables(vec):
    pad = jnp.zeros((N_HEADS, Q_BLOCK), F32)
    behind = SEL_BAND_BEHIND * LANES
    ahead = (SEL_BAND_CHUNKS - SEL_BAND_BEHIND) * LANES
    w = jnp.concatenate([_by_distance(vec, behind, 1 - ahead), pad], axis=1)
    t_sel = _per_group(_toeplitz(w, Q_BLOCK, SEL_BAND_CHUNKS * LANES))
    w = jnp.concatenate([_by_distance(vec, WINDOW, 1 - Q_BLOCK, window=WINDOW), pad + NEG_BIG], axis=1)
    t_win = _toeplitz(w, Q_BLOCK, WINDOW + Q_BLOCK)
    t_win = t_win.reshape(N_KV_GROUPS, HEADS_PER_GROUP * Q_BLOCK, WINDOW + Q_BLOCK)
    lo_rel = -(CMP_TABLES - 1) * (Q_BLOCK // CMP_STRIDE)
    n_rel = LANES - lo_rel
    end_min = CMP_STRIDE * lo_rel + CMP_BLOCK - 1
    w = jnp.concatenate([_by_distance(vec, -end_min, 1 - end_min - CMP_STRIDE * n_rel), pad], axis=1)
    band = _toeplitz(w, Q_BLOCK, CMP_STRIDE * n_rel)[:, :, ::CMP_STRIDE]
    per_tile = Q_BLOCK // CMP_STRIDE
    t_cmp = jnp.concatenate(
        [band[:, :, -lo_rel - per_tile * e:-lo_rel - per_tile * e + LANES] for e in range(CMP_TABLES)], axis=2)
    return t_sel, t_win, _per_group(t_cmp)


def _cmp_kernel(q_ref, kc_ref, vc_ref, t0_ref, t1_ref, a_ref, o_ref, sel_ref, lg_ref, p_ref, imp_ref,
                *, n_top):
    i = pl.program_id(1)
    ncp = kc_ref.shape[0]
    nselp = a_ref.shape[1]
    qb = q_ref.shape[1]
    c_hi = (i * (qb // CMP_STRIDE)) // LANES
    lg_ref[...] = _nt_dot(q_ref[...].reshape(HEADS_PER_GROUP * qb, HEAD_DIM), kc_ref[...])
    chunks = [slice(c * LANES, (c + 1) * LANES) for c in range(ncp // LANES)]
    heads = [slice(r * qb, (r + 1) * qb) for r in range(HEADS_PER_GROUP)]
    row_max = []
    for r, rows in enumerate(heads):
        own = slice(r * Q_BLOCK, (r + 1) * Q_BLOCK)
        b0 = jnp.concatenate([t0_ref[u, own, :] for u in range(qb // Q_BLOCK)], axis=0)
        b1 = jnp.concatenate([t1_ref[u, own, :] for u in range(qb // Q_BLOCK)], axis=0)
        top = None
        for c, cols in enumerate(chunks):
            bias = jnp.where(c == c_hi, b0, jnp.where(c == c_hi - 1, b1,
                                                      jnp.where(c > c_hi, NEG_BIG, 0.0)))
            x = lg_ref[rows, cols] + bias
            lg_ref[rows, cols] = x
            top = x if top is None else jnp.maximum(top, x)
        row_max.append(jnp.max(top, axis=1, keepdims=True))
    scale = []
    for rows, m in zip(heads, row_max):
        part = None
        for cols in chunks:
            e = jnp.exp2(lg_ref[rows, cols] - m)
            lg_ref[rows, cols] = e
            part = e if part is None else part + e
        total = jnp.maximum(jnp.sum(part, axis=1, keepdims=True), 1e-30)
        scale.append(jnp.where(m > 0.5 * NEG_BIG, 1.0 / total, 0.0))
    for cols in chunks:
        imp = None
        for rows, inv in zip(heads, scale):
            p = lg_ref[rows, cols] * inv
            p_ref[rows, cols] = p.astype(BF16)
            imp = p if imp is None else imp + p
        imp_ref[:, cols] = imp
    o = jnp.dot(p_ref[...], vc_ref[...], preferred_element_type=F32)
    for r in range(HEADS_PER_GROUP):
        o_ref[:, r * HEAD_DIM:(r + 1) * HEAD_DIM] = o[heads[r], :]
    imp = imp_ref[...]

    hi = imp.astype(BF16)
    rest = imp - hi.astype(F32)
    mid = rest.astype(BF16)
    low = (rest - mid.astype(F32)).astype(BF16)
    a = a_ref[...]
    imp_sel = (jnp.dot(hi, a, preferred_element_type=F32) + jnp.dot(mid, a, preferred_element_type=F32)
               + jnp.dot(low, a, preferred_element_type=F32))
    imp_sel = imp_sel.T
    t = i * qb + lax.broadcasted_iota(jnp.int32, (1, qb), 1)
    blk_t = t >> (SEL_BLOCK.bit_length() - 1)
    j = lax.broadcasted_iota(jnp.int32, (nselp, 1), 0)
    forced = (j == 0) | (j == blk_t) | (j == blk_t - 1)
    score = jnp.where(forced, PICKED, jnp.where(j <= blk_t, imp_sel, -1.0))
    jf = jnp.broadcast_to(j.astype(F32), (nselp, qb))

    def pick(_, score):
        top = jnp.max(score, axis=0, keepdims=True)
        first = jnp.min(jnp.where(score == top, jf, float(nselp)), axis=0, keepdims=True)
        return jnp.where(jf == first, PICKED, score)

    score = lax.fori_loop(0, n_top - N_FORCED, pick, score)
    sel_ref[...] = jnp.where(score == PICKED, 0.0, NEG_BIG).T.astype(BF16)


def _cmp_select(q, kv_cmp, t_cmp, agg, *, n_top):
    _, s, _ = q.shape
    g, r4 = N_KV_GROUPS, HEADS_PER_GROUP
    ncp = kv_cmp.shape[2]
    nselp = agg.shape[1]
    qb = CMP_TILES * Q_BLOCK
    rows = r4 * qb
    table_rows = r4 * Q_BLOCK
    steps_per_chunk = LANES * CMP_STRIDE // qb
    table = lambda shift: pl.BlockSpec(
        (None, CMP_TILES, table_rows, LANES),
        lambda gg, i: (gg, i % steps_per_chunk + shift * steps_per_chunk, 0, 0))
    return pl.pallas_call(
        functools.partial(_cmp_kernel, n_top=n_top),
        grid=(g, s // qb),
        in_specs=[pl.BlockSpec((r4, qb, HEAD_DIM), lambda gg, i: (gg, i, 0)),
                  pl.BlockSpec((None, None, ncp, HEAD_DIM), lambda gg, i: (0, gg, 0, 0)),
                  pl.BlockSpec((None, None, ncp, HEAD_DIM), lambda gg, i: (1, gg, 0, 0)),
                  table(0), table(1),
                  pl.BlockSpec((ncp, nselp), lambda gg, i: (0, 0))],
        out_specs=[pl.BlockSpec((qb, r4 * HEAD_DIM), lambda gg, i: (i, gg)),
                   pl.BlockSpec((None, qb, nselp), lambda gg, i: (gg, i, 0))],
        out_shape=[jax.ShapeDtypeStruct((s, N_HEADS * HEAD_DIM), F32),
                   jax.ShapeDtypeStruct((g, s, nselp), BF16)],
        scratch_shapes=[pltpu.VMEM((rows, ncp), F32), pltpu.VMEM((rows, ncp), BF16),
                        pltpu.VMEM((qb, ncp), F32)],
        compiler_params=_params(("parallel", "arbitrary")),
        name="nsa_compressed_select",
    )(q, kv_cmp, kv_cmp, t_cmp, t_cmp, agg)


def _sel_win_kernel(*refs):
    (q_ref, sb_ref, ka_ref, vs_ref, ts_ref, tw_ref, oc_ref, gl_ref) = refs[:8]
    kw_refs = refs[8:8 + WIN_BLOCKS]
    vw_refs = refs[8 + WIN_BLOCKS:8 + 2 * WIN_BLOCKS]
    (x_ref, qa_ref, sa_ref, sb2_ref, sw_ref, ow_ref, m_ref, l_ref, acc_ref) = refs[8 + 2 * WIN_BLOCKS:]
    i = pl.program_id(1)
    qb = SEL_TILES * Q_BLOCK
    tile0 = i * SEL_TILES
    head_rows = [slice(r * qb, (r + 1) * qb) for r in range(HEADS_PER_GROUP)]
    chains = [(u, slice(r * qb + u * Q_BLOCK, r * qb + (u + 1) * Q_BLOCK),
               slice(r * Q_BLOCK, (r + 1) * Q_BLOCK))
              for r in range(HEADS_PER_GROUP) for u in range(SEL_TILES)]
    tiles_per_chunk = LANES * SEL_BLOCK // KEY_TILE
    for c in range(qa_ref.shape[0]):
        sb = sb_ref[:, c * LANES:(c + 1) * LANES]
        for r, rows in enumerate(head_rows):
            qa_ref[c, rows, :HEAD_DIM] = q_ref[r]
            qa_ref[c, rows, HEAD_DIM:] = sb

    _reset(m_ref, l_ref, acc_ref)
    last = (tile0 + SEL_TILES - 1) // TILE_CHUNKS

    def keys(j):
        return pl.ds(pl.multiple_of(j * KEY_TILE, KEY_TILE), KEY_TILE)

    def logits(j, buf):
        buf[...] = _nt_dot(qa_ref[j // tiles_per_chunk], ka_ref[keys(j), :])

    def attend(j, buf, biased=True):
        v = vs_ref[keys(j), :]
        first = SEL_BAND_BEHIND - (tile0 - TILE_CHUNKS * j)
        for u, rows, own in chains:
            s = buf[rows, :]
            if biased:
                s = s + jnp.concatenate(
                    [ts_ref[jnp.maximum(first - u + c, 0), own, :] for c in range(TILE_CHUNKS)], axis=1)
            _online_update(s, v, m_ref, l_ref, acc_ref, rows)

    def step(j, cur, nxt, biased=True):
        logits(j + 1, nxt)
        attend(j, cur, biased)

    kw = jnp.concatenate([kw_refs[b][...] for b in reversed(range(WIN_BLOCKS))], axis=0)
    vw = jnp.concatenate([vw_refs[b][...] for b in reversed(range(WIN_BLOCKS))], axis=0)
    span = WINDOW + Q_BLOCK
    lane_chunk = lax.broadcasted_iota(jnp.int32, (1, span), 1) >> (LANES.bit_length() - 1)
    sw_ref[...] = _nt_dot(q_ref[...].reshape(HEADS_PER_GROUP * qb, HEAD_DIM), kw)
    logits(0, sa_ref)
    for u, rows, own in chains:
        before_start = jnp.where(WIN_CHUNKS - 1 - lane_chunk > tile0 + u, NEG_BIG, 0.0)
        s = sw_ref[rows, u * Q_BLOCK:u * Q_BLOCK + span] + (tw_ref[own, :] + before_start)
        p = jnp.exp2(s - jnp.max(s, axis=1, keepdims=True))
        o = jnp.dot(p.astype(BF16), vw[u * Q_BLOCK:u * Q_BLOCK + span, :], preferred_element_type=F32)
        ow_ref[rows, :] = o / jnp.sum(p, axis=1, keepdims=True)

    far_tiles = jnp.maximum(tile0 - SEL_BAND_BEHIND - 1 + TILE_CHUNKS, 0) // TILE_CHUNKS

    def steps(first, count, biased=True):
        for n in range(0, count, 2):
            step(first + n, sa_ref, sb2_ref, biased)
            step(first + n + 1, sb2_ref, sa_ref, biased)

    def run(first, count, unroll, biased):
        def body(p, carry):
            steps(first + unroll * p, unroll, biased)
            return carry
        lax.fori_loop(0, count, body, 0)

    far_runs = far_tiles // FAR_UNROLL
    run(0, far_runs, FAR_UNROLL, False)
    start = FAR_UNROLL * far_runs
    left = last - start + 1
    runs = (left - 1) // UNROLL
    run(start, runs, UNROLL, True)
    start = start + UNROLL * runs
    left = left - UNROLL * runs

    @pl.when(left > 2)
    def _():
        steps(start, 2)

    @pl.when(left % 2 == 1)
    def _():
        attend(last, sa_ref)

    @pl.when(left % 2 == 0)
    def _():
        step(last - 1, sa_ref, sb2_ref)
        attend(last, sb2_ref)

    gates = jax.nn.sigmoid(gl_ref[...])
    o_sel = _normalized(l_ref, acc_ref)
    for r, rows in enumerate(head_rows):
        cols = slice(r * HEAD_DIM, (r + 1) * HEAD_DIM)
        mix = (gates[:, 3 * r:3 * r + 1] * oc_ref[:, cols] + gates[:, 3 * r + 1:3 * r + 2] * o_sel[rows, :]
               + gates[:, 3 * r + 2:3 * r + 3] * ow_ref[rows, :])
        x_ref[:, cols] = mix.astype(BF16)


def _sel_win(q, selbias, k_aug, kv, t_sel, t_win, o_cmp, gate_logits):
    _, s, _ = q.shape
    g, r4 = N_KV_GROUPS, HEADS_PER_GROUP
    qb = SEL_TILES * Q_BLOCK
    rows = r4 * qb
    table_rows = r4 * Q_BLOCK
    width = r4 * HEAD_DIM
    nselp = selbias.shape[2]
    once = pl.Buffered(1)
    whole = lambda slot: pl.BlockSpec((None, s, HEAD_DIM), lambda gg, i: (slot * g + gg, 0, 0),
                                      pipeline_mode=once)

    def win(slot, behind):
        return pl.BlockSpec((None, Q_BLOCK, HEAD_DIM),
                            lambda gg, i: (slot * g + gg,
                                           jnp.maximum(i * SEL_TILES + SEL_TILES - 1 - behind, 0), 0))

    in_specs = [pl.BlockSpec((r4, qb, HEAD_DIM), lambda gg, i: (gg, i, 0)),
                pl.BlockSpec((None, qb, nselp), lambda gg, i: (gg, i, 0)),
                pl.BlockSpec((None, s, 2 * HEAD_DIM), lambda gg, i: (gg, 0, 0), pipeline_mode=once),
                whole(1),
                pl.BlockSpec((None, SEL_BAND_CHUNKS, table_rows, LANES), lambda gg, i: (gg, 0, 0, 0),
                             pipeline_mode=once),
                pl.BlockSpec((None, table_rows, WINDOW + Q_BLOCK), lambda gg, i: (gg, 0, 0),
                             pipeline_mode=once),
                pl.BlockSpec((qb, width), lambda gg, i: (i, gg)),
                pl.BlockSpec((None, qb, LANES), lambda gg, i: (gg, i, 0))]
    in_specs += [win(2, b) for b in range(WIN_BLOCKS)] + [win(3, b) for b in range(WIN_BLOCKS)]
    per_group = 3 * r4
    group_gates = jnp.stack([jnp.pad(gate_logits[:, per_group * gg:per_group * (gg + 1)],
                                     ((0, 0), (0, LANES - per_group))) for gg in range(g)])
    return pl.pallas_call(
        _sel_win_kernel,
        grid=(g, s // qb),
        in_specs=in_specs,
        out_specs=pl.BlockSpec((qb, width), lambda gg, i: (i, gg)),
        out_shape=jax.ShapeDtypeStruct((s, N_HEADS * HEAD_DIM), BF16),
        scratch_shapes=[pltpu.VMEM((nselp // LANES, rows, 2 * HEAD_DIM), BF16),
                        pltpu.VMEM((rows, KEY_TILE), F32), pltpu.VMEM((rows, KEY_TILE), F32),
                        pltpu.VMEM((rows, WINDOW + qb), F32), pltpu.VMEM((rows, HEAD_DIM), F32),
                        pltpu.VMEM((rows, LANES), F32), pltpu.VMEM((rows, LANES), F32),
                        pltpu.VMEM((rows, HEAD_DIM), F32)],
        compiler_params=_params(("parallel", "arbitrary")),
        name="nsa_selected_window",
    )(q, selbias, k_aug, kv, t_sel, t_win, o_cmp, group_gates, *([kv] * (2 * WIN_BLOCKS)))


def _selection_aggregator(ncp, nselp):
    n = np.arange(ncp)[:, None]
    j = np.arange(nselp)[None, :]
    hit = (n >= SEL_RATIO * j - 1) & (n <= SEL_RATIO * j + SEL_RATIO - 1) & (n < ncp - 1)
    return jnp.asarray(hit.astype(np.float32), dtype=BF16)


def _block_onehot(s):
    blk = (np.arange(s) // SEL_BLOCK) % LANES
    return jnp.asarray((blk[:, None] == np.arange(LANES)[None, :]).astype(np.float32), dtype=BF16)


def _round_up(x, m):
    return (x + m - 1) // m * m


def kernel(x, fox_w_in, fox_b_f, fox_w_o, nsa_w_in, nsa_w_o, kv_w, cmp_pos_k, cmp_pos_v, cmp_k_w1, cmp_k_w2, cmp_v_w1, cmp_v_w2, rel_bias, mlp_w1, mlp_w2, ln1_g, ln1_b, ln2_g, ln2_b):
    b, s, d = x.shape
    assert b == 1 and d == D_MODEL and s % (CMP_STRIDE * LANES) == 0
    hd = N_HEADS * HEAD_DIM
    scale = HEAD_DIM ** -0.5 * LOG2E
    tm = min(512, s)
    tp = min(1024, s)
    fox_t = min(512, s)
    pad_cols = lambda w: jnp.pad(w, ((0, 0), (0, LANES - w.shape[1])))

    fox_in, fox_out = fox_w_in.astype(BF16), fox_w_o.astype(BF16)
    nsa_in, nsa_out = nsa_w_in.astype(BF16), nsa_w_o.astype(BF16)
    w1_all, w2_all = mlp_w1.astype(BF16), mlp_w2.astype(BF16)
    project = functools.partial(_matmul, tm=tp, tn=512, out_dtype=BF16, head_major=True)
    gate_project = functools.partial(_matmul, tm=tp, tn=LANES, out_dtype=F32, head_major=False)

    h = x[0]
    hb = h.astype(BF16)
    kv_state = None
    for layer in range(DEPTH):
        g1, b1 = ln1_g[layer][None, :], ln1_b[layer][None, :]
        g2, b2 = ln2_g[layer][None, :], ln2_b[layer][None, :]
        if layer < N_A_LAYERS:
            qkv = project(hb, fox_in, layer=layer, n=3 * hd, scaled_cols=hd, scale=scale)
            gate_logits = gate_project(hb, pad_cols(fox_in[layer][:, 3 * hd:]))
            cum = _forget_cumsum(gate_logits, fox_b_f[layer])
            cum_rows = cum[:, :N_HEADS].T.reshape(N_HEADS, s // fox_t, 1, fox_t)
            attn = _fox_attention(qkv, cum_rows, _first_fox_tile(qkv, cum, t=fox_t), t=fox_t)
            h, hb = _oproj(attn, fox_out, layer, h, g1, b1, tm=tm)
        else:
            nsa_layer = layer - N_A_LAYERS
            k_cmp_v_cmp, kv_tok, k_aug, tables, agg, n_top = kv_state
            t_sel, t_win, t_cmp = tables
            q = project(hb, nsa_in, layer=nsa_layer, n=hd, scaled_cols=hd, scale=scale)
            gate_logits = gate_project(hb, pad_cols(nsa_in[nsa_layer][:, hd:]))
            o_cmp, selbias = _cmp_select(q, k_cmp_v_cmp, t_cmp, agg, n_top=n_top)
            mixed = _sel_win(q, selbias, k_aug, kv_tok, t_sel, t_win, o_cmp, gate_logits)
            h, hb = _oproj(mixed, nsa_out, nsa_layer, h, g1, b1, tm=tm)
        h, hb = _mlp(hb, w1_all, w2_all, layer, h, g2, b2, tm=tm, tf=1024)
        if layer == N_A_LAYERS - 1:
            gd = N_KV_GROUPS * HEAD_DIM
            kvw = kv_w.astype(BF16)
            kv_raw = project(hb, kvw, n=2 * gd, out_dtype=F32)
            kv_tok = project(hb, kvw, n=4 * gd, first_col=2 * gd)
            pos = jnp.stack([cmp_pos_k.reshape(1, -1), cmp_pos_v.reshape(1, -1)])
            w1 = jnp.stack([cmp_k_w1, cmp_v_w1]).astype(BF16)
            w2 = jnp.stack([cmp_k_w2, cmp_v_w2]).astype(BF16)
            kv_cmp = _compress(kv_raw, pos, w1, w2)
            n_sel = s // SEL_BLOCK
            nselp = _round_up(n_sel, LANES)
            onehot = jnp.broadcast_to(_block_onehot(s), (N_KV_GROUPS, s, LANES))
            k_aug = jnp.concatenate([kv_tok[:N_KV_GROUPS], onehot], axis=-1)
            tables = _bias_tables(_bias_vec(rel_bias))
            agg = _selection_aggregator(s // CMP_STRIDE, nselp)
            kv_state = (kv_cmp, kv_tok, k_aug, tables, agg, min(N_SELECTED, n_sel))
    return h[None]
```
